```python
import jax, jax.numpy as jnp
from jax import lax
import numpy as np

D_MODEL = 2048
BATCH = 4
SEQ = 4096
DEPTH = 2
DEC_BATCH = 8
DEC_SEQ = 16
PAST_LEN = 1024

CHUNK = 64
N_META = 16
N_EVEN = (DEPTH + 1) // 2
N_ODD = DEPTH // 2
MLA_HEADS = 8
Q_LORA = 512
KV_LORA = 512
QK_NOPE = 128
QK_ROPE = 64
V_HEAD = 128
ROPE_THETA = 10000.0
Q_BLOCK = 128
DN_HEADS = 8
DN_DK = 128
DN_DV = 128
DN_CONV = 4
DN_BLOCK = 64
DN_QKV = DN_HEADS * (2 * DN_DK + DN_DV)
IN_WIDTHS = (Q_LORA, KV_LORA, QK_ROPE, DN_QKV, DN_HEADS * DN_DV, DN_HEADS, DN_HEADS)
IN_COLS = Q_LORA + KV_LORA + QK_ROPE + DN_QKV + DN_HEADS * DN_DV + 2 * DN_HEADS
MIX_WIDTH = MLA_HEADS * V_HEAD + DN_HEADS * DN_DV
CF_KERNEL = 31
D_FF = -(-8 * D_MODEL // (3 * 256)) * 256

EPS = 1e-6
NEG_INF = -1e30
BIG_CID = 2 ** 30

kernel_name = 'hybrid_streaming_mla_gdn_conformer_step'


def rmsnorm(x, g):
    xf = x.astype(jnp.float32)
    y = xf * lax.rsqrt(jnp.mean(xf * xf, axis=-1, keepdims=True) + EPS)
    return (y * g.astype(jnp.float32)).astype(x.dtype)


def layernorm(x, g, b):
    xf = x.astype(jnp.float32)
    xc = xf - jnp.mean(xf, axis=-1, keepdims=True)
    y = xc * lax.rsqrt(jnp.mean(xc * xc, axis=-1, keepdims=True) + EPS)
    return (y * g.astype(jnp.float32) + b.astype(jnp.float32)).astype(x.dtype)


def l2norm(x):
    xf = x.astype(jnp.float32)
    return xf * lax.rsqrt(jnp.sum(xf * xf, axis=-1, keepdims=True) + EPS)


def rope(x, pos):
    half = x.shape[-1] // 2
    inv = ROPE_THETA ** (-jnp.arange(half, dtype=jnp.float32) / half)
    ang = pos.astype(jnp.float32)[:, None] * inv[None, :]
    cos = jnp.cos(ang)[:, None, :]
    sin = jnp.sin(ang)[:, None, :]
    xf = x.astype(jnp.float32)
    x1, x2 = xf[..., :half], xf[..., half:]
    return jnp.concatenate([x1 * cos - x2 * sin, x2 * cos + x1 * sin], axis=-1).astype(x.dtype)


def chunk_id(idx):
    return jnp.where(idx < N_META, -1, (idx - N_META) // CHUNK)


def causal_depthwise(x_ext, w):
    return lax.conv_general_dilated(
        x_ext, w[:, None, :].astype(x_ext.dtype), window_strides=(1,), padding='VALID',
        dimension_numbers=('NWC', 'WIO', 'NWC'), feature_group_count=x_ext.shape[-1])


def mla_attention(q_nope, q_rope, ckv_all, krope_all, q_cid, k_cid, w_uk, w_uv):
    b, lq, h, _ = q_nope.shape
    k_nope = jnp.einsum('bkc,chd->bkhd', ckv_all, w_uk)
    v = jnp.einsum('bkc,chd->bkhd', ckv_all, w_uv)
    scale = (QK_NOPE + QK_ROPE) ** -0.5
    qb = min(Q_BLOCK, lq)
    nb = -(-lq // qb)
    pad = nb * qb - lq

    def to_blocks(t):
        t = jnp.pad(t, ((0, 0), (0, pad), (0, 0), (0, 0)))
        return t.reshape(b, nb, qb, h, t.shape[-1]).transpose(1, 0, 2, 3, 4)

    qc = jnp.pad(q_cid, (0, pad), constant_values=BIG_CID).reshape(nb, qb)

    def block(args):
        qn, qr, qcb = args
        s = (jnp.einsum('bqhd,bkhd->bhqk', qn, k_nope)
             + jnp.einsum('bqhd,bkd->bhqk', qr, krope_all)).astype(jnp.float32) * scale
        mask = k_cid[None, :] <= qcb[:, None]
        p = jax.nn.softmax(jnp.where(mask, s, NEG_INF), axis=-1).astype(v.dtype)
        return jnp.einsum('bhqk,bkhd->bqhd', p, v)

    out = lax.map(block, (to_blocks(q_nope), to_blocks(q_rope), qc))
    return out.transpose(1, 0, 2, 3, 4).reshape(b, nb * qb, h * V_HEAD)[:, :lq]


def gated_delta_rule(q, k, v, g, beta, s0):
    b, l, h, dk = q.shape
    dv = v.shape[-1]
    c = DN_BLOCK
    n = -(-l // c)
    pad = n * c - l
    f32 = jnp.float32

    def blk4(t):
        t = jnp.pad(t.astype(f32), ((0, 0), (0, pad), (0, 0), (0, 0)))
        return t.reshape(b, n, c, h, t.shape[-1]).transpose(1, 0, 3, 2, 4)

    def blk3(t):
        t = jnp.pad(t.astype(f32), ((0, 0), (0, pad), (0, 0)))
        return t.reshape(b, n, c, h).transpose(1, 0, 3, 2)

    qc, kc, vc = blk4(q), blk4(k), blk4(v)
    gcum = jnp.cumsum(blk3(g), axis=-1)
    bc = blk3(beta)
    idx = jnp.arange(c)
    incl = idx[:, None] >= idx[None, :]
    strict = idx[:, None] > idx[None, :]
    diff = gcum[..., :, None] - gcum[..., None, :]
    decay = jnp.where(incl, jnp.exp(jnp.where(incl, diff, 0.0)), 0.0)
    kb = kc * bc[..., None]
    m = jnp.where(strict, jnp.einsum('nbhid,nbhjd->nbhij', kb, kc) * decay, 0.0)
    a = m + jnp.eye(c, dtype=f32)
    u = lax.linalg.triangular_solve(a, vc * bc[..., None], left_side=True, lower=True,
                                    unit_diagonal=True)
    w = lax.linalg.triangular_solve(a, kb * jnp.exp(gcum)[..., None], left_side=True, lower=True,
                                    unit_diagonal=True)
    attn = jnp.einsum('nbhid,nbhjd->nbhij', qc, kc) * decay
    q_dec = qc * jnp.exp(gcum)[..., None]
    k_dec = kc * jnp.exp(gcum[..., -1:] - gcum)[..., None]
    blk_decay = jnp.exp(gcum[..., -1])

    def step(s, xs):
        u_, w_, attn_, qd_, kd_, bd_ = xs
        v_new = u_ - jnp.einsum('bhcd,bhde->bhce', w_, s)
        o = jnp.einsum('bhcd,bhde->bhce', qd_, s) + jnp.einsum('bhij,bhje->bhie', attn_, v_new)
        s = s * bd_[..., None, None] + jnp.einsum('bhcd,bhce->bhde', kd_, v_new)
        return s, o

    s_fin, o = lax.scan(step, s0.astype(f32), (u, w, attn, q_dec, k_dec, blk_decay))
    o = o.transpose(1, 0, 3, 2, 4).reshape(b, n * c, h, dv)[:, :l]
    return o, s_fin.astype(s0.dtype)


def hybrid_mixer(h, pos, q_cid, k_cid, hist_ckv, hist_krope, s0, conv_hist,
                 w_in, mla_gq, mla_gkv, w_uq, w_uk, w_uv,
                 dn_conv_w, dn_a_log, dn_dt_bias, dn_norm_w, w_out):
    b, l, _ = h.shape
    proj = h @ w_in
    offs = [int(o) for o in np.cumsum(IN_WIDTHS)[:-1]]
    x_qd, x_kvd, x_kr, x_qkv, x_z, x_a, x_b = jnp.split(proj, offs, axis=-1)

    cq = rmsnorm(x_qd, mla_gq)
    q = jnp.einsum('blc,chd->blhd', cq, w_uq)
    q_nope = q[..., :QK_NOPE]
    q_rope = rope(q[..., QK_NOPE:], pos)
    ckv = rmsnorm(x_kvd, mla_gkv)
    krope = rope(x_kr[:, :, None, :], pos)[:, :, 0, :]
    ckv_all = jnp.concatenate([hist_ckv, ckv], axis=1)
    krope_all = jnp.concatenate([hist_krope, krope], axis=1)
    y_mla = mla_attention(q_nope, q_rope, ckv_all, krope_all, q_cid, k_cid, w_uk, w_uv)

    qkv_ext = jnp.concatenate([conv_hist, x_qkv], axis=1)
    qkv = jax.nn.silu(causal_depthwise(qkv_ext, dn_conv_w))
    new_conv = qkv_ext[:, -(DN_CONV - 1):]
    dq, dk_, dv_ = jnp.split(qkv, [DN_HEADS * DN_DK, 2 * DN_HEADS * DN_DK], axis=-1)
    dq = l2norm(dq.reshape(b, l, DN_HEADS, DN_DK)) * (DN_DK ** -0.5)
    dk_ = l2norm(dk_.reshape(b, l, DN_HEADS, DN_DK))
    dv_ = dv_.reshape(b, l, DN_HEADS, DN_DV)
    g = -jnp.exp(dn_a_log.astype(jnp.float32)) * jax.nn.softplus(
        x_a.astype(jnp.float32) + dn_dt_bias.astype(jnp.float32))
    beta = jax.nn.sigmoid(x_b.astype(jnp.float32))
    o, s_new = gated_delta_rule(dq, dk_, dv_, g, beta, s0)
    z = x_z.reshape(b, l, DN_HEADS, DN_DV)
    y_dn = (rmsnorm(o.astype(h.dtype), dn_norm_w) * jax.nn.silu(z)).reshape(b, l, DN_HEADS * DN_DV)

    y = jnp.concatenate([y_mla, y_dn], axis=-1) @ w_out
    return y, ckv, krope, s_new, new_conv


def conformer_conv(h, conv_hist, w_pw1, b_pw1, w_dw, b_dw, ln_g, ln_b, w_pw2, b_pw2):
    u = h @ w_pw1 + b_pw1
    ua, ug = jnp.split(u, 2, axis=-1)
    u = ua * jax.nn.sigmoid(ug)
    u_ext = jnp.concatenate([conv_hist, u], axis=1)
    c = causal_depthwise(u_ext, w_dw) + b_dw
    c = layernorm(c, ln_g, ln_b)
    c = c * jax.nn.sigmoid(c)
    return c @ w_pw2 + b_pw2, u_ext[:, -(CF_KERNEL - 1):]


def swiglu(h, wg, wu, wd):
    return (jax.nn.silu(h @ wg) * (h @ wu)) @ wd


def setup_inputs(seed: int = 0) -> dict:
    key = jax.random.key(seed)
    ks = list(jax.random.split(key, 40))

    def nrm(i, shape, scale):
        return jax.random.normal(ks[i], shape, jnp.float32) * scale

    def gain(i, shape):
        return 1.0 + nrm(i, shape, 0.02)

    dt = jnp.exp(jax.random.uniform(ks[30], (N_EVEN, DN_HEADS), jnp.float32,
                                    np.log(1e-3), np.log(1e-1)))
    return {
        'x_prompt': nrm(0, (BATCH, SEQ, D_MODEL), 1.0),
        'x_sample': nrm(1, (DEC_BATCH, DEC_SEQ, D_MODEL), 1.0),
        'cache_mla_ckv': nrm(2, (N_EVEN, DEC_BATCH, N_META + PAST_LEN, KV_LORA), 1.0),
        'cache_mla_krope': nrm(3, (N_EVEN, DEC_BATCH, N_META + PAST_LEN, QK_ROPE), 1.0),
        'state_dn_s': nrm(4, (N_EVEN, DEC_BATCH, DN_HEADS, DN_DK, DN_DV), DN_DK ** -0.5),
        'state_dn_conv': nrm(5, (N_EVEN, DEC_BATCH, DN_CONV - 1, DN_QKV), 1.0),
        'state_cf_conv': nrm(6, (N_ODD, DEC_BATCH, CF_KERNEL - 1, D_MODEL), 0.5),
        'meta_tokens': nrm(7, (N_META, D_MODEL), 1.0),
        'norm_gains': gain(8, (DEPTH, 4, D_MODEL)),
        'w_in': nrm(9, (N_EVEN, D_MODEL, IN_COLS), D_MODEL ** -0.5),
        'mla_gq': gain(10, (N_EVEN, Q_LORA)),
        'mla_gkv': gain(11, (N_EVEN, KV_LORA)),
        'w_uq': nrm(12, (N_EVEN, Q_LORA, MLA_HEADS, QK_NOPE + QK_ROPE), Q_LORA ** -0.5),
        'w_uk': nrm(13, (N_EVEN, KV_LORA, MLA_HEADS, QK_NOPE), KV_LORA ** -0.5),
        'w_uv': nrm(14, (N_EVEN, KV_LORA, MLA_HEADS, V_HEAD), KV_LORA ** -0.5),
        'dn_conv_w': nrm(15, (N_EVEN, DN_CONV, DN_QKV), DN_CONV ** -0.5),
        'dn_a_log': jnp.log(jax.random.uniform(ks[16], (N_EVEN, DN_HEADS), jnp.float32, 1.0, 16.0)),
        'dn_dt_bias': dt + jnp.log(-jnp.expm1(-dt)),
        'dn_norm_w': gain(17, (N_EVEN, DN_DV)),
        'w_out': nrm(18, (N_EVEN, MIX_WIDTH, D_MODEL), MIX_WIDTH ** -0.5),
        'cf_w_pw1': nrm(19, (N_ODD, D_MODEL, 2 * D_MODEL), D_MODEL ** -0.5),
        'cf_b_pw1': nrm(20, (N_ODD, 2 * D_MODEL), 0.02),
        'cf_w_dw': nrm(21, (N_ODD, CF_KERNEL, D_MODEL), CF_KERNEL ** -0.5),
        'cf_b_dw': nrm(22, (N_ODD, D_MODEL), 0.02),
        'cf_ln_g': gain(23, (N_ODD, D_MODEL)),
        'cf_ln_b': nrm(24, (N_ODD, D_MODEL), 0.02),
        'cf_w_pw2': nrm(25, (N_ODD, D_MODEL, D_MODEL), D_MODEL ** -0.5),
        'cf_b_pw2': nrm(26, (N_ODD, D_MODEL), 0.02),
        'w_gate': nrm(27, (DEPTH, D_MODEL, D_FF), D_MODEL ** -0.5),
        'w_up': nrm(28, (DEPTH, D_MODEL, D_FF), D_MODEL ** -0.5),
        'w_down': nrm(29, (DEPTH, D_FF, D_MODEL), D_FF ** -0.5),
    }


def reference(x_prompt, x_sample, cache_mla_ckv, cache_mla_krope, state_dn_s, state_dn_conv,
              state_cf_conv, meta_tokens, norm_gains, w_in, mla_gq, mla_gkv, w_uq, w_uk, w_uv,
              dn_conv_w, dn_a_log, dn_dt_bias, dn_norm_w, w_out, cf_w_pw1, cf_b_pw1, cf_w_dw,
              cf_b_dw, cf_ln_g, cf_ln_b, cf_w_pw2, cf_b_pw2, w_gate, w_up, w_down):
    dt = x_prompt.dtype
    bp = x_prompt.shape[0]
    bs, ls = x_sample.shape[0], x_sample.shape[1]
    past = cache_mla_ckv.shape[2] - N_META

    hp = jnp.concatenate([jnp.broadcast_to(meta_tokens.astype(dt), (bp, N_META, D_MODEL)), x_prompt], axis=1)
    lp = hp.shape[1]
    pos_p = jnp.arange(lp)
    cid_p = chunk_id(pos_p)
    k_idx_s = jnp.arange(N_META + past + ls)
    cid_ks = chunk_id(k_idx_s)
    cid_qs = cid_ks[-ls:]
    pos_s = k_idx_s[-ls:]
    hs = x_sample

    p_ckv, p_kr, p_s, p_conv, p_cf = [], [], [], [], []
    s_ckv, s_kr, s_s, s_conv, s_cf = [], [], [], [], []
    for layer in range(DEPTH):
        ng = norm_gains[layer]
        if layer % 2 == 0:
            e = layer // 2
            ew = (w_in[e], mla_gq[e], mla_gkv[e], w_uq[e], w_uk[e], w_uv[e],
                  dn_conv_w[e], dn_a_log[e], dn_dt_bias[e], dn_norm_w[e], w_out[e])
            mp, ckv_p, kr_p, st_p, cv_p = hybrid_mixer(
                rmsnorm(hp, ng[0]), pos_p, cid_p, cid_p,
                jnp.zeros((bp, 0, KV_LORA), dt), jnp.zeros((bp, 0, QK_ROPE), dt),
                jnp.zeros((bp, DN_HEADS, DN_DK, DN_DV), dt), jnp.zeros((bp, DN_CONV - 1, DN_QKV), dt), *ew)
            ms, ckv_s, kr_s, st_s, cv_s = hybrid_mixer(
                rmsnorm(hs, ng[0]), pos_s, cid_qs, cid_ks,
                cache_mla_ckv[e], cache_mla_krope[e], state_dn_s[e], state_dn_conv[e], *ew)
            p_ckv.append(ckv_p); p_kr.append(kr_p); p_s.append(st_p); p_conv.append(cv_p)
            s_ckv.append(ckv_s); s_kr.append(kr_s); s_s.append(st_s); s_conv.append(cv_s)
        else:
            o = layer // 2
            cw = (cf_w_pw1[o], cf_b_pw1[o], cf_w_dw[o], cf_b_dw[o], cf_ln_g[o], cf_ln_b[o],
                  cf_w_pw2[o], cf_b_pw2[o])
            mp, cf_p = conformer_conv(rmsnorm(hp, ng[0]), jnp.zeros((bp, CF_KERNEL - 1, D_MODEL), dt), *cw)
            ms, cf_s = conformer_conv(rmsnorm(hs, ng[0]), state_cf_conv[o], *cw)
            p_cf.append(cf_p); s_cf.append(cf_s)
        hp = hp + rmsnorm(mp, ng[1])
        hs = hs + rmsnorm(ms, ng[1])
        fw = (w_gate[layer], w_up[layer], w_down[layer])
        hp = hp + rmsnorm(swiglu(rmsnorm(hp, ng[2]), *fw), ng[3])
        hs = hs + rmsnorm(swiglu(rmsnorm(hs, ng[2]), *fw), ng[3])

    y_prompt = hp[:, N_META:]
    y_sample = hs
    return (y_prompt, y_sample,
            jnp.stack(p_ckv), jnp.stack(p_kr), jnp.stack(p_s), jnp.stack(p_conv), jnp.stack(p_cf),
            jnp.stack(s_ckv), jnp.stack(s_kr), jnp.stack(s_s), jnp.stack(s_conv), jnp.stack(s_cf))
```

```python
import functools

import numpy as np
import jax
import jax.numpy as jnp
from jax import lax
from jax.experimental import pallas as pl
from jax.experimental.pallas import tpu as pltpu

F32 = jnp.float32
BF16 = jnp.bfloat16

EPS = 1e-6
NEG_INF = -1e30
CHUNK = 64
N_META = 16
MLA_HEADS = 8
Q_LORA = 512
KV_LORA = 512
QK_NOPE = 128
QK_ROPE = 64
V_HEAD = 128
ROPE_THETA = 10000.0
DN_HEADS = 8
DN_DK = 128
DN_DV = 128
DN_CONV = 4
DN_QKV = DN_HEADS * (2 * DN_DK + DN_DV)
CF_KERNEL = 31

LANE = 128
VMEM_LIMIT = 56 * 1024 * 1024

COL_QKV = 0
COL_Z = DN_QKV
COL_QD = COL_Z + DN_HEADS * DN_DV
COL_KVD = COL_QD + Q_LORA
COL_SMALL = COL_KVD + KV_LORA
SMALL_W = 256
PROJ_W = COL_SMALL + SMALL_W


def _cparams(sem):
    return pltpu.CompilerParams(dimension_semantics=sem, vmem_limit_bytes=VMEM_LIMIT)


def _rms(x, g):
    return x * lax.rsqrt(jnp.mean(x * x, axis=-1, keepdims=True) + EPS) * g


def _sigmoid(x):
    return 1.0 / (1.0 + jnp.exp(-x))


def _dot(a, b):
    return jnp.dot(a, b, preferred_element_type=F32)


def _dot_nt(a, b):
    return lax.dot_general(a, b, (((1,), (1,)), ((), ())), preferred_element_type=F32)


def _dot_tn(a, b):
    return lax.dot_general(a, b, (((0,), (0,)), ((), ())), preferred_element_type=F32)


def _row_tile(m, pref):
    t = min(pref, m)
    while m % t:
        t //= 2
    return t


def _norm_mm_kernel(x_ref, g_ref, w_ref, b_ref, o_ref, xn_ref):
    @pl.when(pl.program_id(1) == 0)
    def _():
        xn_ref[...] = _rms(x_ref[...], g_ref[...]).astype(BF16)

    o_ref[...] = (_dot(xn_ref[...], w_ref[...]) + b_ref[...]).astype(o_ref.dtype)


def _norm_glu_kernel(x_ref, g_ref, wa_ref, wg_ref, ba_ref, bg_ref, o_ref, xn_ref):
    @pl.when(pl.program_id(1) == 0)
    def _():
        xn_ref[...] = _rms(x_ref[...], g_ref[...]).astype(BF16)

    xn = xn_ref[...]
    a = _dot(xn, wa_ref[...]) + ba_ref[...]
    gt = _dot(xn, wg_ref[...]) + bg_ref[...]
    o_ref[...] = (a * _sigmoid(gt)).astype(o_ref.dtype)


def norm_matmul(x, g, w, b, tm, tn, out_dtype=F32):
    m, k = x.shape
    n = w.shape[1]
    tm = _row_tile(m, tm)
    return pl.pallas_call(
        _norm_mm_kernel,
        grid=(m // tm, n // tn),
        in_specs=[pl.BlockSpec((tm, k), lambda i, j: (i, 0)),
                  pl.BlockSpec((1, k), lambda i, j: (0, 0)),
                  pl.BlockSpec((k, tn), lambda i, j: (0, j)),
                  pl.BlockSpec((1, tn), lambda i, j: (0, j))],
        out_specs=pl.BlockSpec((tm, tn), lambda i, j: (i, j)),
        out_shape=jax.ShapeDtypeStruct((m, n), out_dtype),
        scratch_shapes=[pltpu.VMEM((tm, k), BF16)],
        compiler_params=_cparams(("parallel", "arbitrary")),
        name="norm_matmul",
    )(x, g, w, b)


def norm_glu(x, g, w, b, tm, tn):
    m, k = x.shape
    n = w.shape[1] // 2
    tm = _row_tile(m, tm)
    nb = n // tn
    return pl.pallas_call(
        _norm_glu_kernel,
        grid=(m // tm, nb),
        in_specs=[pl.BlockSpec((tm, k), lambda i, j: (i, 0)),
                  pl.BlockSpec((1, k), lambda i, j: (0, 0)),
                  pl.BlockSpec((k, tn), lambda i, j: (0, j)),
                  pl.BlockSpec((k, tn), lambda i, j: (0, j + nb)),
                  pl.BlockSpec((1, tn), lambda i, j: (0, j)),
                  pl.BlockSpec((1, tn), lambda i, j: (0, j + nb))],
        out_specs=pl.BlockSpec((tm, tn), lambda i, j: (i, j)),
        out_shape=jax.ShapeDtypeStruct((m, n), F32),
        scratch_shapes=[pltpu.VMEM((tm, k), BF16)],
        compiler_params=_cparams(("parallel", "arbitrary")),
        name="norm_glu",
    )(x, g, w, w, b, b)


def _mm_resnorm_kernel(*refs, n_in):
    xs = refs[:n_in]
    ws = refs[n_in:2 * n_in]
    b_ref, g_ref, res_ref, o_ref = refs[2 * n_in:]
    y = b_ref[...]
    for x_ref, w_ref in zip(xs, ws):
        y = y + _dot(x_ref[...], w_ref[...])
    o_ref[...] = res_ref[...] + _rms(y, g_ref[...])


def matmul_resnorm(xs, ws, b, g, res, tm):
    m, n = res.shape
    tm = _row_tile(m, tm)
    n_in = len(xs)
    in_specs = ([pl.BlockSpec((tm, x.shape[1]), lambda i: (i, 0)) for x in xs]
                + [pl.BlockSpec(w.shape, lambda i: (0, 0)) for w in ws]
                + [pl.BlockSpec((1, n), lambda i: (0, 0)),
                   pl.BlockSpec((1, n), lambda i: (0, 0)),
                   pl.BlockSpec((tm, n), lambda i: (i, 0))])
    return pl.pallas_call(
        functools.partial(_mm_resnorm_kernel, n_in=n_in),
        grid=(m // tm,),
        in_specs=in_specs,
        out_specs=pl.BlockSpec((tm, n), lambda i: (i, 0)),
        out_shape=jax.ShapeDtypeStruct((m, n), F32),
        compiler_params=_cparams(("parallel",)),
        name="matmul_resnorm",
    )(*xs, *ws, b, g, res)


def _ffn_kernel(h_ref, g2_ref, wg_ref, wu_ref, wd_ref, g3_ref, o_ref, xn_ref, acc_ref):
    j = pl.program_id(1)

    @pl.when(j == 0)
    def _():
        xn_ref[...] = _rms(h_ref[...], g2_ref[...]).astype(BF16)
        acc_ref[...] = jnp.zeros_like(acc_ref)

    xn = xn_ref[...]
    gate = _dot(xn, wg_ref[...])
    up = _dot(xn, wu_ref[...])
    a = (gate * _sigmoid(gate) * up).astype(BF16)
    acc_ref[...] += _dot(a, wd_ref[...])

    @pl.when(j == pl.num_programs(1) - 1)
    def _():
        o_ref[...] = h_ref[...] + _rms(acc_ref[...], g3_ref[...])


def ffn(h, g2, wg, wu, wd, g3, tm, tf):
    m, d = h.shape
    f = wg.shape[1]
    tm = _row_tile(m, tm)
    return pl.pallas_call(
        _ffn_kernel,
        grid=(m // tm, f // tf),
        in_specs=[pl.BlockSpec((tm, d), lambda i, j: (i, 0)),
                  pl.BlockSpec((1, d), lambda i, j: (0, 0)),
                  pl.BlockSpec((d, tf), lambda i, j: (0, j)),
                  pl.BlockSpec((d, tf), lambda i, j: (0, j)),
                  pl.BlockSpec((tf, d), lambda i, j: (j, 0)),
                  pl.BlockSpec((1, d), lambda i, j: (0, 0))],
        out_specs=pl.BlockSpec((tm, d), lambda i, j: (i, 0)),
        out_shape=jax.ShapeDtypeStruct((m, d), F32),
        scratch_shapes=[pltpu.VMEM((tm, d), BF16), pltpu.VMEM((tm, d), F32)],
        compiler_params=_cparams(("parallel", "arbitrary")),
        name="ffn",
    )(h, g2, wg, wu, wd, g3)


def _mla_prep_kernel(qd_ref, kvd_ref, sm_ref, gq_ref, gkv_ref, wq_ref, wkv_ref, cq_ref, sq_ref, ck_ref,
                     qn_ref, qr_ref, ckv_ref, kn_ref, v_ref, kro_ref, krp_ref):
    hw = MLA_HEADS * QK_NOPE
    scale = (QK_NOPE + QK_ROPE) ** -0.5
    cq = _rms(qd_ref[...], gq_ref[...]).astype(BF16)
    q = _dot(cq, wq_ref[...])
    qn_ref[...] = (q[:, :hw] * scale).astype(BF16)
    cos8 = jnp.tile(cq_ref[...], (1, MLA_HEADS))
    sin8 = jnp.tile(sq_ref[...], (1, MLA_HEADS))
    qr_ref[...] = ((q[:, hw:2 * hw] * cos8 + q[:, 2 * hw:] * sin8) * scale).astype(BF16)
    ckv = _rms(kvd_ref[...], gkv_ref[...])
    ckv_ref[...] = ckv
    kv = _dot(ckv.astype(BF16), wkv_ref[...])
    kn_ref[...] = kv[:, :hw].astype(BF16)
    v_ref[...] = kv[:, hw:].astype(BF16)
    y = sm_ref[:, :LANE] * ck_ref[...]
    kro = y + pltpu.roll(y, QK_ROPE, 1)
    kro_ref[...] = kro[:, :QK_ROPE]
    lane = lax.broadcasted_iota(jnp.int32, kro.shape, 1)
    krp_ref[...] = jnp.where(lane < QK_ROPE, kro, 0.0).astype(BF16)


def mla_prep(proj, gq, gkv, wq, wkv, cq_tab, sq_tab, ck_tab, tm):
    m = proj.shape[0]
    tm = _row_tile(min(m, cq_tab.shape[0]), tm)
    nt = cq_tab.shape[0] // tm
    hw = MLA_HEADS * QK_NOPE
    tab = lambda: pl.BlockSpec((tm, LANE), lambda i: (i % nt, 0))
    full = lambda a: pl.BlockSpec(a.shape, lambda i: (0, 0))
    return pl.pallas_call(
        _mla_prep_kernel,
        grid=(m // tm,),
        in_specs=[pl.BlockSpec((tm, Q_LORA), lambda i: (i, COL_QD // Q_LORA)),
                  pl.BlockSpec((tm, KV_LORA), lambda i: (i, COL_KVD // KV_LORA)),
                  pl.BlockSpec((tm, SMALL_W), lambda i: (i, COL_SMALL // SMALL_W)),
                  full(gq), full(gkv), full(wq), full(wkv), tab(), tab(), tab()],
        out_specs=[pl.BlockSpec((tm, hw), lambda i: (i, 0)),
                   pl.BlockSpec((tm, hw), lambda i: (i, 0)),
                   pl.BlockSpec((tm, KV_LORA), lambda i: (i, 0)),
                   pl.BlockSpec((tm, hw), lambda i: (i, 0)),
                   pl.BlockSpec((tm, hw), lambda i: (i, 0)),
                   pl.BlockSpec((tm, QK_ROPE), lambda i: (i, 0)),
                   pl.BlockSpec((tm, LANE), lambda i: (i, 0))],
        out_shape=[jax.ShapeDtypeStruct((m, hw), BF16),
                   jax.ShapeDtypeStruct((m, hw), BF16),
                   jax.ShapeDtypeStruct((m, KV_LORA), F32),
                   jax.ShapeDtypeStruct((m, hw), BF16),
                   jax.ShapeDtypeStruct((m, hw), BF16),
                   jax.ShapeDtypeStruct((m, QK_ROPE), F32),
                   jax.ShapeDtypeStruct((m, LANE), BF16)],
        compiler_params=_cparams(("parallel",)),
        name="mla_prep",
    )(proj, proj, proj, gq, gkv, wq, wkv, cq_tab, sq_tab, ck_tab)


def _kv_up_kernel(ckv_ref, wkv_ref, kn_ref, v_ref):
    hw = MLA_HEADS * QK_NOPE
    kv = _dot(ckv_ref[...].astype(BF16), wkv_ref[...])
    kn_ref[...] = kv[:, :hw].astype(BF16)
    v_ref[...] = kv[:, hw:].astype(BF16)


def kv_up(ckv, wkv, tm):
    m = ckv.shape[0]
    tm = _row_tile(m, tm)
    hw = MLA_HEADS * QK_NOPE
    return pl.pallas_call(
        _kv_up_kernel,
        grid=(m // tm,),
        in_specs=[pl.BlockSpec((tm, KV_LORA), lambda i: (i, 0)),
                  pl.BlockSpec(wkv.shape, lambda i: (0, 0))],
        out_specs=[pl.BlockSpec((tm, hw), lambda i: (i, 0)),
                   pl.BlockSpec((tm, hw), lambda i: (i, 0))],
        out_shape=[jax.ShapeDtypeStruct((m, hw), BF16), jax.ShapeDtypeStruct((m, hw), BF16)],
        compiler_params=_cparams(("parallel",)),
        name="kv_up",
    )(ckv, wkv)


def _attn_kernel(hlen_ref, qn_ref, qr_ref, kn_ref, kr_ref, v_ref, hkn_ref, hkr_ref, hv_ref, o_ref,
                 m_ref, l_ref, acc_ref, *, tile, lh):
    s_idx = pl.program_id(0)
    qt = pl.program_id(1)
    hlen = hlen_ref[s_idx]
    nh = MLA_HEADS

    def head_q(h):
        hs = slice(h * LANE, (h + 1) * LANE)
        return jnp.concatenate([qn_ref[:, hs], qr_ref[:, hs]], axis=1)

    def update(h, s, valid, v_h, first):
        hs = slice(h * LANE, (h + 1) * LANE)
        if valid is not None:
            s = jnp.where(valid, s, NEG_INF)
        m_cur = jnp.max(s, axis=1, keepdims=True)
        if first:
            m_new = jnp.broadcast_to(m_cur, (tile, LANE))
        else:
            m_prev = m_ref[:, hs]
            m_new = jnp.maximum(m_prev, m_cur)
        p = jnp.exp(s - m_new[:, :1])
        if valid is not None:
            p = jnp.where(valid, p, 0.0)
        l_cur = jnp.sum(p, axis=1, keepdims=True)
        pv = _dot(p.astype(BF16), v_h)
        if first:
            l_ref[:, hs] = jnp.broadcast_to(l_cur, (tile, LANE))
            acc_ref[:, hs] = pv
        else:
            alpha = jnp.exp(m_prev - m_new)
            l_ref[:, hs] = alpha * l_ref[:, hs] + l_cur
            acc_ref[:, hs] = alpha * acc_ref[:, hs] + pv
        m_ref[:, hs] = m_new

    hcol = lax.broadcasted_iota(jnp.int32, (tile, lh), 1)
    hvalid = hcol < hlen
    hkr = hkr_ref[...]
    for h in range(nh):
        hs = slice(h * LANE, (h + 1) * LANE)
        k_h = jnp.concatenate([hkn_ref[:, hs], hkr], axis=1)
        update(h, _dot_nt(head_q(h), k_h), hvalid, hv_ref[:, hs], True)

    def own_tile(kt, valid):
        off = pl.multiple_of(kt * tile, tile)
        kr = kr_ref[pl.ds(off, tile), :]
        for h in range(nh):
            hs = slice(h * LANE, (h + 1) * LANE)
            k_h = jnp.concatenate([kn_ref[pl.ds(off, tile), hs], kr], axis=1)
            update(h, _dot_nt(head_q(h), k_h), valid, v_ref[pl.ds(off, tile), hs], False)

    def body(kt, carry):
        own_tile(kt, None)
        return carry

    lax.fori_loop(0, qt, body, 0)
    if tile > CHUNK:
        row = lax.broadcasted_iota(jnp.int32, (tile, tile), 0) // CHUNK
        col = lax.broadcasted_iota(jnp.int32, (tile, tile), 1) // CHUNK
        dvalid = col <= row
    else:
        dvalid = None
    own_tile(qt, dvalid)

    for h in range(nh):
        hs = slice(h * LANE, (h + 1) * LANE)
        o_ref[:, hs] = (acc_ref[:, hs] / l_ref[:, hs]).astype(o_ref.dtype)


def attention(qn, qr, kn, krp, v, hkn, hkrp, hv, hlen, nseq, seq_len, tile):
    hw = MLA_HEADS * LANE
    tile = min(tile, seq_len)
    assert seq_len % tile == 0 and (tile % CHUNK == 0 or seq_len == tile <= CHUNK)
    nqt = seq_len // tile
    nhs, lh = hkn.shape[0], hkn.shape[1]
    assert nhs in (1, nseq)
    hidx = (lambda s, q, hl: (s, 0, 0)) if nhs > 1 else (lambda s, q, hl: (0, 0, 0))
    grid_spec = pltpu.PrefetchScalarGridSpec(
        num_scalar_prefetch=1,
        grid=(nseq, nqt),
        in_specs=[pl.BlockSpec((tile, hw), lambda s, q, hl: (s * nqt + q, 0)),
                  pl.BlockSpec((tile, hw), lambda s, q, hl: (s * nqt + q, 0)),
                  pl.BlockSpec((seq_len, hw), lambda s, q, hl: (s, 0)),
                  pl.BlockSpec((seq_len, LANE), lambda s, q, hl: (s, 0)),
                  pl.BlockSpec((seq_len, hw), lambda s, q, hl: (s, 0)),
                  pl.BlockSpec((None, lh, hw), hidx),
                  pl.BlockSpec((None, lh, LANE), hidx),
                  pl.BlockSpec((None, lh, hw), hidx)],
        out_specs=pl.BlockSpec((tile, hw), lambda s, q, hl: (s * nqt + q, 0)),
        scratch_shapes=[pltpu.VMEM((tile, hw), F32), pltpu.VMEM((tile, hw), F32),
                        pltpu.VMEM((tile, hw), F32)],
    )
    return pl.pallas_call(
        functools.partial(_attn_kernel, tile=tile, lh=lh),
        grid_spec=grid_spec,
        out_shape=jax.ShapeDtypeStruct((nseq * seq_len, hw), BF16),
        compiler_params=_cparams(("parallel", "arbitrary")),
        name="attention",
    )(hlen, qn, qr, kn, krp, v, hkn, hkrp, hv)


def _unit_lower_inverse(mm, c):
    r = lax.broadcasted_iota(jnp.int32, (c, c), 0)
    q = lax.broadcasted_iota(jnp.int32, (c, c), 1)
    t = jnp.where(r == q, 1.0, 0.0) - jnp.where((r ^ q) == 1, mm, 0.0)
    s = 2
    while s < c:
        sh = s.bit_length() - 1
        e = jnp.where(((r >> sh) ^ (q >> sh)) == 1, mm, 0.0)
        tb = t.astype(BF16)
        t = t - _dot(tb, _dot(e.astype(BF16), tb).astype(BF16))
        s *= 2
    return t


def _dn_kernel(x_ref, z_ref, sm_ref, cw_ref, hist_ref, s0_ref, ab_ref, nw_ref, y_ref, sfin_ref,
               ext_ref, s_ref, *, c):
    j = pl.program_id(1)
    nh, dk, dv = DN_HEADS, DN_DK, DN_DV
    kw = nh * dk

    @pl.when(j == 0)
    def _():
        ext_ref[0:8, :] = hist_ref[...]
        s_ref[...] = s0_ref[...]

    ext_ref[8:8 + c, :] = x_ref[...]
    conv = cw_ref[0:1, :] * ext_ref[5:5 + c, :]
    for t in range(1, DN_CONV):
        conv = conv + cw_ref[t:t + 1, :] * ext_ref[5 + t:5 + t + c, :]
    ext_ref[0:8, :] = ext_ref[c:c + 8, :]
    act = conv * _sigmoid(conv)

    gates = sm_ref[:, LANE:2 * LANE]
    xa = gates + ab_ref[1:2, :]
    softplus = jnp.maximum(xa, 0.0) + jnp.log(1.0 + jnp.exp(-jnp.abs(xa)))
    g_all = -jnp.exp(ab_ref[0:1, :]) * softplus
    beta_all = _sigmoid(gates)

    r = lax.broadcasted_iota(jnp.int32, (c, c), 0)
    q = lax.broadcasted_iota(jnp.int32, (c, c), 1)
    incl = r >= q
    strict = r > q
    tri = jnp.where(incl, 1.0, 0.0)
    gc = jnp.dot(tri, g_all, preferred_element_type=F32, precision=lax.Precision.HIGHEST)
    if c < LANE:
        gc_sq = jnp.concatenate([gc, jnp.zeros((LANE - c, LANE), F32)], axis=0)
    else:
        gc_sq = gc
    gc_t = gc_sq.T
    egc = jnp.exp(gc)
    glast = gc[c - 1:c, :]
    edl = jnp.exp(glast - gc)
    ebd = jnp.exp(glast)

    for h in range(nh):
        hs = slice(h * dk, (h + 1) * dk)
        qh = act[:, hs]
        kh = act[:, kw + h * dk:kw + (h + 1) * dk]
        vh = act[:, 2 * kw + h * dv:2 * kw + (h + 1) * dv]
        qh = qh * (lax.rsqrt(jnp.sum(qh * qh, axis=1, keepdims=True) + EPS) * dk ** -0.5)
        kh = kh * lax.rsqrt(jnp.sum(kh * kh, axis=1, keepdims=True) + EPS)
        bcol = beta_all[:, 8 + h:9 + h]
        gcol = gc[:, h:h + 1]
        grow = gc_t[h:h + 1, :c]
        decay = jnp.where(incl, jnp.exp(jnp.where(incl, gcol - grow, 0.0)), 0.0)
        kb = kh * bcol
        kq = _dot_nt(jnp.concatenate([kb, qh], axis=0).astype(BF16), kh.astype(BF16))
        mm = jnp.where(strict, kq[:c] * decay, 0.0)
        attn = kq[c:] * decay
        tinv = _unit_lower_inverse(mm, c)
        ecol = egc[:, h:h + 1]
        rhs = jnp.concatenate([vh * bcol, kb * ecol], axis=1).astype(BF16)
        uw = _dot(tinv.astype(BF16), rhs)
        u, w = uw[:, :dv], uw[:, dv:]
        s_old = s_ref[h]
        rr = _dot(jnp.concatenate([w, qh * ecol], axis=0).astype(BF16), s_old.astype(BF16))
        v_new = u - rr[:c]
        vnb = v_new.astype(BF16)
        o = rr[c:] + _dot(attn.astype(BF16), vnb)
        k_dec = (kh * edl[:, h:h + 1]).astype(BF16)
        s_ref[h] = s_old * ebd[:, h:h + 1] + _dot_tn(k_dec, vnb)
        zh = z_ref[:, hs]
        y_ref[:, hs] = (_rms(o, nw_ref[...]) * (zh * _sigmoid(zh))).astype(y_ref.dtype)

    @pl.when(j == pl.num_programs(1) - 1)
    def _():
        sfin_ref[...] = s_ref[...]


def deltanet(proj, cw, hist, s0, ab, nw, nseq, seq_len, c):
    c = min(c, seq_len)
    assert seq_len % c == 0 and c % 8 == 0
    nblk = seq_len // c
    nhs = hist.shape[0]
    assert nhs in (1, nseq) and s0.shape[0] == nhs
    hidx3 = (lambda s, j: (s, 0, 0)) if nhs > 1 else (lambda s, j: (0, 0, 0))
    hidx4 = (lambda s, j: (s, 0, 0, 0)) if nhs > 1 else (lambda s, j: (0, 0, 0, 0))
    zw = DN_HEADS * DN_DV
    return pl.pallas_call(
        functools.partial(_dn_kernel, c=c),
        grid=(nseq, nblk),
        in_specs=[pl.BlockSpec((c, DN_QKV), lambda s, j: (s * nblk + j, COL_QKV // DN_QKV)),
                  pl.BlockSpec((c, zw), lambda s, j: (s * nblk + j, COL_Z // zw)),
                  pl.BlockSpec((c, SMALL_W), lambda s, j: (s * nblk + j, COL_SMALL // SMALL_W)),
                  pl.BlockSpec(cw.shape, lambda s, j: (0, 0)),
                  pl.BlockSpec((None, 8, DN_QKV), hidx3),
                  pl.BlockSpec((None, DN_HEADS, DN_DK, DN_DV), hidx4),
                  pl.BlockSpec(ab.shape, lambda s, j: (0, 0)),
                  pl.BlockSpec(nw.shape, lambda s, j: (0, 0))],
        out_specs=[pl.BlockSpec((c, zw), lambda s, j: (s * nblk + j, 0)),
                   pl.BlockSpec((None, DN_HEADS, DN_DK, DN_DV), lambda s, j: (s, 0, 0, 0))],
        out_shape=[jax.ShapeDtypeStruct((nseq * seq_len, zw), BF16),
                   jax.ShapeDtypeStruct((nseq, DN_HEADS, DN_DK, DN_DV), F32)],
        scratch_shapes=[pltpu.VMEM((c + 8, DN_QKV), F32),
                        pltpu.VMEM((DN_HEADS, DN_DK, DN_DV), F32)],
        compiler_params=_cparams(("parallel", "arbitrary")),
        name="deltanet",
    )(proj, proj, proj, cw, hist, s0, ab, nw)


CF_HALO = 32


def _cf_conv_kernel(u_ref, hist_ref, w_ref, b_ref, lg_ref, lb_ref, o_ref, ext_ref, acc_ref, *, r):
    j = pl.program_id(1)
    first = CF_HALO - (CF_KERNEL - 1)

    @pl.when(j == 0)
    def _():
        ext_ref[0:CF_HALO, :] = hist_ref[...]

    ext_ref[CF_HALO:CF_HALO + r, :] = u_ref[...]
    group = 4
    for t0 in range(0, CF_KERNEL, group):
        part = None
        for t in range(t0, min(t0 + group, CF_KERNEL)):
            term = w_ref[t:t + 1, :] * ext_ref[first + t:first + t + r, :]
            part = term if part is None else part + term
        if t0 == 0:
            acc_ref[...] = part + b_ref[...]
        else:
            acc_ref[...] += part
    if r >= CF_HALO:
        ext_ref[0:CF_HALO, :] = ext_ref[r:r + CF_HALO, :]
    x = acc_ref[...]
    xc = x - jnp.mean(x, axis=-1, keepdims=True)
    y = xc * lax.rsqrt(jnp.mean(xc * xc, axis=-1, keepdims=True) + EPS) * lg_ref[...] + lb_ref[...]
    o_ref[...] = (y * _sigmoid(y)).astype(o_ref.dtype)


def cf_conv(u, hist, w, b, lg, lb, nseq, seq_len, r):
    d = u.shape[1]
    r = min(r, seq_len)
    nblk = seq_len // r
    assert seq_len % r == 0 and (r >= CF_HALO or nblk == 1)
    nhs = hist.shape[0]
    hidx = (lambda s, j: (s, 0, 0)) if nhs > 1 else (lambda s, j: (0, 0, 0))
    vec = lambda: pl.BlockSpec((1, d), lambda s, j: (0, 0))
    return pl.pallas_call(
        functools.partial(_cf_conv_kernel, r=r),
        grid=(nseq, nblk),
        in_specs=[pl.BlockSpec((r, d), lambda s, j: (s * nblk + j, 0)),
                  pl.BlockSpec((None, CF_HALO, d), hidx),
                  pl.BlockSpec(w.shape, lambda s, j: (0, 0)),
                  vec(), vec(), vec()],
        out_specs=pl.BlockSpec((r, d), lambda s, j: (s * nblk + j, 0)),
        out_shape=jax.ShapeDtypeStruct((nseq * seq_len, d), BF16),
        scratch_shapes=[pltpu.VMEM((r + CF_HALO, d), F32), pltpu.VMEM((r, d), F32)],
        compiler_params=_cparams(("parallel", "arbitrary")),
        name="cf_conv",
    )(u, hist, w, b, lg, lb)


def _rope_tables(pos):
    half = QK_ROPE // 2
    inv = ROPE_THETA ** (-jnp.arange(half, dtype=F32) / half)
    ang = pos.astype(F32)[:, None] * inv[None, :]
    cos, sin = jnp.cos(ang), jnp.sin(ang)
    zeros = jnp.zeros((pos.shape[0], LANE - QK_ROPE), F32)
    cq = jnp.concatenate([cos, cos, zeros], axis=1)
    sq = jnp.concatenate([sin, sin, zeros], axis=1)
    ck = jnp.concatenate([cos, cos, sin, sin], axis=1)
    return cq, sq, ck


def _rot_cols(w):
    half = w.shape[-1] // 2
    return jnp.concatenate([-w[..., half:], w[..., :half]], axis=-1)


def _row(v, width=None):
    v = v.astype(F32).reshape(1, -1)
    if width is not None and v.shape[1] < width:
        v = jnp.pad(v, ((0, 0), (0, width - v.shape[1])))
    return v


def kernel(x_prompt, x_sample, cache_mla_ckv, cache_mla_krope, state_dn_s, state_dn_conv, state_cf_conv, meta_tokens, norm_gains, w_in, mla_gq, mla_gkv, w_uq, w_uk, w_uv, dn_conv_w, dn_a_log, dn_dt_bias, dn_norm_w, w_out, cf_w_pw1, cf_b_pw1, cf_w_dw, cf_b_dw, cf_ln_g, cf_ln_b, cf_w_pw2, cf_b_pw2, w_gate, w_up, w_down):
    bp, lp, d = x_prompt.shape
    bs, ls, _ = x_sample.shape
    n_meta = meta_tokens.shape[0]
    past = cache_mla_ckv.shape[2] - n_meta
    depth = norm_gains.shape[0]
    assert n_meta == N_META and ls == n_meta and n_meta <= CHUNK
    assert past % CHUNK == 0 and ls <= CHUNK and lp % CHUNK == 0
    ns = bs + 1
    hw = MLA_HEADS * QK_NOPE

    hp = x_prompt.reshape(bp * lp, d)
    hs = jnp.concatenate([x_sample.reshape(bs * ls, d), meta_tokens.astype(F32)], axis=0)
    meta_rows = slice(bs * ls, bs * ls + n_meta)

    pos_p = n_meta + jnp.arange(lp)
    pos_s = jnp.concatenate([jnp.tile(n_meta + past + jnp.arange(ls), bs), jnp.arange(n_meta)])
    tab_p = _rope_tables(pos_p)
    tab_s = _rope_tables(pos_s)
    zero_d = jnp.zeros((1, d), F32)

    outs = {k: [] for k in ("p_ckv", "p_kr", "p_s", "p_conv", "p_cf", "s_ckv", "s_kr", "s_s", "s_conv", "s_cf")}
    for layer in range(depth):
        ng = norm_gains[layer].astype(F32)
        g0, g1, g2, g3 = (ng[i:i + 1] for i in range(4))
        if layer % 2 == 0:
            e = layer // 2
            offs = np.cumsum((Q_LORA, KV_LORA, QK_ROPE, DN_QKV, DN_HEADS * DN_DV, DN_HEADS, DN_HEADS))
            wi = w_in[e]
            w_qd, w_kvd, w_kr = wi[:, :offs[0]], wi[:, offs[0]:offs[1]], wi[:, offs[1]:offs[2]]
            w_qkv, w_z = wi[:, offs[2]:offs[3]], wi[:, offs[3]:offs[4]]
            w_a, w_b = wi[:, offs[4]:offs[5]], wi[:, offs[5]:offs[6]]
            w_proj = jnp.concatenate(
                [w_qkv, w_z, w_qd, w_kvd, w_kr, _rot_cols(w_kr), w_a, w_b,
                 jnp.zeros((d, SMALL_W - 2 * QK_ROPE - 2 * DN_HEADS), F32)], axis=1).astype(BF16)
            zero_proj = jnp.zeros((1, PROJ_W), F32)
            wq3 = w_uq[e]
            wq_n = wq3[:, :, :QK_NOPE].reshape(Q_LORA, hw)
            wq_r = wq3[:, :, QK_NOPE:]
            pad_r = lambda w: jnp.pad(w, ((0, 0), (0, 0), (0, LANE - QK_ROPE))).reshape(Q_LORA, hw)
            wq = jnp.concatenate([wq_n, pad_r(wq_r), pad_r(_rot_cols(wq_r))], axis=1).astype(BF16)
            wkv = jnp.concatenate([w_uk[e].reshape(KV_LORA, hw), w_uv[e].reshape(KV_LORA, hw)],
                                  axis=1).astype(BF16)
            gq, gkv = _row(mla_gq[e]), _row(mla_gkv[e])
            cw = jnp.pad(dn_conv_w[e].astype(F32), ((0, 8 - DN_CONV), (0, 0)))
            ab = jnp.concatenate([_row(dn_a_log[e], LANE), _row(dn_dt_bias[e], LANE),
                                  jnp.zeros((6, LANE), F32)], axis=0)
            nw = _row(dn_norm_w[e])
            wo = w_out[e].astype(BF16)
            wo_mla, wo_dn = wo[:hw], wo[hw:]

            proj_s = norm_matmul(hs, g0, w_proj, zero_proj, 1024, 768)
            qn_s, qr_s, ckv_s, kn_s, v_s, kro_s, krp_s = mla_prep(proj_s, gq, gkv, wq, wkv, *tab_s, 512)
            lh = n_meta + past
            lh_pad = -(-lh // LANE) * LANE
            hist_ckv = jnp.pad(cache_mla_ckv[e].astype(F32), ((0, 1), (0, lh_pad - lh), (0, 0)))
            hkn, hv = kv_up(hist_ckv.reshape(ns * lh_pad, KV_LORA), wkv, 128)
            hkrp = jnp.pad(cache_mla_krope[e].astype(BF16),
                           ((0, 1), (0, lh_pad - lh), (0, LANE - QK_ROPE)))
            hlen_s = jnp.concatenate([jnp.full((bs,), lh, jnp.int32), jnp.zeros((1,), jnp.int32)])
            ymla_s = attention(qn_s, qr_s, kn_s, krp_s, v_s, hkn.reshape(ns, lh_pad, hw), hkrp,
                               hv.reshape(ns, lh_pad, hw), hlen_s, ns, ls, ls)
            conv_hist_s = jnp.pad(state_dn_conv[e].astype(F32), ((0, 1), (8 - (DN_CONV - 1), 0), (0, 0)))
            s0_s = jnp.pad(state_dn_s[e].astype(F32), ((0, 1), (0, 0), (0, 0), (0, 0)))
            ydn_s, sfin_s = deltanet(proj_s, cw, conv_hist_s, s0_s, ab, nw, ns, ls, CHUNK)
            hs = matmul_resnorm([ymla_s, ydn_s], [wo_mla, wo_dn], zero_d, g1, hs, 512)

            proj_p = norm_matmul(hp, g0, w_proj, zero_proj, 1024, 768)
            qn_p, qr_p, ckv_p, kn_p, v_p, kro_p, krp_p = mla_prep(proj_p, gq, gkv, wq, wkv, *tab_p, 512)
            hlen_p = jnp.full((bp,), n_meta, jnp.int32)
            ymla_p = attention(qn_p, qr_p, kn_p, krp_p, v_p, kn_s[meta_rows][None], krp_s[meta_rows][None],
                               v_s[meta_rows][None], hlen_p, bp, lp, 256)
            conv_hist_p = jnp.pad(proj_s[meta_rows, COL_QKV:COL_QKV + DN_QKV][-(DN_CONV - 1):],
                                  ((8 - (DN_CONV - 1), 0), (0, 0)))[None]
            ydn_p, sfin_p = deltanet(proj_p, cw, conv_hist_p, sfin_s[bs:], ab, nw, bp, lp, CHUNK)
            hp = matmul_resnorm([ymla_p, ydn_p], [wo_mla, wo_dn], zero_d, g1, hp, 512)

            bc = lambda a: jnp.broadcast_to(a[None], (bp,) + a.shape)
            outs["p_ckv"].append(jnp.concatenate([bc(ckv_s[meta_rows]), ckv_p.reshape(bp, lp, KV_LORA)], axis=1))
            outs["p_kr"].append(jnp.concatenate([bc(kro_s[meta_rows]), kro_p.reshape(bp, lp, QK_ROPE)], axis=1))
            outs["p_s"].append(sfin_p)
            outs["p_conv"].append(proj_p.reshape(bp, lp, PROJ_W)[:, lp - (DN_CONV - 1):, COL_QKV:COL_QKV + DN_QKV])
            outs["s_ckv"].append(ckv_s[:bs * ls].reshape(bs, ls, KV_LORA))
            outs["s_kr"].append(kro_s[:bs * ls].reshape(bs, ls, QK_ROPE))
            outs["s_s"].append(sfin_s[:bs])
            xqkv_s = proj_s[:bs * ls, COL_QKV:COL_QKV + DN_QKV].reshape(bs, ls, DN_QKV)
            outs["s_conv"].append(jnp.concatenate([state_dn_conv[e].astype(F32), xqkv_s], axis=1)[:, -(DN_CONV - 1):])
        else:
            o = layer // 2
            w1 = cf_w_pw1[o].astype(BF16)
            b1 = _row(cf_b_pw1[o])
            wdw = jnp.pad(cf_w_dw[o].astype(F32), ((0, CF_HALO - CF_KERNEL), (0, 0)))
            bdw, lg, lb = _row(cf_b_dw[o]), _row(cf_ln_g[o]), _row(cf_ln_b[o])
            w2 = cf_w_pw2[o].astype(BF16)
            b2 = _row(cf_b_pw2[o])
            keep = CF_KERNEL - 1

            u_s = norm_glu(hs, g0, w1, b1, 512, 512)
            hist_s = jnp.pad(state_cf_conv[o].astype(F32), ((0, 1), (CF_HALO - keep, 0), (0, 0)))
            c_s = cf_conv(u_s, hist_s, wdw, bdw, lg, lb, ns, ls, 128)
            hs = matmul_resnorm([c_s], [w2], b2, g1, hs, 512)

            u_p = norm_glu(hp, g0, w1, b1, 512, 512)
            hist_p = jnp.pad(u_s[meta_rows], ((CF_HALO - n_meta, 0), (0, 0)))[None]
            c_p = cf_conv(u_p, hist_p, wdw, bdw, lg, lb, bp, lp, 128)
            hp = matmul_resnorm([c_p], [w2], b2, g1, hp, 512)

            u_p3 = u_p.reshape(bp, lp, d)
            meta_u = jnp.broadcast_to(u_s[meta_rows][None], (bp, n_meta, d))
            outs["p_cf"].append(jnp.concatenate([jnp.zeros((bp, keep, d), F32), meta_u, u_p3[:, max(lp - keep, 0):]],
                                                axis=1)[:, -keep:])
            outs["s_cf"].append(jnp.concatenate([state_cf_conv[o].astype(F32),
                                                 u_s[:bs * ls].reshape(bs, ls, d)], axis=1)[:, -keep:])
        wg, wu, wd = w_gate[layer].astype(BF16), w_up[layer].astype(BF16), w_down[layer].astype(BF16)
        hs = ffn(hs, g2, wg, wu, wd, g3, 512, 512)
        hp = ffn(hp, g2, wg, wu, wd, g3, 512, 512)

    y_prompt = hp.reshape(bp, lp, d)
    y_sample = hs[:bs * ls].reshape(bs, ls, d)
    st = lambda k: jnp.stack(outs[k])
    return (y_prompt, y_sample, st("p_ckv"), st("p_kr"), st("p_s"), st("p_conv"), st("p_cf"),
            st("s_ckv"), st("s_kr"), st("s_s"), st("s_conv"), st("s_cf"))
```

```python
import functools

import numpy as np
import jax
import jax.numpy as jnp
from jax import lax
from jax.experimental import pallas as pl
from jax.experimental.pallas import tpu as pltpu

F32 = jnp.float32
BF16 = jnp.bfloat16

EPS = 1e-6
NEG_INF = -1e30
CHUNK = 64
N_META = 16
MLA_HEADS = 8
Q_LORA = 512
KV_LORA = 512
QK_NOPE = 128
QK_ROPE = 64
V_HEAD = 128
ROPE_THETA = 10000.0
DN_HEADS = 8
DN_DK = 128
DN_DV = 128
DN_CONV = 4
DN_QKV = DN_HEADS * (2 * DN_DK + DN_DV)
CF_KERNEL = 31

LANE = 128
VMEM_LIMIT = 56 * 1024 * 1024

COL_QKV = 0
COL_Z = DN_QKV
COL_QD = COL_Z + DN_HEADS * DN_DV
COL_KVD = COL_QD + Q_LORA
COL_SMALL = COL_KVD + KV_LORA
SMALL_W = 256
PROJ_W = COL_SMALL + SMALL_W


def _cparams(sem):
    return pltpu.CompilerParams(dimension_semantics=sem, vmem_limit_bytes=VMEM_LIMIT)


def _rms(x, g):
    return x * lax.rsqrt(jnp.mean(x * x, axis=-1, keepdims=True) + EPS) * g


def _sigmoid(x):
    return 1.0 / (1.0 + jnp.exp(-x))


def _dot(a, b):
    return jnp.dot(a, b, preferred_element_type=F32)


def _dot_nt(a, b):
    return lax.dot_general(a, b, (((1,), (1,)), ((), ())), preferred_element_type=F32)


def _dot_tn(a, b):
    return lax.dot_general(a, b, (((0,), (0,)), ((), ())), preferred_element_type=F32)


def _row_tile(m, pref):
    t = min(pref, m)
    while m % t:
        t //= 2
    return t


def _norm_mm_kernel(x_ref, g_ref, w_ref, b_ref, o_ref, xn_ref):
    @pl.when(pl.program_id(1) == 0)
    def _():
        xn_ref[...] = _rms(x_ref[...], g_ref[...]).astype(BF16)

    o_ref[...] = (_dot(xn_ref[...], w_ref[...]) + b_ref[...]).astype(o_ref.dtype)


def _norm_glu_kernel(x_ref, g_ref, wa_ref, wg_ref, ba_ref, bg_ref, o_ref, xn_ref):
    @pl.when(pl.program_id(1) == 0)
    def _():
        xn_ref[...] = _rms(x_ref[...], g_ref[...]).astype(BF16)

    xn = xn_ref[...]
    a = _dot(xn, wa_ref[...]) + ba_ref[...]
    gt = _dot(xn, wg_ref[...]) + bg_ref[...]
    o_ref[...] = (a * _sigmoid(gt)).astype(o_ref.dtype)


def norm_matmul(x, g, w, b, tm, tn, out_dtype=F32):
    m, k = x.shape
    n = w.shape[1]
    tm = _row_tile(m, tm)
    return pl.pallas_call(
        _norm_mm_kernel,
        grid=(m // tm, n // tn),
        in_specs=[pl.BlockSpec((tm, k), lambda i, j: (i, 0)),
                  pl.BlockSpec((1, k), lambda i, j: (0, 0)),
                  pl.BlockSpec((k, tn), lambda i, j: (0, j)),
                  pl.BlockSpec((1, tn), lambda i, j: (0, j))],
        out_specs=pl.BlockSpec((tm, tn), lambda i, j: (i, j)),
        out_shape=jax.ShapeDtypeStruct((m, n), out_dtype),
        scratch_shapes=[pltpu.VMEM((tm, k), BF16)],
        compiler_params=_cparams(("parallel", "arbitrary")),
        name="norm_matmul",
    )(x, g, w, b)


def norm_glu(x, g, w, b, tm, tn):
    m, k = x.shape
    n = w.shape[1] // 2
    tm = _row_tile(m, tm)
    nb = n // tn
    return pl.pallas_call(
        _norm_glu_kernel,
        grid=(m // tm, nb),
        in_specs=[pl.BlockSpec((tm, k), lambda i, j: (i, 0)),
                  pl.BlockSpec((1, k), lambda i, j: (0, 0)),
                  pl.BlockSpec((k, tn), lambda i, j: (0, j)),
                  pl.BlockSpec((k, tn), lambda i, j: (0, j + nb)),
                  pl.BlockSpec((1, tn), lambda i, j: (0, j)),
                  pl.BlockSpec((1, tn), lambda i, j: (0, j + nb))],
        out_specs=pl.BlockSpec((tm, tn), lambda i, j: (i, j)),
        out_shape=jax.ShapeDtypeStruct((m, n), F32),
        scratch_shapes=[pltpu.VMEM((tm, k), BF16)],
        compiler_params=_cparams(("parallel", "arbitrary")),
        name="norm_glu",
    )(x, g, w, w, b, b)


def _mm_resnorm_kernel(*refs, n_in):
    xs = refs[:n_in]
    ws = refs[n_in:2 * n_in]
    b_ref, g_ref, res_ref, o_ref = refs[2 * n_in:]
    y = b_ref[...]
    for x_ref, w_ref in zip(xs, ws):
        y = y + _dot(x_ref[...], w_ref[...])
    o_ref[...] = res_ref[...] + _rms(y, g_ref[...])


def matmul_resnorm(xs, ws, b, g, res, tm):
    m, n = res.shape
    tm = _row_tile(m, tm)
    n_in = len(xs)
    in_specs = ([pl.BlockSpec((tm, x.shape[1]), lambda i: (i, 0)) for x in xs]
                + [pl.BlockSpec(w.shape, lambda i: (0, 0)) for w in ws]
                + [pl.BlockSpec((1, n), lambda i: (0, 0)),
                   pl.BlockSpec((1, n), lambda i: (0, 0)),
                   pl.BlockSpec((tm, n), lambda i: (i, 0))])
    return pl.pallas_call(
        functools.partial(_mm_resnorm_kernel, n_in=n_in),
        grid=(m // tm,),
        in_specs=in_specs,
        out_specs=pl.BlockSpec((tm, n), lambda i: (i, 0)),
        out_shape=jax.ShapeDtypeStruct((m, n), F32),
        compiler_params=_cparams(("parallel",)),
        name="matmul_resnorm",
    )(*xs, *ws, b, g, res)


def _ffn_kernel(h_ref, g2_ref, wg_ref, wu_ref, wd_ref, g3_ref, o_ref, xn_ref, acc_ref):
    j = pl.program_id(1)

    @pl.when(j == 0)
    def _():
        xn_ref[...] = _rms(h_ref[...], g2_ref[...]).astype(BF16)
        acc_ref[...] = jnp.zeros_like(acc_ref)

    xn = xn_ref[...]
    gate = _dot(xn, wg_ref[...])
    up = _dot(xn, wu_ref[...])
    a = (gate * _sigmoid(gate) * up).astype(BF16)
    acc_ref[...] += _dot(a, wd_ref[...])

    @pl.when(j == pl.num_programs(1) - 1)
    def _():
        o_ref[...] = h_ref[...] + _rms(acc_ref[...], g3_ref[...])


def ffn(h, g2, wg, wu, wd, g3, tm, tf):
    m, d = h.shape
    f = wg.shape[1]
    tm = _row_tile(m, tm)
    return pl.pallas_call(
        _ffn_kernel,
        grid=(m // tm, f // tf),
        in_specs=[pl.BlockSpec((tm, d), lambda i, j: (i, 0)),
                  pl.BlockSpec((1, d), lambda i, j: (0, 0)),
                  pl.BlockSpec((d, tf), lambda i, j: (0, j)),
                  pl.BlockSpec((d, tf), lambda i, j: (0, j)),
                  pl.BlockSpec((tf, d), lambda i, j: (j, 0)),
                  pl.BlockSpec((1, d), lambda i, j: (0, 0))],
        out_specs=pl.BlockSpec((tm, d), lambda i, j: (i, 0)),
        out_shape=jax.ShapeDtypeStruct((m, d), F32),
        scratch_shapes=[pltpu.VMEM((tm, d), BF16), pltpu.VMEM((tm, d), F32)],
        compiler_params=_cparams(("parallel", "arbitrary")),
        name="ffn",
    )(h, g2, wg, wu, wd, g3)


def _mla_prep_kernel(qd_ref, kvd_ref, sm_ref, gq_ref, gkv_ref, wq_ref, wkv_ref, cq_ref, sq_ref, ck_ref,
                     qn_ref, qr_ref, ckv_ref, kn_ref, v_ref, kro_ref, krp_ref):
    hw = MLA_HEADS * QK_NOPE
    scale = (QK_NOPE + QK_ROPE) ** -0.5
    cq = _rms(qd_ref[...], gq_ref[...]).astype(BF16)
    q = _dot(cq, wq_ref[...])
    qn_ref[...] = (q[:, :hw] * scale).astype(BF16)
    cos8 = jnp.tile(cq_ref[...], (1, MLA_HEADS))
    sin8 = jnp.tile(sq_ref[...], (1, MLA_HEADS))
    qr_ref[...] = ((q[:, hw:2 * hw] * cos8 + q[:, 2 * hw:] * sin8) * scale).astype(BF16)
    ckv = _rms(kvd_ref[...], gkv_ref[...])
    ckv_ref[...] = ckv
    kv = _dot(ckv.astype(BF16), wkv_ref[...])
    kn_ref[...] = kv[:, :hw].astype(BF16)
    v_ref[...] = kv[:, hw:].astype(BF16)
    y = sm_ref[:, :LANE] * ck_ref[...]
    kro = y + pltpu.roll(y, QK_ROPE, 1)
    kro_ref[...] = kro[:, :QK_ROPE]
    lane = lax.broadcasted_iota(jnp.int32, kro.shape, 1)
    krp_ref[...] = jnp.where(lane < QK_ROPE, kro, 0.0).astype(BF16)


def mla_prep(proj, gq, gkv, wq, wkv, cq_tab, sq_tab, ck_tab, tm):
    m = proj.shape[0]
    tm = _row_tile(min(m, cq_tab.shape[0]), tm)
    nt = cq_tab.shape[0] // tm
    hw = MLA_HEADS * QK_NOPE
    tab = lambda: pl.BlockSpec((tm, LANE), lambda i: (i % nt, 0))
    full = lambda a: pl.BlockSpec(a.shape, lambda i: (0, 0))
    return pl.pallas_call(
        _mla_prep_kernel,
        grid=(m // tm,),
        in_specs=[pl.BlockSpec((tm, Q_LORA), lambda i: (i, COL_QD // Q_LORA)),
                  pl.BlockSpec((tm, KV_LORA), lambda i: (i, COL_KVD // KV_LORA)),
                  pl.BlockSpec((tm, SMALL_W), lambda i: (i, COL_SMALL // SMALL_W)),
                  full(gq), full(gkv), full(wq), full(wkv), tab(), tab(), tab()],
        out_specs=[pl.BlockSpec((tm, hw), lambda i: (i, 0)),
                   pl.BlockSpec((tm, hw), lambda i: (i, 0)),
                   pl.BlockSpec((tm, KV_LORA), lambda i: (i, 0)),
                   pl.BlockSpec((tm, hw), lambda i: (i, 0)),
                   pl.BlockSpec((tm, hw), lambda i: (i, 0)),
                   pl.BlockSpec((tm, QK_ROPE), lambda i: (i, 0)),
                   pl.BlockSpec((tm, LANE), lambda i: (i, 0))],
        out_shape=[jax.ShapeDtypeStruct((m, hw), BF16),
                   jax.ShapeDtypeStruct((m, hw), BF16),
                   jax.ShapeDtypeStruct((m, KV_LORA), F32),
                   jax.ShapeDtypeStruct((m, hw), BF16),
                   jax.ShapeDtypeStruct((m, hw), BF16),
                   jax.ShapeDtypeStruct((m, QK_ROPE), F32),
                   jax.ShapeDtypeStruct((m, LANE), BF16)],
        compiler_params=_cparams(("parallel",)),
        name="mla_prep",
    )(proj, proj, proj, gq, gkv, wq, wkv, cq_tab, sq_tab, ck_tab)


def _kv_up_kernel(ckv_ref, wkv_ref, kn_ref, v_ref):
    hw = MLA_HEADS * QK_NOPE
    kv = _dot(ckv_ref[...].astype(BF16), wkv_ref[...])
    kn_ref[...] = kv[:, :hw].astype(BF16)
    v_ref[...] = kv[:, hw:].astype(BF16)


def kv_up(ckv, wkv, tm):
    m = ckv.shape[0]
    tm = _row_tile(m, tm)
    hw = MLA_HEADS * QK_NOPE
    return pl.pallas_call(
        _kv_up_kernel,
        grid=(m // tm,),
        in_specs=[pl.BlockSpec((tm, KV_LORA), lambda i: (i, 0)),
                  pl.BlockSpec(wkv.shape, lambda i: (0, 0))],
        out_specs=[pl.BlockSpec((tm, hw), lambda i: (i, 0)),
                   pl.BlockSpec((tm, hw), lambda i: (i, 0))],
        out_shape=[jax.ShapeDtypeStruct((m, hw), BF16), jax.ShapeDtypeStruct((m, hw), BF16)],
        compiler_params=_cparams(("parallel",)),
        name="kv_up",
    )(ckv, wkv)


def _attn_kernel(hlen_ref, qn_ref, qr_ref, kn_ref, kr_ref, v_ref, hkn_ref, hkr_ref, hv_ref, o_ref,
                 m_ref, l_ref, acc_ref, a_ref, qc_ref, s_ref, p_ref, hs_ref, hp_ref, *, tile, lh):
    s_idx = pl.program_id(0)
    qt = pl.program_id(1)
    hlen = hlen_ref[s_idx]
    nh = MLA_HEADS
    heads = [slice(h * LANE, (h + 1) * LANE) for h in range(nh)]

    for h, hs in enumerate(heads):
        qc_ref[h] = jnp.concatenate([qn_ref[:, hs], qr_ref[:, hs]], axis=1)

    def process(get_k, get_v, s_scr, p_scr, valid, first):
        for h in range(nh):
            s_scr[h] = _dot_nt(qc_ref[h], get_k(h))
        for h, hs in enumerate(heads):
            s = s_scr[h]
            if valid is not None:
                s = jnp.where(valid, s, NEG_INF)
            m_cur = jnp.max(s, axis=1, keepdims=True)
            if first:
                m_new = jnp.broadcast_to(m_cur, (tile, LANE))
            else:
                m_prev = m_ref[:, hs]
                m_new = jnp.maximum(m_prev, m_cur)
            p = jnp.exp(s - m_new[:, :1])
            if valid is not None:
                p = jnp.where(valid, p, 0.0)
            p_scr[h] = p.astype(BF16)
            l_cur = jnp.sum(p, axis=1, keepdims=True)
            if first:
                l_ref[:, hs] = jnp.broadcast_to(l_cur, (tile, LANE))
            else:
                alpha = jnp.exp(m_prev - m_new)
                a_ref[:, hs] = alpha
                l_ref[:, hs] = alpha * l_ref[:, hs] + l_cur
            m_ref[:, hs] = m_new
        for h, hs in enumerate(heads):
            pv = _dot(p_scr[h], get_v(h))
            if first:
                acc_ref[:, hs] = pv
            else:
                acc_ref[:, hs] = a_ref[:, hs] * acc_ref[:, hs] + pv

    hvalid = lax.broadcasted_iota(jnp.int32, (tile, lh), 1) < hlen
    process(lambda h: jnp.concatenate([hkn_ref[:, heads[h]], hkr_ref[...]], axis=1),
            lambda h: hv_ref[:, heads[h]], hs_ref, hp_ref, hvalid, True)

    def own_tile(kt, valid):
        rows = pl.ds(pl.multiple_of(kt * tile, tile), tile)
        process(lambda h: jnp.concatenate([kn_ref[rows, heads[h]], kr_ref[rows, :]], axis=1),
                lambda h: v_ref[rows, heads[h]], s_ref, p_ref, valid, False)

    def body(kt, carry):
        own_tile(kt, None)
        return carry

    lax.fori_loop(0, qt, body, 0)
    if tile > CHUNK:
        row = lax.broadcasted_iota(jnp.int32, (tile, tile), 0) // CHUNK
        col = lax.broadcasted_iota(jnp.int32, (tile, tile), 1) // CHUNK
        dvalid = col <= row
    else:
        dvalid = None
    own_tile(qt, dvalid)

    for hs in heads:
        o_ref[:, hs] = (acc_ref[:, hs] / l_ref[:, hs]).astype(o_ref.dtype)


def attention(qn, qr, kn, krp, v, hkn, hkrp, hv, hlen, nseq, seq_len, tile):
    hw = MLA_HEADS * LANE
    tile = min(tile, seq_len)
    assert seq_len % tile == 0 and (tile % CHUNK == 0 or seq_len == tile <= CHUNK)
    nqt = seq_len // tile
    nhs, lh = hkn.shape[0], hkn.shape[1]
    assert nhs in (1, nseq)
    hidx = (lambda s, q, hl: (s, 0, 0)) if nhs > 1 else (lambda s, q, hl: (0, 0, 0))
    grid_spec = pltpu.PrefetchScalarGridSpec(
        num_scalar_prefetch=1,
        grid=(nseq, nqt),
        in_specs=[pl.BlockSpec((tile, hw), lambda s, q, hl: (s * nqt + q, 0)),
                  pl.BlockSpec((tile, hw), lambda s, q, hl: (s * nqt + q, 0)),
                  pl.BlockSpec((seq_len, hw), lambda s, q, hl: (s, 0)),
                  pl.BlockSpec((seq_len, LANE), lambda s, q, hl: (s, 0)),
                  pl.BlockSpec((seq_len, hw), lambda s, q, hl: (s, 0)),
                  pl.BlockSpec((None, lh, hw), hidx),
                  pl.BlockSpec((None, lh, LANE), hidx),
                  pl.BlockSpec((None, lh, hw), hidx)],
        out_specs=pl.BlockSpec((tile, hw), lambda s, q, hl: (s * nqt + q, 0)),
        scratch_shapes=[pltpu.VMEM((tile, hw), F32), pltpu.VMEM((tile, hw), F32),
                        pltpu.VMEM((tile, hw), F32), pltpu.VMEM((tile, hw), F32),
                        pltpu.VMEM((MLA_HEADS, tile, 2 * LANE), BF16),
                        pltpu.VMEM((MLA_HEADS, tile, tile), F32),
                        pltpu.VMEM((MLA_HEADS, tile, tile), BF16),
                        pltpu.VMEM((MLA_HEADS, tile, lh), F32),
                        pltpu.VMEM((MLA_HEADS, tile, lh), BF16)],
    )
    return pl.pallas_call(
        functools.partial(_attn_kernel, tile=tile, lh=lh),
        grid_spec=grid_spec,
        out_shape=jax.ShapeDtypeStruct((nseq * seq_len, hw), BF16),
        compiler_params=_cparams(("parallel", "arbitrary")),
        name="attention",
    )(hlen, qn, qr, kn, krp, v, hkn, hkrp, hv)


def _dn_prep_kernel(x_ref, sm_ref, cw_ref, hist_ref, ab_ref,
                    u_ref, w_ref, qd_ref, kd_ref, attn_ref, gl_ref,
                    ext_ref, act_ref, lhs_ref, kb_ref, dec_ref, rhs_ref, mm_ref, t_ref, xs_ref, *, c, g):
    j = pl.program_id(1)
    nh, dk, dv = DN_HEADS, DN_DK, DN_DV
    kw = nh * dk
    rows_all = c * g
    inst = [(b, h) for b in range(g) for h in range(nh)]

    @pl.when(j == 0)
    def _():
        ext_ref[0:8, :] = hist_ref[...]

    ext_ref[8:8 + rows_all, :] = x_ref[...]
    conv = cw_ref[0:1, :] * ext_ref[5:5 + rows_all, :]
    for t in range(1, DN_CONV):
        conv = conv + cw_ref[t:t + 1, :] * ext_ref[5 + t:5 + t + rows_all, :]
    ext_ref[0:8, :] = ext_ref[rows_all:rows_all + 8, :]
    act_ref[...] = conv * _sigmoid(conv)

    gates = sm_ref[:, LANE:2 * LANE]
    xa = gates + ab_ref[1:2, :]
    softplus = jnp.maximum(xa, 0.0) + jnp.log(1.0 + jnp.exp(-jnp.abs(xa)))
    g_all = -jnp.exp(ab_ref[0:1, :]) * softplus
    beta_all = _sigmoid(gates)

    r = lax.broadcasted_iota(jnp.int32, (c, c), 0)
    q = lax.broadcasted_iota(jnp.int32, (c, c), 1)
    incl = r >= q
    strict = r > q
    eye = jnp.where(r == q, 1.0, 0.0)
    tri = jnp.where(incl, 1.0, 0.0)

    for b in range(g):
        rows = slice(b * c, (b + 1) * c)
        gc = jnp.dot(tri, g_all[rows], preferred_element_type=F32, precision=lax.Precision.HIGHEST)
        if c < LANE:
            gc_sq = jnp.concatenate([gc, jnp.zeros((LANE - c, LANE), F32)], axis=0)
        else:
            gc_sq = gc
        gc_t = gc_sq.T
        egc = jnp.exp(gc)
        glast = gc[c - 1:c, :]
        edl = jnp.exp(glast - gc)
        gl_ref[rows, :] = jnp.broadcast_to(glast, (c, LANE))
        beta_b = beta_all[rows]
        for h in range(nh):
            i = b * nh + h
            hs = slice(h * dk, (h + 1) * dk)
            qh = act_ref[rows, hs]
            kh = act_ref[rows, kw + h * dk:kw + (h + 1) * dk]
            vh = act_ref[rows, 2 * kw + h * dv:2 * kw + (h + 1) * dv]
            qh = qh * (lax.rsqrt(jnp.sum(qh * qh, axis=1, keepdims=True) + EPS) * dk ** -0.5)
            kh = kh * lax.rsqrt(jnp.sum(kh * kh, axis=1, keepdims=True) + EPS)
            bcol = beta_b[:, 8 + h:9 + h]
            gcol = gc[:, h:h + 1]
            grow = gc_t[h:h + 1, :c]
            dec_ref[i] = jnp.where(incl, jnp.exp(jnp.where(incl, gcol - grow, 0.0)), 0.0)
            kb = kh * bcol
            ecol = egc[:, h:h + 1]
            lhs_ref[i] = jnp.concatenate([kb, qh], axis=0).astype(BF16)
            kb_ref[i] = kh.astype(BF16)
            rhs_ref[i] = jnp.concatenate([vh * bcol, kb * ecol], axis=1).astype(BF16)
            qd_ref[rows, hs] = (qh * ecol).astype(BF16)
            kd_ref[rows, hs] = (kh * edl[:, h:h + 1]).astype(BF16)

    for i, (b, h) in enumerate(inst):
        kq = _dot_nt(lhs_ref[i], kb_ref[i])
        dec = dec_ref[i]
        mm = jnp.where(strict, kq[:c] * dec, 0.0)
        mm_ref[i] = mm
        t_ref[i] = eye - jnp.where((r ^ q) == 1, mm, 0.0)
        attn_ref[b, h] = (kq[c:] * dec).astype(BF16)

    s = 2
    while s < c:
        sh = s.bit_length() - 1
        sel = ((r >> sh) ^ (q >> sh)) == 1
        for i in range(len(inst)):
            e = jnp.where(sel, mm_ref[i], 0.0).astype(BF16)
            xs_ref[i] = _dot(e, t_ref[i].astype(BF16)).astype(BF16)
        for i in range(len(inst)):
            t = t_ref[i]
            t_ref[i] = t - _dot(t.astype(BF16), xs_ref[i])
        s *= 2

    for i, (b, h) in enumerate(inst):
        rows = slice(b * c, (b + 1) * c)
        hs = slice(h * dk, (h + 1) * dk)
        uw = _dot(t_ref[i].astype(BF16), rhs_ref[i])
        u_ref[rows, hs] = uw[:, :dv]
        w_ref[rows, hs] = uw[:, dv:].astype(BF16)


def _dn_scan_kernel(u_ref, w_ref, qd_ref, kd_ref, attn_ref, gl_ref, z_ref, s0_ref, nw_ref,
                    y_ref, sfin_ref, s_ref, rr_ref, vn_ref, *, c, nseq, shared_s0):
    j = pl.program_id(0)
    nh, dk = DN_HEADS, DN_DK
    inst = [(s, h) for s in range(nseq) for h in range(nh)]
    heads = [slice(h * dk, (h + 1) * dk) for h in range(nh)]

    @pl.when(j == 0)
    def _():
        for s in range(nseq):
            s_ref[s] = s0_ref[0 if shared_s0 else s]

    for i, (s, h) in enumerate(inst):
        lhs = jnp.concatenate([w_ref[s, :, heads[h]], qd_ref[s, :, heads[h]]], axis=0)
        rr_ref[i] = _dot(lhs, s_ref[s, h].astype(BF16))
    for i, (s, h) in enumerate(inst):
        vn_ref[i] = (u_ref[s, :, heads[h]] - rr_ref[i, 0:c, :]).astype(BF16)
    for i, (s, h) in enumerate(inst):
        o = rr_ref[i, c:2 * c, :] + _dot(attn_ref[s, h], vn_ref[i])
        zh = z_ref[s, :, heads[h]]
        y_ref[s, :, heads[h]] = (_rms(o, nw_ref[...]) * (zh * _sigmoid(zh))).astype(y_ref.dtype)
    for i, (s, h) in enumerate(inst):
        ebd = jnp.exp(gl_ref[s, 0:1, h:h + 1])
        s_ref[s, h] = s_ref[s, h] * ebd + _dot_tn(kd_ref[s, :, heads[h]], vn_ref[i])

    @pl.when(j == pl.num_programs(0) - 1)
    def _():
        sfin_ref[...] = s_ref[...]


def deltanet(proj, cw, hist, s0, ab, nw, nseq, seq_len, c, g):
    c = min(c, seq_len)
    nblk = seq_len // c
    g = min(g, nblk)
    assert seq_len % c == 0 and c % 8 == 0 and nblk % g == 0
    ntile = nblk // g
    rows_t = c * g
    nhs = hist.shape[0]
    assert nhs in (1, nseq) and s0.shape[0] == nhs
    hidx3 = (lambda s, j: (s, 0, 0)) if nhs > 1 else (lambda s, j: (0, 0, 0))
    nh, dk, dv = DN_HEADS, DN_DK, DN_DV
    zw = nh * dv
    rows = nseq * seq_len
    ni = g * nh
    row_blk = lambda w: pl.BlockSpec((rows_t, w), lambda s, j: (s * ntile + j, 0))
    u, w, qd, kd, attn, gl = pl.pallas_call(
        functools.partial(_dn_prep_kernel, c=c, g=g),
        grid=(nseq, ntile),
        in_specs=[pl.BlockSpec((rows_t, DN_QKV), lambda s, j: (s * ntile + j, COL_QKV // DN_QKV)),
                  pl.BlockSpec((rows_t, SMALL_W), lambda s, j: (s * ntile + j, COL_SMALL // SMALL_W)),
                  pl.BlockSpec(cw.shape, lambda s, j: (0, 0)),
                  pl.BlockSpec((None, 8, DN_QKV), hidx3),
                  pl.BlockSpec(ab.shape, lambda s, j: (0, 0))],
        out_specs=[row_blk(zw), row_blk(zw), row_blk(zw), row_blk(zw),
                   pl.BlockSpec((g, nh, c, c), lambda s, j: (s * ntile + j, 0, 0, 0)),
                   row_blk(LANE)],
        out_shape=[jax.ShapeDtypeStruct((rows, zw), F32),
                   jax.ShapeDtypeStruct((rows, zw), BF16),
                   jax.ShapeDtypeStruct((rows, zw), BF16),
                   jax.ShapeDtypeStruct((rows, zw), BF16),
                   jax.ShapeDtypeStruct((nseq * nblk, nh, c, c), BF16),
                   jax.ShapeDtypeStruct((rows, LANE), F32)],
        scratch_shapes=[pltpu.VMEM((rows_t + 8, DN_QKV), F32),
                        pltpu.VMEM((rows_t, DN_QKV), F32),
                        pltpu.VMEM((ni, 2 * c, dk), BF16),
                        pltpu.VMEM((ni, c, dk), BF16),
                        pltpu.VMEM((ni, c, c), F32),
                        pltpu.VMEM((ni, c, dk + dv), BF16),
                        pltpu.VMEM((ni, c, c), F32),
                        pltpu.VMEM((ni, c, c), F32),
                        pltpu.VMEM((ni, c, c), BF16)],
        compiler_params=_cparams(("parallel", "arbitrary")),
        name="dn_prep",
    )(proj, proj, cw, hist, ab)

    seq3 = lambda a: a.reshape(nseq, seq_len, a.shape[1])
    blk3 = lambda wdt, col=0: pl.BlockSpec((nseq, c, wdt), lambda j: (0, j, col))
    y, sfin = pl.pallas_call(
        functools.partial(_dn_scan_kernel, c=c, nseq=nseq, shared_s0=nhs == 1),
        grid=(nblk,),
        in_specs=[blk3(zw), blk3(zw), blk3(zw), blk3(zw),
                  pl.BlockSpec((nseq, None, nh, c, c), lambda j: (0, j, 0, 0, 0)),
                  blk3(LANE), blk3(zw, COL_Z // zw),
                  pl.BlockSpec(s0.shape, lambda j: (0, 0, 0, 0)),
                  pl.BlockSpec(nw.shape, lambda j: (0, 0))],
        out_specs=[blk3(zw), pl.BlockSpec((nseq, nh, dk, dv), lambda j: (0, 0, 0, 0))],
        out_shape=[jax.ShapeDtypeStruct((nseq, seq_len, zw), BF16),
                   jax.ShapeDtypeStruct((nseq, nh, dk, dv), F32)],
        scratch_shapes=[pltpu.VMEM((nseq, nh, dk, dv), F32),
                        pltpu.VMEM((nseq * nh, 2 * c, dv), F32),
                        pltpu.VMEM((nseq * nh, c, dv), BF16)],
        compiler_params=_cparams(("arbitrary",)),
        name="dn_scan",
    )(seq3(u), seq3(w), seq3(qd), seq3(kd), attn.reshape(nseq, nblk, nh, c, c), seq3(gl), seq3(proj),
      s0, nw)
    return y.reshape(rows, zw), sfin


CF_HALO = 32


def _cf_conv_kernel(u_ref, hist_ref, w_ref, b_ref, lg_ref, lb_ref, o_ref, ext_ref, sh_ref, acc_ref, *, r):
    j = pl.program_id(1)
    first = CF_HALO - (CF_KERNEL - 1)
    sub = 8
    span = r + CF_HALO - sub

    @pl.when(j == 0)
    def _():
        ext_ref[0:CF_HALO, :] = hist_ref[...]

    ext_ref[CF_HALO:CF_HALO + r, :] = u_ref[...]
    for sft in range(1, sub):
        sh_ref[sft - 1, 0:span, :] = ext_ref[sft:sft + span, :]
    group = 4
    for t0 in range(0, CF_KERNEL, group):
        part = None
        for t in range(t0, min(t0 + group, CF_KERNEL)):
            sft = (first + t) % sub
            base = first + t - sft
            src = ext_ref[base:base + r, :] if sft == 0 else sh_ref[sft - 1, base:base + r, :]
            term = w_ref[t:t + 1, :] * src
            part = term if part is None else part + term
        if t0 == 0:
            acc_ref[...] = part + b_ref[...]
        else:
            acc_ref[...] += part
    if r >= CF_HALO:
        ext_ref[0:CF_HALO, :] = ext_ref[r:r + CF_HALO, :]
    x = acc_ref[...]
    xc = x - jnp.mean(x, axis=-1, keepdims=True)
    y = xc * lax.rsqrt(jnp.mean(xc * xc, axis=-1, keepdims=True) + EPS) * lg_ref[...] + lb_ref[...]
    o_ref[...] = (y * _sigmoid(y)).astype(o_ref.dtype)


def cf_conv(u, hist, w, b, lg, lb, nseq, seq_len, r):
    d = u.shape[1]
    r = min(r, seq_len)
    nblk = seq_len // r
    assert seq_len % r == 0 and (r >= CF_HALO or nblk == 1)
    nhs = hist.shape[0]
    hidx = (lambda s, j: (s, 0, 0)) if nhs > 1 else (lambda s, j: (0, 0, 0))
    vec = lambda: pl.BlockSpec((1, d), lambda s, j: (0, 0))
    return pl.pallas_call(
        functools.partial(_cf_conv_kernel, r=r),
        grid=(nseq, nblk),
        in_specs=[pl.BlockSpec((r, d), lambda s, j: (s * nblk + j, 0)),
                  pl.BlockSpec((None, CF_HALO, d), hidx),
                  pl.BlockSpec(w.shape, lambda s, j: (0, 0)),
                  vec(), vec(), vec()],
        out_specs=pl.BlockSpec((r, d), lambda s, j: (s * nblk + j, 0)),
        out_shape=jax.ShapeDtypeStruct((nseq * seq_len, d), BF16),
        scratch_shapes=[pltpu.VMEM((r + CF_HALO, d), F32), pltpu.VMEM((7, r + CF_HALO - 8, d), F32),
                        pltpu.VMEM((r, d), F32)],
        compiler_params=_cparams(("parallel", "arbitrary")),
        name="cf_conv",
    )(u, hist, w, b, lg, lb)


def _rope_tables(pos):
    half = QK_ROPE // 2
    inv = ROPE_THETA ** (-jnp.arange(half, dtype=F32) / half)
    ang = pos.astype(F32)[:, None] * inv[None, :]
    cos, sin = jnp.cos(ang), jnp.sin(ang)
    zeros = jnp.zeros((pos.shape[0], LANE - QK_ROPE), F32)
    cq = jnp.concatenate([cos, cos, zeros], axis=1)
    sq = jnp.concatenate([sin, sin, zeros], axis=1)
    ck = jnp.concatenate([cos, cos, sin, sin], axis=1)
    return cq, sq, ck


def _rot_cols(w):
    half = w.shape[-1] // 2
    return jnp.concatenate([-w[..., half:], w[..., :half]], axis=-1)


def _row(v, width=None):
    v = v.astype(F32).reshape(1, -1)
    if width is not None and v.shape[1] < width:
        v = jnp.pad(v, ((0, 0), (0, width - v.shape[1])))
    return v


def kernel(x_prompt, x_sample, cache_mla_ckv, cache_mla_krope, state_dn_s, state_dn_conv, state_cf_conv, meta_tokens, norm_gains, w_in, mla_gq, mla_gkv, w_uq, w_uk, w_uv, dn_conv_w, dn_a_log, dn_dt_bias, dn_norm_w, w_out, cf_w_pw1, cf_b_pw1, cf_w_dw, cf_b_dw, cf_ln_g, cf_ln_b, cf_w_pw2, cf_b_pw2, w_gate, w_up, w_down):
    bp, lp, d = x_prompt.shape
    bs, ls, _ = x_sample.shape
    n_meta = meta_tokens.shape[0]
    past = cache_mla_ckv.shape[2] - n_meta
    depth = norm_gains.shape[0]
    assert n_meta == N_META and ls == n_meta and n_meta <= CHUNK
    assert past % CHUNK == 0 and ls <= CHUNK and lp % CHUNK == 0
    ns = bs + 1
    hw = MLA_HEADS * QK_NOPE

    hp = x_prompt.reshape(bp * lp, d)
    hs = jnp.concatenate([x_sample.reshape(bs * ls, d), meta_tokens.astype(F32)], axis=0)
    meta_rows = slice(bs * ls, bs * ls + n_meta)

    pos_p = n_meta + jnp.arange(lp)
    pos_s = jnp.concatenate([jnp.tile(n_meta + past + jnp.arange(ls), bs), jnp.arange(n_meta)])
    tab_p = _rope_tables(pos_p)
    tab_s = _rope_tables(pos_s)
    zero_d = jnp.zeros((1, d), F32)

    outs = {k: [] for k in ("p_ckv", "p_kr", "p_s", "p_conv", "p_cf", "s_ckv", "s_kr", "s_s", "s_conv", "s_cf")}
    for layer in range(depth):
        ng = norm_gains[layer].astype(F32)
        g0, g1, g2, g3 = (ng[i:i + 1] for i in range(4))
        if layer % 2 == 0:
            e = layer // 2
            offs = np.cumsum((Q_LORA, KV_LORA, QK_ROPE, DN_QKV, DN_HEADS * DN_DV, DN_HEADS, DN_HEADS))
            wi = w_in[e]
            w_qd, w_kvd, w_kr = wi[:, :offs[0]], wi[:, offs[0]:offs[1]], wi[:, offs[1]:offs[2]]
            w_qkv, w_z = wi[:, offs[2]:offs[3]], wi[:, offs[3]:offs[4]]
            w_a, w_b = wi[:, offs[4]:offs[5]], wi[:, offs[5]:offs[6]]
            w_proj = jnp.concatenate(
                [w_qkv, w_z, w_qd, w_kvd, w_kr, _rot_cols(w_kr), w_a, w_b,
                 jnp.zeros((d, SMALL_W - 2 * QK_ROPE - 2 * DN_HEADS), F32)], axis=1).astype(BF16)
            zero_proj = jnp.zeros((1, PROJ_W), F32)
            wq3 = w_uq[e]
            wq_n = wq3[:, :, :QK_NOPE].reshape(Q_LORA, hw)
            wq_r = wq3[:, :, QK_NOPE:]
            pad_r = lambda w: jnp.pad(w, ((0, 0), (0, 0), (0, LANE - QK_ROPE))).reshape(Q_LORA, hw)
            wq = jnp.concatenate([wq_n, pad_r(wq_r), pad_r(_rot_cols(wq_r))], axis=1).astype(BF16)
            wkv = jnp.concatenate([w_uk[e].reshape(KV_LORA, hw), w_uv[e].reshape(KV_LORA, hw)],
                                  axis=1).astype(BF16)
            gq, gkv = _row(mla_gq[e]), _row(mla_gkv[e])
            cw = jnp.pad(dn_conv_w[e].astype(F32), ((0, 8 - DN_CONV), (0, 0)))
            ab = jnp.concatenate([_row(dn_a_log[e], LANE), _row(dn_dt_bias[e], LANE),
                                  jnp.zeros((6, LANE), F32)], axis=0)
            nw = _row(dn_norm_w[e])
            wo = w_out[e].astype(BF16)
            wo_mla, wo_dn = wo[:hw], wo[hw:]

            proj_s = norm_matmul(hs, g0, w_proj, zero_proj, 1024, 768)
            qn_s, qr_s, ckv_s, kn_s, v_s, kro_s, krp_s = mla_prep(proj_s, gq, gkv, wq, wkv, *tab_s, 512)
            lh = n_meta + past
            lh_pad = -(-lh // LANE) * LANE
            hist_ckv = jnp.pad(cache_mla_ckv[e].astype(F32), ((0, 1), (0, lh_pad - lh), (0, 0)))
            hkn, hv = kv_up(hist_ckv.reshape(ns * lh_pad, KV_LORA), wkv, 128)
            hkrp = jnp.pad(cache_mla_krope[e].astype(BF16),
                           ((0, 1), (0, lh_pad - lh), (0, LANE - QK_ROPE)))
            hlen_s = jnp.concatenate([jnp.full((bs,), lh, jnp.int32), jnp.zeros((1,), jnp.int32)])
            ymla_s = attention(qn_s, qr_s, kn_s, krp_s, v_s, hkn.reshape(ns, lh_pad, hw), hkrp,
                               hv.reshape(ns, lh_pad, hw), hlen_s, ns, ls, ls)
            conv_hist_s = jnp.pad(state_dn_conv[e].astype(F32), ((0, 1), (8 - (DN_CONV - 1), 0), (0, 0)))
            s0_s = jnp.pad(state_dn_s[e].astype(F32), ((0, 1), (0, 0), (0, 0), (0, 0)))
            ydn_s, sfin_s = deltanet(proj_s, cw, conv_hist_s, s0_s, ab, nw, ns, ls, CHUNK, 1)
            hs = matmul_resnorm([ymla_s, ydn_s], [wo_mla, wo_dn], zero_d, g1, hs, 512)

            proj_p = norm_matmul(hp, g0, w_proj, zero_proj, 1024, 768)
            qn_p, qr_p, ckv_p, kn_p, v_p, kro_p, krp_p = mla_prep(proj_p, gq, gkv, wq, wkv, *tab_p, 512)
            hlen_p = jnp.full((bp,), n_meta, jnp.int32)
            ymla_p = attention(qn_p, qr_p, kn_p, krp_p, v_p, kn_s[meta_rows][None], krp_s[meta_rows][None],
                               v_s[meta_rows][None], hlen_p, bp, lp, 256)
            conv_hist_p = jnp.pad(proj_s[meta_rows, COL_QKV:COL_QKV + DN_QKV][-(DN_CONV - 1):],
                                  ((8 - (DN_CONV - 1), 0), (0, 0)))[None]
            ydn_p, sfin_p = deltanet(proj_p, cw, conv_hist_p, sfin_s[bs:], ab, nw, bp, lp, CHUNK, 2)
            hp = matmul_resnorm([ymla_p, ydn_p], [wo_mla, wo_dn], zero_d, g1, hp, 512)

            bc = lambda a: jnp.broadcast_to(a[None], (bp,) + a.shape)
            outs["p_ckv"].append(jnp.concatenate([bc(ckv_s[meta_rows]), ckv_p.reshape(bp, lp, KV_LORA)], axis=1))
            outs["p_kr"].append(jnp.concatenate([bc(kro_s[meta_rows]), kro_p.reshape(bp, lp, QK_ROPE)], axis=1))
            outs["p_s"].append(sfin_p)
            outs["p_conv"].append(proj_p.reshape(bp, lp, PROJ_W)[:, lp - (DN_CONV - 1):, COL_QKV:COL_QKV + DN_QKV])
            outs["s_ckv"].append(ckv_s[:bs * ls].reshape(bs, ls, KV_LORA))
            outs["s_kr"].append(kro_s[:bs * ls].reshape(bs, ls, QK_ROPE))
            outs["s_s"].append(sfin_s[:bs])
            xqkv_s = proj_s[:bs * ls, COL_QKV:COL_QKV + DN_QKV].reshape(bs, ls, DN_QKV)
            outs["s_conv"].append(jnp.concatenate([state_dn_conv[e].astype(F32), xqkv_s], axis=1)[:, -(DN_CONV - 1):])
        else:
            o = layer // 2
            w1 = cf_w_pw1[o].astype(BF16)
            b1 = _row(cf_b_pw1[o])
            wdw = jnp.pad(cf_w_dw[o].astype(F32), ((0, CF_HALO - CF_KERNEL), (0, 0)))
            bdw, lg, lb = _row(cf_b_dw[o]), _row(cf_ln_g[o]), _row(cf_ln_b[o])
            w2 = cf_w_pw2[o].astype(BF16)
            b2 = _row(cf_b_pw2[o])
            keep = CF_KERNEL - 1

            u_s = norm_glu(hs, g0, w1, b1, 512, 512)
            hist_s = jnp.pad(state_cf_conv[o].astype(F32), ((0, 1), (CF_HALO - keep, 0), (0, 0)))
            c_s = cf_conv(u_s, hist_s, wdw, bdw, lg, lb, ns, ls, 128)
            hs = matmul_resnorm([c_s], [w2], b2, g1, hs, 512)

            u_p = norm_glu(hp, g0, w1, b1, 512, 512)
            hist_p = jnp.pad(u_s[meta_rows], ((CF_HALO - n_meta, 0), (0, 0)))[None]
            c_p = cf_conv(u_p, hist_p, wdw, bdw, lg, lb, bp, lp, 128)
            hp = matmul_resnorm([c_p], [w2], b2, g1, hp, 512)

            u_p3 = u_p.reshape(bp, lp, d)
            meta_u = jnp.broadcast_to(u_s[meta_rows][None], (bp, n_meta, d))
            outs["p_cf"].append(jnp.concatenate([jnp.zeros((bp, keep, d), F32), meta_u, u_p3[:, max(lp - keep, 0):]],
                                                axis=1)[:, -keep:])
            outs["s_cf"].append(jnp.concatenate([state_cf_conv[o].astype(F32),
                                                 u_s[:bs * ls].reshape(bs, ls, d)], axis=1)[:, -keep:])
        wg, wu, wd = w_gate[layer].astype(BF16), w_up[layer].astype(BF16), w_down[layer].astype(BF16)
        hs = ffn(hs, g2, wg, wu, wd, g3, 512, 512)
        hp = ffn(hp, g2, wg, wu, wd, g3, 512, 512)

    y_prompt = hp.reshape(bp, lp, d)
    y_sample = hs[:bs * ls].reshape(bs, ls, d)
    st = lambda k: jnp.stack(outs[k])
    return (y_prompt, y_sample, st("p_ckv"), st("p_kr"), st("p_s"), st("p_conv"), st("p_cf"),
            st("s_ckv"), st("s_kr"), st("s_s"), st("s_conv"), st("s_cf"))
```

```python
import functools

import numpy as np
import jax
import jax.numpy as jnp
from jax import lax
from jax.experimental import pallas as pl
from jax.experimental.pallas import tpu as pltpu

F32 = jnp.float32
BF16 = jnp.bfloat16

EPS = 1e-6
NEG_INF = -1e30
CHUNK = 64
N_META = 16
MLA_HEADS = 8
Q_LORA = 512
KV_LORA = 512
QK_NOPE = 128
QK_ROPE = 64
V_HEAD = 128
ROPE_THETA = 10000.0
DN_HEADS = 8
DN_DK = 128
DN_DV = 128
DN_CONV = 4
DN_QKV = DN_HEADS * (2 * DN_DK + DN_DV)
CF_KERNEL = 31

LANE = 128
ATT_TILE = 256
VMEM_LIMIT = 56 * 1024 * 1024

COL_QKV = 0
COL_Z = DN_QKV
COL_QD = COL_Z + DN_HEADS * DN_DV
COL_KVD = COL_QD + Q_LORA
COL_SMALL = COL_KVD + KV_LORA
SMALL_W = 256
PROJ_W = COL_SMALL + SMALL_W


def _cparams(sem):
    return pltpu.CompilerParams(dimension_semantics=sem, vmem_limit_bytes=VMEM_LIMIT)


def _rms(x, g):
    return x * lax.rsqrt(jnp.mean(x * x, axis=-1, keepdims=True) + EPS) * g


def _sigmoid(x):
    return 1.0 / (1.0 + jnp.exp(-x))


def _dot(a, b):
    return jnp.dot(a, b, preferred_element_type=F32)


def _dot_nt(a, b):
    return lax.dot_general(a, b, (((1,), (1,)), ((), ())), preferred_element_type=F32)


def _dot_tn(a, b):
    return lax.dot_general(a, b, (((0,), (0,)), ((), ())), preferred_element_type=F32)


def _row_tile(m, pref):
    t = min(pref, m)
    while m % t:
        t //= 2
    return t


def _norm_mm_kernel(x_ref, g_ref, w_ref, b_ref, o_ref, xn_ref):
    @pl.when(pl.program_id(1) == 0)
    def _():
        xn_ref[...] = _rms(x_ref[...], g_ref[...]).astype(BF16)

    o_ref[...] = (_dot(xn_ref[...], w_ref[...]) + b_ref[...]).astype(o_ref.dtype)


def _norm_glu_kernel(x_ref, g_ref, wa_ref, wg_ref, ba_ref, bg_ref, o_ref, xn_ref):
    @pl.when(pl.program_id(1) == 0)
    def _():
        xn_ref[...] = _rms(x_ref[...], g_ref[...]).astype(BF16)

    xn = xn_ref[...]
    a = _dot(xn, wa_ref[...]) + ba_ref[...]
    gt = _dot(xn, wg_ref[...]) + bg_ref[...]
    o_ref[...] = (a * _sigmoid(gt)).astype(o_ref.dtype)


def norm_matmul(x, g, w, b, tm, tn, out_dtype=F32):
    m, k = x.shape
    n = w.shape[1]
    tm = _row_tile(m, tm)
    return pl.pallas_call(
        _norm_mm_kernel,
        grid=(m // tm, n // tn),
        in_specs=[pl.BlockSpec((tm, k), lambda i, j: (i, 0)),
                  pl.BlockSpec((1, k), lambda i, j: (0, 0)),
                  pl.BlockSpec((k, tn), lambda i, j: (0, j)),
                  pl.BlockSpec((1, tn), lambda i, j: (0, j))],
        out_specs=pl.BlockSpec((tm, tn), lambda i, j: (i, j)),
        out_shape=jax.ShapeDtypeStruct((m, n), out_dtype),
        scratch_shapes=[pltpu.VMEM((tm, k), BF16)],
        compiler_params=_cparams(("parallel", "arbitrary")),
        name="norm_matmul",
    )(x, g, w, b)


def norm_glu(x, g, w, b, tm, tn):
    m, k = x.shape
    n = w.shape[1] // 2
    tm = _row_tile(m, tm)
    nb = n // tn
    return pl.pallas_call(
        _norm_glu_kernel,
        grid=(m // tm, nb),
        in_specs=[pl.BlockSpec((tm, k), lambda i, j: (i, 0)),
                  pl.BlockSpec((1, k), lambda i, j: (0, 0)),
                  pl.BlockSpec((k, tn), lambda i, j: (0, j)),
                  pl.BlockSpec((k, tn), lambda i, j: (0, j + nb)),
                  pl.BlockSpec((1, tn), lambda i, j: (0, j)),
                  pl.BlockSpec((1, tn), lambda i, j: (0, j + nb))],
        out_specs=pl.BlockSpec((tm, tn), lambda i, j: (i, j)),
        out_shape=jax.ShapeDtypeStruct((m, n), F32),
        scratch_shapes=[pltpu.VMEM((tm, k), BF16)],
        compiler_params=_cparams(("parallel", "arbitrary")),
        name="norm_glu",
    )(x, g, w, w, b, b)


def _mm_resnorm_kernel(*refs, n_in):
    xs = refs[:n_in]
    ws = refs[n_in:2 * n_in]
    b_ref, g_ref, res_ref, o_ref = refs[2 * n_in:]
    y = b_ref[...]
    for x_ref, w_ref in zip(xs, ws):
        y = y + _dot(x_ref[...], w_ref[...])
    o_ref[...] = res_ref[...] + _rms(y, g_ref[...])


def matmul_resnorm(xs, ws, b, g, res, tm):
    m, n = res.shape
    tm = _row_tile(m, tm)
    n_in = len(xs)
    in_specs = ([pl.BlockSpec((tm, x.shape[1]), lambda i: (i, 0)) for x in xs]
                + [pl.BlockSpec(w.shape, lambda i: (0, 0)) for w in ws]
                + [pl.BlockSpec((1, n), lambda i: (0, 0)),
                   pl.BlockSpec((1, n), lambda i: (0, 0)),
                   pl.BlockSpec((tm, n), lambda i: (i, 0))])
    return pl.pallas_call(
        functools.partial(_mm_resnorm_kernel, n_in=n_in),
        grid=(m // tm,),
        in_specs=in_specs,
        out_specs=pl.BlockSpec((tm, n), lambda i: (i, 0)),
        out_shape=jax.ShapeDtypeStruct((m, n), F32),
        compiler_params=_cparams(("parallel",)),
        name="matmul_resnorm",
    )(*xs, *ws, b, g, res)


def _ffn_kernel(h_ref, g2_ref, wg_ref, wu_ref, wd_ref, g3_ref, o_ref, xn_ref, acc_ref):
    j = pl.program_id(1)

    @pl.when(j == 0)
    def _():
        xn_ref[...] = _rms(h_ref[...], g2_ref[...]).astype(BF16)
        acc_ref[...] = jnp.zeros_like(acc_ref)

    xn = xn_ref[...]
    gate = _dot(xn, wg_ref[...])
    up = _dot(xn, wu_ref[...])
    a = (gate * _sigmoid(gate) * up).astype(BF16)
    acc_ref[...] += _dot(a, wd_ref[...])

    @pl.when(j == pl.num_programs(1) - 1)
    def _():
        o_ref[...] = h_ref[...] + _rms(acc_ref[...], g3_ref[...])


def ffn(h, g2, wg, wu, wd, g3, layer, tm, tf):
    m, d = h.shape
    f = wg.shape[2]
    tm = _row_tile(m, tm)
    return pl.pallas_call(
        _ffn_kernel,
        grid=(m // tm, f // tf),
        in_specs=[pl.BlockSpec((tm, d), lambda i, j: (i, 0)),
                  pl.BlockSpec((1, d), lambda i, j: (0, 0)),
                  pl.BlockSpec((None, d, tf), lambda i, j: (layer, 0, j)),
                  pl.BlockSpec((None, d, tf), lambda i, j: (layer, 0, j)),
                  pl.BlockSpec((None, tf, d), lambda i, j: (layer, j, 0)),
                  pl.BlockSpec((1, d), lambda i, j: (0, 0))],
        out_specs=pl.BlockSpec((tm, d), lambda i, j: (i, 0)),
        out_shape=jax.ShapeDtypeStruct((m, d), F32),
        scratch_shapes=[pltpu.VMEM((tm, d), BF16), pltpu.VMEM((tm, d), F32)],
        compiler_params=_cparams(("parallel", "arbitrary")),
        name="ffn",
    )(h, g2, wg, wu, wd, g3)


def _mla_prep_kernel(qd_ref, kvd_ref, sm_ref, gq_ref, gkv_ref, wq_ref, wk_ref, wv_ref, cq_ref, sq_ref, ck_ref,
                     qn_ref, qr_ref, ckv_ref, kn_ref, v_ref, kro_ref, krp_ref, *, v_transposed):
    hw = MLA_HEADS * QK_NOPE
    scale = (QK_NOPE + QK_ROPE) ** -0.5
    cq = _rms(qd_ref[...], gq_ref[...]).astype(BF16)
    q = _dot(cq, wq_ref[...])
    qn_ref[...] = (q[:, :hw] * scale).astype(BF16)
    cos8 = jnp.tile(cq_ref[...], (1, MLA_HEADS))
    sin8 = jnp.tile(sq_ref[...], (1, MLA_HEADS))
    qr_ref[...] = ((q[:, hw:2 * hw] * cos8 + q[:, 2 * hw:] * sin8) * scale).astype(BF16)
    ckv = _rms(kvd_ref[...], gkv_ref[...])
    ckv_ref[...] = ckv
    ckv_b = ckv.astype(BF16)
    kn_ref[...] = _dot(ckv_b, wk_ref[...]).astype(BF16)
    if v_transposed:
        vt = _dot_nt(wv_ref[...], ckv_b).astype(BF16)
        for t in range(v_ref.shape[0]):
            v_ref[t] = vt[:, t * ATT_TILE:(t + 1) * ATT_TILE]
    else:
        v_ref[...] = _dot(ckv_b, wv_ref[...]).astype(BF16)
    y = sm_ref[:, :LANE] * ck_ref[...]
    kro = y + pltpu.roll(y, QK_ROPE, 1)
    kro_ref[...] = kro[:, :QK_ROPE]
    lane = lax.broadcasted_iota(jnp.int32, kro.shape, 1)
    krp_ref[...] = jnp.where(lane < QK_ROPE, kro, 0.0).astype(BF16)


def mla_prep(proj, gq, gkv, wq, wk, wv, cq_tab, sq_tab, ck_tab, tm, v_transposed):
    m = proj.shape[0]
    tm = _row_tile(min(m, cq_tab.shape[0]), tm)
    nt = cq_tab.shape[0] // tm
    hw = MLA_HEADS * QK_NOPE
    tab = lambda: pl.BlockSpec((tm, LANE), lambda i: (i % nt, 0))
    full = lambda a: pl.BlockSpec(a.shape, lambda i: (0, 0))
    if v_transposed:
        assert tm % ATT_TILE == 0
        v_spec = pl.BlockSpec((tm // ATT_TILE, hw, ATT_TILE), lambda i: (i, 0, 0))
        v_shape = jax.ShapeDtypeStruct((m // ATT_TILE, hw, ATT_TILE), BF16)
    else:
        v_spec = pl.BlockSpec((tm, hw), lambda i: (i, 0))
        v_shape = jax.ShapeDtypeStruct((m, hw), BF16)
    return pl.pallas_call(
        functools.partial(_mla_prep_kernel, v_transposed=v_transposed),
        grid=(m // tm,),
        in_specs=[pl.BlockSpec((tm, Q_LORA), lambda i: (i, COL_QD // Q_LORA)),
                  pl.BlockSpec((tm, KV_LORA), lambda i: (i, COL_KVD // KV_LORA)),
                  pl.BlockSpec((tm, SMALL_W), lambda i: (i, COL_SMALL // SMALL_W)),
                  full(gq), full(gkv), full(wq), full(wk), full(wv), tab(), tab(), tab()],
        out_specs=[pl.BlockSpec((tm, hw), lambda i: (i, 0)),
                   pl.BlockSpec((tm, hw), lambda i: (i, 0)),
                   pl.BlockSpec((tm, KV_LORA), lambda i: (i, 0)),
                   pl.BlockSpec((tm, hw), lambda i: (i, 0)),
                   v_spec,
                   pl.BlockSpec((tm, QK_ROPE), lambda i: (i, 0)),
                   pl.BlockSpec((tm, LANE), lambda i: (i, 0))],
        out_shape=[jax.ShapeDtypeStruct((m, hw), BF16),
                   jax.ShapeDtypeStruct((m, hw), BF16),
                   jax.ShapeDtypeStruct((m, KV_LORA), F32),
                   jax.ShapeDtypeStruct((m, hw), BF16),
                   v_shape,
                   jax.ShapeDtypeStruct((m, QK_ROPE), F32),
                   jax.ShapeDtypeStruct((m, LANE), BF16)],
        compiler_params=_cparams(("parallel",)),
        name="mla_prep",
    )(proj, proj, proj, gq, gkv, wq, wk, wv, cq_tab, sq_tab, ck_tab)


def _kv_up_kernel(ckv_ref, wkv_ref, kn_ref, v_ref):
    hw = MLA_HEADS * QK_NOPE
    kv = _dot(ckv_ref[...].astype(BF16), wkv_ref[...])
    kn_ref[...] = kv[:, :hw].astype(BF16)
    v_ref[...] = kv[:, hw:].astype(BF16)


def kv_up(ckv, wkv, tm):
    m = ckv.shape[0]
    tm = _row_tile(m, tm)
    hw = MLA_HEADS * QK_NOPE
    return pl.pallas_call(
        _kv_up_kernel,
        grid=(m // tm,),
        in_specs=[pl.BlockSpec((tm, KV_LORA), lambda i: (i, 0)),
                  pl.BlockSpec(wkv.shape, lambda i: (0, 0))],
        out_specs=[pl.BlockSpec((tm, hw), lambda i: (i, 0)),
                   pl.BlockSpec((tm, hw), lambda i: (i, 0))],
        out_shape=[jax.ShapeDtypeStruct((m, hw), BF16), jax.ShapeDtypeStruct((m, hw), BF16)],
        compiler_params=_cparams(("parallel",)),
        name="kv_up",
    )(ckv, wkv)


def _attn_kernel(hlen_ref, qn_ref, qr_ref, kn_ref, kr_ref, v_ref, hkn_ref, hkr_ref, hv_ref, o_ref,
                 m_ref, l_ref, acc_ref, a_ref, qc_ref, s_ref, p_ref, hs_ref, hp_ref, *, tile, lh):
    s_idx = pl.program_id(0)
    qt = pl.program_id(1)
    hlen = hlen_ref[s_idx]
    nh = MLA_HEADS
    heads = [slice(h * LANE, (h + 1) * LANE) for h in range(nh)]

    for h, hs in enumerate(heads):
        qc_ref[h] = jnp.concatenate([qn_ref[:, hs], qr_ref[:, hs]], axis=1)

    def process(get_k, get_v, s_scr, p_scr, valid, first):
        for h in range(nh):
            s_scr[h] = _dot_nt(qc_ref[h], get_k(h))
        for h, hs in enumerate(heads):
            s = s_scr[h]
            if valid is not None:
                s = jnp.where(valid, s, NEG_INF)
            m_cur = jnp.max(s, axis=1, keepdims=True)
            if first:
                m_new = jnp.broadcast_to(m_cur, (tile, LANE))
            else:
                m_prev = m_ref[:, hs]
                m_new = jnp.maximum(m_prev, m_cur)
            p = jnp.exp(s - m_new[:, :1])
            if valid is not None:
                p = jnp.where(valid, p, 0.0)
            p_scr[h] = p.astype(BF16)
            l_cur = jnp.sum(p, axis=1, keepdims=True)
            if first:
                l_ref[:, hs] = jnp.broadcast_to(l_cur, (tile, LANE))
            else:
                alpha = jnp.exp(m_prev - m_new)
                a_ref[:, hs] = alpha
                l_ref[:, hs] = alpha * l_ref[:, hs] + l_cur
            m_ref[:, hs] = m_new
        for h, hs in enumerate(heads):
            pv = _dot(p_scr[h], get_v(h))
            if first:
                acc_ref[:, hs] = pv
            else:
                acc_ref[:, hs] = a_ref[:, hs] * acc_ref[:, hs] + pv

    hvalid = lax.broadcasted_iota(jnp.int32, (tile, lh), 1) < hlen
    process(lambda h: jnp.concatenate([hkn_ref[:, heads[h]], hkr_ref[...]], axis=1),
            lambda h: hv_ref[:, heads[h]], hs_ref, hp_ref, hvalid, True)

    def own_tile(kt, valid):
        rows = pl.ds(pl.multiple_of(kt * tile, tile), tile)
        process(lambda h: jnp.concatenate([kn_ref[rows, heads[h]], kr_ref[rows, :]], axis=1),
                lambda h: v_ref[rows, heads[h]], s_ref, p_ref, valid, False)

    def body(kt, carry):
        own_tile(kt, None)
        return carry

    lax.fori_loop(0, qt, body, 0)
    if tile > CHUNK:
        row = lax.broadcasted_iota(jnp.int32, (tile, tile), 0) // CHUNK
        col = lax.broadcasted_iota(jnp.int32, (tile, tile), 1) // CHUNK
        dvalid = col <= row
    else:
        dvalid = None
    own_tile(qt, dvalid)

    for hs in heads:
        o_ref[:, hs] = (acc_ref[:, hs] / l_ref[:, hs]).astype(o_ref.dtype)


def attention(qn, qr, kn, krp, v, hkn, hkrp, hv, hlen, nseq, seq_len, tile):
    hw = MLA_HEADS * LANE
    tile = min(tile, seq_len)
    assert seq_len % tile == 0 and (tile % CHUNK == 0 or seq_len == tile <= CHUNK)
    nqt = seq_len // tile
    nhs, lh = hkn.shape[0], hkn.shape[1]
    assert nhs in (1, nseq)
    hidx = (lambda s, q, hl: (s, 0, 0)) if nhs > 1 else (lambda s, q, hl: (0, 0, 0))
    grid_spec = pltpu.PrefetchScalarGridSpec(
        num_scalar_prefetch=1,
        grid=(nseq, nqt),
        in_specs=[pl.BlockSpec((tile, hw), lambda s, q, hl: (s * nqt + q, 0)),
                  pl.BlockSpec((tile, hw), lambda s, q, hl: (s * nqt + q, 0)),
                  pl.BlockSpec((seq_len, hw), lambda s, q, hl: (s, 0)),
                  pl.BlockSpec((seq_len, LANE), lambda s, q, hl: (s, 0)),
                  pl.BlockSpec((seq_len, hw), lambda s, q, hl: (s, 0)),
                  pl.BlockSpec((None, lh, hw), hidx),
                  pl.BlockSpec((None, lh, LANE), hidx),
                  pl.BlockSpec((None, lh, hw), hidx)],
        out_specs=pl.BlockSpec((tile, hw), lambda s, q, hl: (s * nqt + q, 0)),
        scratch_shapes=[pltpu.VMEM((tile, hw), F32), pltpu.VMEM((tile, hw), F32),
                        pltpu.VMEM((tile, hw), F32), pltpu.VMEM((tile, hw), F32),
                        pltpu.VMEM((MLA_HEADS, tile, 2 * LANE), BF16),
                        pltpu.VMEM((MLA_HEADS, tile, tile), F32),
                        pltpu.VMEM((MLA_HEADS, tile, tile), BF16),
                        pltpu.VMEM((MLA_HEADS, tile, lh), F32),
                        pltpu.VMEM((MLA_HEADS, tile, lh), BF16)],
    )
    return pl.pallas_call(
        functools.partial(_attn_kernel, tile=tile, lh=lh),
        grid_spec=grid_spec,
        out_shape=jax.ShapeDtypeStruct((nseq * seq_len, hw), BF16),
        compiler_params=_cparams(("parallel", "arbitrary")),
        name="attention",
    )(hlen, qn, qr, kn, krp, v, hkn, hkrp, hv)


def _attn_t_kernel(hlen_ref, qn_ref, qr_ref, kn_ref, kr_ref, vt_ref, hkn_ref, hkr_ref, hvt_ref, o_ref,
                   m_ref, l_ref, a_ref, acc_ref, qc_ref, s_ref, p_ref, hs_ref, hp_ref, *, tile, lh):
    s_idx = pl.program_id(0)
    qt = pl.program_id(1)
    hlen = hlen_ref[s_idx]
    nh = MLA_HEADS
    heads = [slice(h * LANE, (h + 1) * LANE) for h in range(nh)]

    for h, hs in enumerate(heads):
        qc_ref[h] = jnp.concatenate([qn_ref[:, hs], qr_ref[:, hs]], axis=1)

    def process(get_k, get_vt, s_scr, p_scr, valid, first):
        for h in range(nh):
            s_scr[h] = _dot_nt(get_k(h), qc_ref[h])
        for h in range(nh):
            s = s_scr[h]
            if valid is not None:
                s = jnp.where(valid, s, NEG_INF)
            m_cur = jnp.max(s, axis=0, keepdims=True)
            if first:
                m_new = m_cur
            else:
                m_prev = m_ref[h]
                m_new = jnp.maximum(m_prev, m_cur)
            p = jnp.exp(s - m_new)
            if valid is not None:
                p = jnp.where(valid, p, 0.0)
            p_scr[h] = p.astype(BF16)
            l_cur = jnp.sum(p, axis=0, keepdims=True)
            if first:
                l_ref[h] = l_cur
            else:
                alpha = jnp.exp(m_prev - m_new)
                a_ref[h] = alpha
                l_ref[h] = alpha * l_ref[h] + l_cur
            m_ref[h] = m_new
        for h in range(nh):
            pv = _dot(get_vt(h), p_scr[h])
            if first:
                acc_ref[h] = pv
            else:
                acc_ref[h] = a_ref[h] * acc_ref[h] + pv

    hvalid = lax.broadcasted_iota(jnp.int32, (lh, tile), 0) < hlen
    process(lambda h: jnp.concatenate([hkn_ref[:, heads[h]], hkr_ref[...]], axis=1),
            lambda h: hvt_ref[heads[h], :], hs_ref, hp_ref, hvalid, True)

    def own_tile(kt, valid):
        rows = pl.ds(pl.multiple_of(kt * tile, tile), tile)
        process(lambda h: jnp.concatenate([kn_ref[rows, heads[h]], kr_ref[rows, :]], axis=1),
                lambda h: vt_ref[kt, heads[h], :], s_ref, p_ref, valid, False)

    def body(kt, carry):
        own_tile(kt, None)
        return carry

    lax.fori_loop(0, qt, body, 0)
    key_chunk = lax.broadcasted_iota(jnp.int32, (tile, tile), 0) // CHUNK
    qry_chunk = lax.broadcasted_iota(jnp.int32, (tile, tile), 1) // CHUNK
    own_tile(qt, key_chunk <= qry_chunk)

    for h, hs in enumerate(heads):
        o_ref[:, hs] = (acc_ref[h] / l_ref[h]).T.astype(o_ref.dtype)


def attention_t(qn, qr, kn, krp, vt, hkn, hkrp, hvt, hlen, nseq, seq_len):
    hw = MLA_HEADS * LANE
    tile = ATT_TILE
    assert seq_len % tile == 0 and tile % CHUNK == 0
    nqt = seq_len // tile
    lh = hkn.shape[0]
    nh = MLA_HEADS
    grid_spec = pltpu.PrefetchScalarGridSpec(
        num_scalar_prefetch=1,
        grid=(nseq, nqt),
        in_specs=[pl.BlockSpec((tile, hw), lambda s, q, hl: (s * nqt + q, 0)),
                  pl.BlockSpec((tile, hw), lambda s, q, hl: (s * nqt + q, 0)),
                  pl.BlockSpec((seq_len, hw), lambda s, q, hl: (s, 0)),
                  pl.BlockSpec((seq_len, LANE), lambda s, q, hl: (s, 0)),
                  pl.BlockSpec((nqt, hw, tile), lambda s, q, hl: (s, 0, 0)),
                  pl.BlockSpec((lh, hw), lambda s, q, hl: (0, 0)),
                  pl.BlockSpec((lh, LANE), lambda s, q, hl: (0, 0)),
                  pl.BlockSpec((hw, lh), lambda s, q, hl: (0, 0))],
        out_specs=pl.BlockSpec((tile, hw), lambda s, q, hl: (s * nqt + q, 0)),
        scratch_shapes=[pltpu.VMEM((nh, 1, tile), F32), pltpu.VMEM((nh, 1, tile), F32),
                        pltpu.VMEM((nh, 1, tile), F32),
                        pltpu.VMEM((nh, V_HEAD, tile), F32),
                        pltpu.VMEM((nh, tile, 2 * LANE), BF16),
                        pltpu.VMEM((nh, tile, tile), F32),
                        pltpu.VMEM((nh, tile, tile), BF16),
                        pltpu.VMEM((nh, lh, tile), F32),
                        pltpu.VMEM((nh, lh, tile), BF16)],
    )
    return pl.pallas_call(
        functools.partial(_attn_t_kernel, tile=tile, lh=lh),
        grid_spec=grid_spec,
        out_shape=jax.ShapeDtypeStruct((nseq * seq_len, hw), BF16),
        compiler_params=_cparams(("parallel", "arbitrary")),
        name="attention_t",
    )(hlen, qn, qr, kn, krp, vt, hkn, hkrp, hvt)


def _dn_prep_kernel(x_ref, sm_ref, cw_ref, hist_ref, ab_ref,
                    u_ref, w_ref, qd_ref, kd_ref, attn_ref, gl_ref,
                    ext_ref, act_ref, lhs_ref, kb_ref, dec_ref, rhs_ref, mm_ref, t_ref, xs_ref, *, c, g):
    j = pl.program_id(1)
    nh, dk, dv = DN_HEADS, DN_DK, DN_DV
    kw = nh * dk
    rows_all = c * g
    inst = [(b, h) for b in range(g) for h in range(nh)]

    @pl.when(j == 0)
    def _():
        ext_ref[0:8, :] = hist_ref[...]

    ext_ref[8:8 + rows_all, :] = x_ref[...]
    conv = cw_ref[0:1, :] * ext_ref[5:5 + rows_all, :]
    for t in range(1, DN_CONV):
        conv = conv + cw_ref[t:t + 1, :] * ext_ref[5 + t:5 + t + rows_all, :]
    ext_ref[0:8, :] = ext_ref[rows_all:rows_all + 8, :]
    act_ref[...] = conv * _sigmoid(conv)

    gates = sm_ref[:, LANE:2 * LANE]
    xa = gates + ab_ref[1:2, :]
    softplus = jnp.maximum(xa, 0.0) + jnp.log(1.0 + jnp.exp(-jnp.abs(xa)))
    g_all = -jnp.exp(ab_ref[0:1, :]) * softplus
    beta_all = _sigmoid(gates)

    r = lax.broadcasted_iota(jnp.int32, (c, c), 0)
    q = lax.broadcasted_iota(jnp.int32, (c, c), 1)
    incl = r >= q
    strict = r > q
    eye = jnp.where(r == q, 1.0, 0.0)
    tri = jnp.where(incl, 1.0, 0.0)

    for b in range(g):
        rows = slice(b * c, (b + 1) * c)
        gc = jnp.dot(tri, g_all[rows], preferred_element_type=F32, precision=lax.Precision.HIGHEST)
        if c < LANE:
            gc_sq = jnp.concatenate([gc, jnp.zeros((LANE - c, LANE), F32)], axis=0)
        else:
            gc_sq = gc
        gc_t = gc_sq.T
        egc = jnp.exp(gc)
        glast = gc[c - 1:c, :]
        edl = jnp.exp(glast - gc)
        gl_ref[rows, :] = jnp.broadcast_to(glast, (c, LANE))
        beta_b = beta_all[rows]
        for h in range(nh):
            i = b * nh + h
            hs = slice(h * dk, (h + 1) * dk)
            qh = act_ref[rows, hs]
            kh = act_ref[rows, kw + h * dk:kw + (h + 1) * dk]
            vh = act_ref[rows, 2 * kw + h * dv:2 * kw + (h + 1) * dv]
            qh = qh * (lax.rsqrt(jnp.sum(qh * qh, axis=1, keepdims=True) + EPS) * dk ** -0.5)
            kh = kh * lax.rsqrt(jnp.sum(kh * kh, axis=1, keepdims=True) + EPS)
            bcol = beta_b[:, 8 + h:9 + h]
            gcol = gc[:, h:h + 1]
            grow = gc_t[h:h + 1, :c]
            dec_ref[i] = jnp.where(incl, jnp.exp(jnp.where(incl, gcol - grow, 0.0)), 0.0)
            kb = kh * bcol
            ecol = egc[:, h:h + 1]
            lhs_ref[i] = jnp.concatenate([kb, qh], axis=0).astype(BF16)
            kb_ref[i] = kh.astype(BF16)
            rhs_ref[i] = jnp.concatenate([vh * bcol, kb * ecol], axis=1).astype(BF16)
            qd_ref[rows, hs] = (qh * ecol).astype(BF16)
            kd_ref[rows, hs] = (kh * edl[:, h:h + 1]).astype(BF16)

    for i, (b, h) in enumerate(inst):
        kq = _dot_nt(lhs_ref[i], kb_ref[i])
        dec = dec_ref[i]
        mm = jnp.where(strict, kq[:c] * dec, 0.0)
        mm_ref[i] = mm
        t_ref[i] = eye - jnp.where((r ^ q) == 1, mm, 0.0)
        attn_ref[b, h] = (kq[c:] * dec).astype(BF16)

    s = 2
    while s < c:
        sh = s.bit_length() - 1
        sel = ((r >> sh) ^ (q >> sh)) == 1
        for i in range(len(inst)):
            e = jnp.where(sel, mm_ref[i], 0.0).astype(BF16)
            xs_ref[i] = _dot(e, t_ref[i].astype(BF16)).astype(BF16)
        for i in range(len(inst)):
            t = t_ref[i]
            t_ref[i] = t - _dot(t.astype(BF16), xs_ref[i])
        s *= 2

    for i, (b, h) in enumerate(inst):
        rows = slice(b * c, (b + 1) * c)
        hs = slice(h * dk, (h + 1) * dk)
        uw = _dot(t_ref[i].astype(BF16), rhs_ref[i])
        u_ref[rows, hs] = uw[:, :dv]
        w_ref[rows, hs] = uw[:, dv:].astype(BF16)


def _dn_scan_kernel(u_ref, w_ref, qd_ref, kd_ref, attn_ref, gl_ref, z_ref, s0_ref, nw_ref,
                    y_ref, sfin_ref, s_ref, rr_ref, vn_ref, *, c, nseq, shared_s0):
    j = pl.program_id(0)
    nh, dk = DN_HEADS, DN_DK
    inst = [(s, h) for s in range(nseq) for h in range(nh)]
    heads = [slice(h * dk, (h + 1) * dk) for h in range(nh)]

    @pl.when(j == 0)
    def _():
        for s in range(nseq):
            s_ref[s] = s0_ref[0 if shared_s0 else s]

    for i, (s, h) in enumerate(inst):
        lhs = jnp.concatenate([w_ref[s, :, heads[h]], qd_ref[s, :, heads[h]]], axis=0)
        rr_ref[i] = _dot(lhs, s_ref[s, h].astype(BF16))
    for i, (s, h) in enumerate(inst):
        vn_ref[i] = (u_ref[s, :, heads[h]] - rr_ref[i, 0:c, :]).astype(BF16)
    for i, (s, h) in enumerate(inst):
        o = rr_ref[i, c:2 * c, :] + _dot(attn_ref[s, h], vn_ref[i])
        zh = z_ref[s, :, heads[h]]
        y_ref[s, :, heads[h]] = (_rms(o, nw_ref[...]) * (zh * _sigmoid(zh))).astype(y_ref.dtype)
    for i, (s, h) in enumerate(inst):
        ebd = jnp.exp(gl_ref[s, 0:1, h:h + 1])
        s_ref[s, h] = s_ref[s, h] * ebd + _dot_tn(kd_ref[s, :, heads[h]], vn_ref[i])

    @pl.when(j == pl.num_programs(0) - 1)
    def _():
        sfin_ref[...] = s_ref[...]


def deltanet(proj, cw, hist, s0, ab, nw, nseq, seq_len, c, g):
    c = min(c, seq_len)
    nblk = seq_len // c
    g = min(g, nblk)
    assert seq_len % c == 0 and c % 8 == 0 and nblk % g == 0
    ntile = nblk // g
    rows_t = c * g
    nhs = hist.shape[0]
    assert nhs in (1, nseq) and s0.shape[0] == nhs
    hidx3 = (lambda s, j: (s, 0, 0)) if nhs > 1 else (lambda s, j: (0, 0, 0))
    nh, dk, dv = DN_HEADS, DN_DK, DN_DV
    zw = nh * dv
    rows = nseq * seq_len
    ni = g * nh
    row_blk = lambda w: pl.BlockSpec((rows_t, w), lambda s, j: (s * ntile + j, 0))
    u, w, qd, kd, attn, gl = pl.pallas_call(
        functools.partial(_dn_prep_kernel, c=c, g=g),
        grid=(nseq, ntile),
        in_specs=[pl.BlockSpec((rows_t, DN_QKV), lambda s, j: (s * ntile + j, COL_QKV // DN_QKV)),
                  pl.BlockSpec((rows_t, SMALL_W), lambda s, j: (s * ntile + j, COL_SMALL // SMALL_W)),
                  pl.BlockSpec(cw.shape, lambda s, j: (0, 0)),
                  pl.BlockSpec((None, 8, DN_QKV), hidx3),
                  pl.BlockSpec(ab.shape, lambda s, j: (0, 0))],
        out_specs=[row_blk(zw), row_blk(zw), row_blk(zw), row_blk(zw),
                   pl.BlockSpec((g, nh, c, c), lambda s, j: (s * ntile + j, 0, 0, 0)),
                   row_blk(LANE)],
        out_shape=[jax.ShapeDtypeStruct((rows, zw), F32),
                   jax.ShapeDtypeStruct((rows, zw), BF16),
                   jax.ShapeDtypeStruct((rows, zw), BF16),
                   jax.ShapeDtypeStruct((rows, zw), BF16),
                   jax.ShapeDtypeStruct((nseq * nblk, nh, c, c), BF16),
                   jax.ShapeDtypeStruct((rows, LANE), F32)],
        scratch_shapes=[pltpu.VMEM((rows_t + 8, DN_QKV), F32),
                        pltpu.VMEM((rows_t, DN_QKV), F32),
                        pltpu.VMEM((ni, 2 * c, dk), BF16),
                        pltpu.VMEM((ni, c, dk), BF16),
                        pltpu.VMEM((ni, c, c), F32),
                        pltpu.VMEM((ni, c, dk + dv), BF16),
                        pltpu.VMEM((ni, c, c), F32),
                        pltpu.VMEM((ni, c, c), F32),
                        pltpu.VMEM((ni, c, c), BF16)],
        compiler_params=_cparams(("parallel", "arbitrary")),
        name="dn_prep",
    )(proj, proj, cw, hist, ab)

    seq3 = lambda a: a.reshape(nseq, seq_len, a.shape[1])
    blk3 = lambda wdt, col=0: pl.BlockSpec((nseq, c, wdt), lambda j: (0, j, col))
    y, sfin = pl.pallas_call(
        functools.partial(_dn_scan_kernel, c=c, nseq=nseq, shared_s0=nhs == 1),
        grid=(nblk,),
        in_specs=[blk3(zw), blk3(zw), blk3(zw), blk3(zw),
                  pl.BlockSpec((nseq, None, nh, c, c), lambda j: (0, j, 0, 0, 0)),
                  blk3(LANE), blk3(zw, COL_Z // zw),
                  pl.BlockSpec(s0.shape, lambda j: (0, 0, 0, 0)),
                  pl.BlockSpec(nw.shape, lambda j: (0, 0))],
        out_specs=[blk3(zw), pl.BlockSpec((nseq, nh, dk, dv), lambda j: (0, 0, 0, 0))],
        out_shape=[jax.ShapeDtypeStruct((nseq, seq_len, zw), BF16),
                   jax.ShapeDtypeStruct((nseq, nh, dk, dv), F32)],
        scratch_shapes=[pltpu.VMEM((nseq, nh, dk, dv), F32),
                        pltpu.VMEM((nseq * nh, 2 * c, dv), F32),
                        pltpu.VMEM((nseq * nh, c, dv), BF16)],
        compiler_params=_cparams(("arbitrary",)),
        name="dn_scan",
    )(seq3(u), seq3(w), seq3(qd), seq3(kd), attn.reshape(nseq, nblk, nh, c, c), seq3(gl), seq3(proj),
      s0, nw)
    return y.reshape(rows, zw), sfin


CF_HALO = 32


def _cf_conv_kernel(u_ref, hist_ref, w_ref, b_ref, lg_ref, lb_ref, o_ref, ext_ref, sh_ref, acc_ref, *, r):
    j = pl.program_id(1)
    first = CF_HALO - (CF_KERNEL - 1)
    sub = 8
    span = r + CF_HALO - sub

    @pl.when(j == 0)
    def _():
        ext_ref[0:CF_HALO, :] = hist_ref[...]

    ext_ref[CF_HALO:CF_HALO + r, :] = u_ref[...]
    for sft in range(1, sub):
        sh_ref[sft - 1, 0:span, :] = ext_ref[sft:sft + span, :]
    group = 4
    for t0 in range(0, CF_KERNEL, group):
        part = None
        for t in range(t0, min(t0 + group, CF_KERNEL)):
            sft = (first + t) % sub
            base = first + t - sft
            src = ext_ref[base:base + r, :] if sft == 0 else sh_ref[sft - 1, base:base + r, :]
            term = w_ref[t:t + 1, :] * src
            part = term if part is None else part + term
        if t0 == 0:
            acc_ref[...] = part + b_ref[...]
        else:
            acc_ref[...] += part
    if r >= CF_HALO:
        ext_ref[0:CF_HALO, :] = ext_ref[r:r + CF_HALO, :]
    x = acc_ref[...]
    xc = x - jnp.mean(x, axis=-1, keepdims=True)
    y = xc * lax.rsqrt(jnp.mean(xc * xc, axis=-1, keepdims=True) + EPS) * lg_ref[...] + lb_ref[...]
    o_ref[...] = (y * _sigmoid(y)).astype(o_ref.dtype)


def cf_conv(u, hist, w, b, lg, lb, nseq, seq_len, r):
    d = u.shape[1]
    r = min(r, seq_len)
    nblk = seq_len // r
    assert seq_len % r == 0 and (r >= CF_HALO or nblk == 1)
    nhs = hist.shape[0]
    hidx = (lambda s, j: (s, 0, 0)) if nhs > 1 else (lambda s, j: (0, 0, 0))
    vec = lambda: pl.BlockSpec((1, d), lambda s, j: (0, 0))
    return pl.pallas_call(
        functools.partial(_cf_conv_kernel, r=r),
        grid=(nseq, nblk),
        in_specs=[pl.BlockSpec((r, d), lambda s, j: (s * nblk + j, 0)),
                  pl.BlockSpec((None, CF_HALO, d), hidx),
                  pl.BlockSpec(w.shape, lambda s, j: (0, 0)),
                  vec(), vec(), vec()],
        out_specs=pl.BlockSpec((r, d), lambda s, j: (s * nblk + j, 0)),
        out_shape=jax.ShapeDtypeStruct((nseq * seq_len, d), BF16),
        scratch_shapes=[pltpu.VMEM((r + CF_HALO, d), F32), pltpu.VMEM((7, r + CF_HALO - 8, d), F32),
                        pltpu.VMEM((r, d), F32)],
        compiler_params=_cparams(("parallel", "arbitrary")),
        name="cf_conv",
    )(u, hist, w, b, lg, lb)


def _rope_tables(pos):
    half = QK_ROPE // 2
    inv = ROPE_THETA ** (-jnp.arange(half, dtype=F32) / half)
    ang = pos.astype(F32)[:, None] * inv[None, :]
    cos, sin = jnp.cos(ang), jnp.sin(ang)
    zeros = jnp.zeros((pos.shape[0], LANE - QK_ROPE), F32)
    cq = jnp.concatenate([cos, cos, zeros], axis=1)
    sq = jnp.concatenate([sin, sin, zeros], axis=1)
    ck = jnp.concatenate([cos, cos, sin, sin], axis=1)
    return cq, sq, ck


def _rot_cols(w):
    half = w.shape[-1] // 2
    return jnp.concatenate([-w[..., half:], w[..., :half]], axis=-1)


def _row(v, width=None):
    v = v.astype(F32).reshape(1, -1)
    if width is not None and v.shape[1] < width:
        v = jnp.pad(v, ((0, 0), (0, width - v.shape[1])))
    return v


def kernel(x_prompt, x_sample, cache_mla_ckv, cache_mla_krope, state_dn_s, state_dn_conv, state_cf_conv, meta_tokens, norm_gains, w_in, mla_gq, mla_gkv, w_uq, w_uk, w_uv, dn_conv_w, dn_a_log, dn_dt_bias, dn_norm_w, w_out, cf_w_pw1, cf_b_pw1, cf_w_dw, cf_b_dw, cf_ln_g, cf_ln_b, cf_w_pw2, cf_b_pw2, w_gate, w_up, w_down):
    bp, lp, d = x_prompt.shape
    bs, ls, _ = x_sample.shape
    n_meta = meta_tokens.shape[0]
    past = cache_mla_ckv.shape[2] - n_meta
    depth = norm_gains.shape[0]
    assert n_meta == N_META and ls == n_meta and n_meta <= CHUNK
    assert past % CHUNK == 0 and ls <= CHUNK and lp % CHUNK == 0
    ns = bs + 1
    hw = MLA_HEADS * QK_NOPE

    hp = x_prompt.reshape(bp * lp, d)
    hs = jnp.concatenate([x_sample.reshape(bs * ls, d), meta_tokens.astype(F32)], axis=0)
    meta_rows = slice(bs * ls, bs * ls + n_meta)

    pos_p = n_meta + jnp.arange(lp)
    pos_s = jnp.concatenate([jnp.tile(n_meta + past + jnp.arange(ls), bs), jnp.arange(n_meta)])
    tab_p = _rope_tables(pos_p)
    tab_s = _rope_tables(pos_s)
    zero_d = jnp.zeros((1, d), F32)
    wg_all, wu_all, wd_all = w_gate.astype(BF16), w_up.astype(BF16), w_down.astype(BF16)

    outs = {k: [] for k in ("p_ckv", "p_kr", "p_s", "p_conv", "p_cf", "s_ckv", "s_kr", "s_s", "s_conv", "s_cf")}
    for layer in range(depth):
        ng = norm_gains[layer].astype(F32)
        g0, g1, g2, g3 = (ng[i:i + 1] for i in range(4))
        if layer % 2 == 0:
            e = layer // 2
            offs = np.cumsum((Q_LORA, KV_LORA, QK_ROPE, DN_QKV, DN_HEADS * DN_DV, DN_HEADS, DN_HEADS))
            wi = w_in[e]
            w_qd, w_kvd, w_kr = wi[:, :offs[0]], wi[:, offs[0]:offs[1]], wi[:, offs[1]:offs[2]]
            w_qkv, w_z = wi[:, offs[2]:offs[3]], wi[:, offs[3]:offs[4]]
            w_a, w_b = wi[:, offs[4]:offs[5]], wi[:, offs[5]:offs[6]]
            w_proj = jnp.concatenate(
                [w_qkv, w_z, w_qd, w_kvd, w_kr, _rot_cols(w_kr), w_a, w_b,
                 jnp.zeros((d, SMALL_W - 2 * QK_ROPE - 2 * DN_HEADS), F32)], axis=1).astype(BF16)
            zero_proj = jnp.zeros((1, PROJ_W), F32)
            wq3 = w_uq[e]
            wq_n = wq3[:, :, :QK_NOPE].reshape(Q_LORA, hw)
            wq_r = wq3[:, :, QK_NOPE:]
            pad_r = lambda w: jnp.pad(w, ((0, 0), (0, 0), (0, LANE - QK_ROPE))).reshape(Q_LORA, hw)
            wq = jnp.concatenate([wq_n, pad_r(wq_r), pad_r(_rot_cols(wq_r))], axis=1).astype(BF16)
            wk = w_uk[e].reshape(KV_LORA, hw).astype(BF16)
            wv = w_uv[e].reshape(KV_LORA, hw).astype(BF16)
            wkv = jnp.concatenate([wk, wv], axis=1)
            gq, gkv = _row(mla_gq[e]), _row(mla_gkv[e])
            cw = jnp.pad(dn_conv_w[e].astype(F32), ((0, 8 - DN_CONV), (0, 0)))
            ab = jnp.concatenate([_row(dn_a_log[e], LANE), _row(dn_dt_bias[e], LANE),
                                  jnp.zeros((6, LANE), F32)], axis=0)
            nw = _row(dn_norm_w[e])
            wo = w_out[e].astype(BF16)
            wo_mla, wo_dn = wo[:hw], wo[hw:]

            proj_s = norm_matmul(hs, g0, w_proj, zero_proj, 1024, 768)
            qn_s, qr_s, ckv_s, kn_s, v_s, kro_s, krp_s = mla_prep(proj_s, gq, gkv, wq, wk, wv, *tab_s, 512, False)
            lh = n_meta + past
            lh_pad = -(-lh // LANE) * LANE
            hist_ckv = jnp.pad(cache_mla_ckv[e].astype(F32), ((0, 1), (0, lh_pad - lh), (0, 0)))
            hkn, hv = kv_up(hist_ckv.reshape(ns * lh_pad, KV_LORA), wkv, lh_pad)
            hkrp = jnp.pad(cache_mla_krope[e].astype(BF16),
                           ((0, 1), (0, lh_pad - lh), (0, LANE - QK_ROPE)))
            hlen_s = jnp.concatenate([jnp.full((bs,), lh, jnp.int32), jnp.zeros((1,), jnp.int32)])
            ymla_s = attention(qn_s, qr_s, kn_s, krp_s, v_s, hkn.reshape(ns, lh_pad, hw), hkrp,
                               hv.reshape(ns, lh_pad, hw), hlen_s, ns, ls, ls)
            conv_hist_s = jnp.pad(state_dn_conv[e].astype(F32), ((0, 1), (8 - (DN_CONV - 1), 0), (0, 0)))
            s0_s = jnp.pad(state_dn_s[e].astype(F32), ((0, 1), (0, 0), (0, 0), (0, 0)))
            ydn_s, sfin_s = deltanet(proj_s, cw, conv_hist_s, s0_s, ab, nw, ns, ls, CHUNK, 1)
            hs = matmul_resnorm([ymla_s, ydn_s], [wo_mla, wo_dn], zero_d, g1, hs, 512)

            proj_p = norm_matmul(hp, g0, w_proj, zero_proj, 1024, 768)
            qn_p, qr_p, ckv_p, kn_p, vt_p, kro_p, krp_p = mla_prep(proj_p, gq, gkv, wq, wk, wv.T, *tab_p,
                                                                   512, True)
            hlen_p = jnp.full((bp,), n_meta, jnp.int32)
            ymla_p = attention_t(qn_p, qr_p, kn_p, krp_p, vt_p, kn_s[meta_rows], krp_s[meta_rows],
                                 v_s[meta_rows].T, hlen_p, bp, lp)
            conv_hist_p = jnp.pad(proj_s[meta_rows, COL_QKV:COL_QKV + DN_QKV][-(DN_CONV - 1):],
                                  ((8 - (DN_CONV - 1), 0), (0, 0)))[None]
            ydn_p, sfin_p = deltanet(proj_p, cw, conv_hist_p, sfin_s[bs:], ab, nw, bp, lp, CHUNK, 2)
            hp = matmul_resnorm([ymla_p, ydn_p], [wo_mla, wo_dn], zero_d, g1, hp, 512)

            bc = lambda a: jnp.broadcast_to(a[None], (bp,) + a.shape)
            outs["p_ckv"].append(jnp.concatenate([bc(ckv_s[meta_rows]), ckv_p.reshape(bp, lp, KV_LORA)], axis=1))
            outs["p_kr"].append(jnp.concatenate([bc(kro_s[meta_rows]), kro_p.reshape(bp, lp, QK_ROPE)], axis=1))
            outs["p_s"].append(sfin_p)
            outs["p_conv"].append(proj_p.reshape(bp, lp, PROJ_W)[:, lp - (DN_CONV - 1):, COL_QKV:COL_QKV + DN_QKV])
            outs["s_ckv"].append(ckv_s[:bs * ls].reshape(bs, ls, KV_LORA))
            outs["s_kr"].append(kro_s[:bs * ls].reshape(bs, ls, QK_ROPE))
            outs["s_s"].append(sfin_s[:bs])
            xqkv_s = proj_s[:bs * ls, COL_QKV:COL_QKV + DN_QKV].reshape(bs, ls, DN_QKV)
            outs["s_conv"].append(jnp.concatenate([state_dn_conv[e].astype(F32), xqkv_s], axis=1)[:, -(DN_CONV - 1):])
        else:
            o = layer // 2
            w1 = cf_w_pw1[o].astype(BF16)
            b1 = _row(cf_b_pw1[o])
            wdw = jnp.pad(cf_w_dw[o].astype(F32), ((0, CF_HALO - CF_KERNEL), (0, 0)))
            bdw, lg, lb = _row(cf_b_dw[o]), _row(cf_ln_g[o]), _row(cf_ln_b[o])
            w2 = cf_w_pw2[o].astype(BF16)
            b2 = _row(cf_b_pw2[o])
            keep = CF_KERNEL - 1

            u_s = norm_glu(hs, g0, w1, b1, 512, 512)
            hist_s = jnp.pad(state_cf_conv[o].astype(F32), ((0, 1), (CF_HALO - keep, 0), (0, 0)))
            c_s = cf_conv(u_s, hist_s, wdw, bdw, lg, lb, ns, ls, 128)
            hs = matmul_resnorm([c_s], [w2], b2, g1, hs, 512)

            u_p = norm_glu(hp, g0, w1, b1, 1024, 512)
            hist_p = jnp.pad(u_s[meta_rows], ((CF_HALO - n_meta, 0), (0, 0)))[None]
            c_p = cf_conv(u_p, hist_p, wdw, bdw, lg, lb, bp, lp, 128)
            hp = matmul_resnorm([c_p], [w2], b2, g1, hp, 512)

            u_p3 = u_p.reshape(bp, lp, d)
            meta_u = jnp.broadcast_to(u_s[meta_rows][None], (bp, n_meta, d))
            outs["p_cf"].append(jnp.concatenate([jnp.zeros((bp, keep, d), F32), meta_u, u_p3[:, max(lp - keep, 0):]],
                                                axis=1)[:, -keep:])
            outs["s_cf"].append(jnp.concatenate([state_cf_conv[o].astype(F32),
                                                 u_s[:bs * ls].reshape(bs, ls, d)], axis=1)[:, -keep:])
        hs = ffn(hs, g2, wg_all, wu_all, wd_all, g3, layer, 512, 512)
        hp = ffn(hp, g2, wg_all, wu_all, wd_all, g3, layer, 512, 512)

    y_prompt = hp.reshape(bp, lp, d)
    y_sample = hs[:bs * ls].reshape(bs, ls, d)
    st = lambda k: jnp.stack(outs[k])
    return (y_prompt, y_sample, st("p_ckv"), st("p_kr"), st("p_s"), st("p_conv"), st("p_cf"),
            st("s_ckv"), st("s_kr"), st("s_s"), st("s_conv"), st("s_cf"))
```

```python
import functools

import numpy as np
import jax
import jax.numpy as jnp
from jax import lax
from jax.experimental import pallas as pl
from jax.experimental.pallas import tpu as pltpu

F32 = jnp.float32
BF16 = jnp.bfloat16

EPS = 1e-6
NEG_INF = -1e30
CHUNK = 64
N_META = 16
MLA_HEADS = 8
Q_LORA = 512
KV_LORA = 512
QK_NOPE = 128
QK_ROPE = 64
V_HEAD = 128
ROPE_THETA = 10000.0
DN_HEADS = 8
DN_DK = 128
DN_DV = 128
DN_CONV = 4
DN_QKV = DN_HEADS * (2 * DN_DK + DN_DV)
CF_KERNEL = 31

LANE = 128
ATT_TILE = 256
VMEM_LIMIT = 56 * 1024 * 1024

COL_QKV = 0
COL_Z = DN_QKV
COL_QD = COL_Z + DN_HEADS * DN_DV
COL_KVD = COL_QD + Q_LORA
COL_SMALL = COL_KVD + KV_LORA
SMALL_W = 256
PROJ_W = COL_SMALL + SMALL_W


def _cparams(sem):
    return pltpu.CompilerParams(dimension_semantics=sem, vmem_limit_bytes=VMEM_LIMIT)


def _rms(x, g):
    return x * lax.rsqrt(jnp.mean(x * x, axis=-1, keepdims=True) + EPS) * g


def _sigmoid(x):
    return 1.0 / (1.0 + jnp.exp(-x))


def _dot(a, b):
    return jnp.dot(a, b, preferred_element_type=F32)


def _dot_nt(a, b):
    return lax.dot_general(a, b, (((1,), (1,)), ((), ())), preferred_element_type=F32)


def _dot_tn(a, b):
    return lax.dot_general(a, b, (((0,), (0,)), ((), ())), preferred_element_type=F32)


def _row_tile(m, pref):
    t = min(pref, m)
    while m % t:
        t //= 2
    return t


def _norm_mm_kernel(x_ref, g_ref, w_ref, b_ref, o_ref, xn_ref):
    @pl.when(pl.program_id(1) == 0)
    def _():
        xn_ref[...] = _rms(x_ref[...], g_ref[...]).astype(BF16)

    o_ref[...] = (_dot(xn_ref[...], w_ref[...]) + b_ref[...]).astype(o_ref.dtype)


CF_HALO = 32


def _glu_conv_kernel(x_ref, g_ref, wa_ref, wg_ref, ba_ref, bg_ref, hist_ref, cw_ref, cb_ref,
                     c_ref, tail_ref, xn_ref, ext_ref, sh_ref, carry_ref, *, tm, chunk, tiles_per_seq):
    i = pl.program_id(0)
    j = pl.program_id(1)
    first = CF_HALO - (CF_KERNEL - 1)
    sub = 8
    span = chunk + CF_HALO - sub

    @pl.when(j == 0)
    def _():
        xn_ref[...] = _rms(x_ref[...], g_ref[...]).astype(BF16)

    seq_start = (i % tiles_per_seq) == 0

    @pl.when(seq_start)
    def _():
        ext_ref[0:CF_HALO, :] = hist_ref[...]

    @pl.when(jnp.logical_not(seq_start))
    def _():
        ext_ref[0:CF_HALO, :] = carry_ref[j]

    def glu_chunk(c):
        lo = c * chunk
        xc = xn_ref[lo:lo + chunk, :]
        a = _dot(xc, wa_ref[...]) + ba_ref[...]
        gt = _dot(xc, wg_ref[...]) + bg_ref[...]
        ext_ref[CF_HALO + lo:CF_HALO + lo + chunk, :] = a * _sigmoid(gt)

    nchunk = tm // chunk
    glu_chunk(0)
    for c in range(nchunk):
        lo = c * chunk
        if c + 1 < nchunk:
            glu_chunk(c + 1)
        for sft in range(1, sub):
            sh_ref[sft - 1, 0:span, :] = ext_ref[lo + sft:lo + sft + span, :]
        rg = min(chunk, 4 * sub)
        for r0 in range(0, chunk, rg):
            acc = None
            for t in range(CF_KERNEL):
                sft = (first + t) % sub
                base = first + t - sft + r0
                if sft == 0:
                    src = ext_ref[lo + base:lo + base + rg, :]
                else:
                    src = sh_ref[sft - 1, base:base + rg, :]
                term = src.reshape(rg // sub, sub, src.shape[1]) * cw_ref[t]
                acc = term if acc is None else acc + term
            c_ref[lo + r0:lo + r0 + rg, :] = acc.reshape(rg, acc.shape[2]) + cb_ref[...]
    tail = ext_ref[tm:tm + CF_HALO, :]
    carry_ref[j] = tail
    tail_ref[...] = tail


def norm_matmul(x, g, w, b, tm, tn, out_dtype=F32):
    m, k = x.shape
    n = w.shape[1]
    tm = _row_tile(m, tm)
    return pl.pallas_call(
        _norm_mm_kernel,
        grid=(m // tm, n // tn),
        in_specs=[pl.BlockSpec((tm, k), lambda i, j: (i, 0)),
                  pl.BlockSpec((1, k), lambda i, j: (0, 0)),
                  pl.BlockSpec((k, tn), lambda i, j: (0, j)),
                  pl.BlockSpec((1, tn), lambda i, j: (0, j))],
        out_specs=pl.BlockSpec((tm, tn), lambda i, j: (i, j)),
        out_shape=jax.ShapeDtypeStruct((m, n), out_dtype),
        scratch_shapes=[pltpu.VMEM((tm, k), BF16)],
        compiler_params=_cparams(("parallel", "arbitrary")),
        name="norm_matmul",
    )(x, g, w, b)


def glu_conv(x, g, w, b, hist, cw, cb, nseq, seq_len, tm, tn, chunk):
    m, k = x.shape
    n = w.shape[1] // 2
    tm = _row_tile(seq_len, tm)
    chunk = min(chunk, tm)
    assert tm % chunk == 0 and chunk % 8 == 0
    tiles_per_seq = seq_len // tm
    nb = n // tn
    nhs = hist.shape[0]
    hidx = ((lambda i, j: (i // tiles_per_seq, 0, j)) if nhs > 1 else (lambda i, j: (0, 0, j)))
    c, tails = pl.pallas_call(
        functools.partial(_glu_conv_kernel, tm=tm, chunk=chunk, tiles_per_seq=tiles_per_seq),
        grid=(m // tm, nb),
        in_specs=[pl.BlockSpec((tm, k), lambda i, j: (i, 0)),
                  pl.BlockSpec((1, k), lambda i, j: (0, 0)),
                  pl.BlockSpec((k, tn), lambda i, j: (0, j)),
                  pl.BlockSpec((k, tn), lambda i, j: (0, j + nb)),
                  pl.BlockSpec((1, tn), lambda i, j: (0, j)),
                  pl.BlockSpec((1, tn), lambda i, j: (0, j + nb)),
                  pl.BlockSpec((None, CF_HALO, tn), hidx),
                  pl.BlockSpec((cw.shape[0], 8, tn), lambda i, j: (0, 0, j)),
                  pl.BlockSpec((1, tn), lambda i, j: (0, j))],
        out_specs=[pl.BlockSpec((tm, tn), lambda i, j: (i, j)),
                   pl.BlockSpec((None, CF_HALO, tn), lambda i, j: (i, 0, j))],
        out_shape=[jax.ShapeDtypeStruct((m, n), F32),
                   jax.ShapeDtypeStruct((m // tm, CF_HALO, n), F32)],
        scratch_shapes=[pltpu.VMEM((tm, k), BF16),
                        pltpu.VMEM((tm + CF_HALO, tn), F32),
                        pltpu.VMEM((7, chunk + CF_HALO - 8, tn), F32),
                        pltpu.VMEM((nb, CF_HALO, tn), F32)],
        compiler_params=_cparams(("arbitrary", "arbitrary")),
        name="glu_conv",
    )(x, g, w, w, b, b, hist, cw, cb)
    return c, tails.reshape(nseq, tiles_per_seq, CF_HALO, n)[:, -1]


def _mm_resnorm_kernel(*refs, n_in):
    xs = refs[:n_in]
    ws = refs[n_in:2 * n_in]
    b_ref, g_ref, res_ref, o_ref = refs[2 * n_in:]
    y = b_ref[...]
    for x_ref, w_ref in zip(xs, ws):
        y = y + _dot(x_ref[...], w_ref[...])
    o_ref[...] = res_ref[...] + _rms(y, g_ref[...])


def _ln_mm_resnorm_kernel(c_ref, lg_ref, lb_ref, w_ref, b_ref, g_ref, res_ref, o_ref, *, tm, chunk):
    for c in range(tm // chunk):
        rows = slice(c * chunk, (c + 1) * chunk)
        x = c_ref[rows, :]
        xc = x - jnp.mean(x, axis=-1, keepdims=True)
        y = xc * lax.rsqrt(jnp.mean(xc * xc, axis=-1, keepdims=True) + EPS) * lg_ref[...] + lb_ref[...]
        a = (y * _sigmoid(y)).astype(BF16)
        z = _dot(a, w_ref[...]) + b_ref[...]
        o_ref[rows, :] = res_ref[rows, :] + _rms(z, g_ref[...])


def ln_matmul_resnorm(c, lg, lb, w, b, g, res, tm, chunk):
    m, n = res.shape
    k = c.shape[1]
    tm = _row_tile(m, tm)
    chunk = min(chunk, tm)
    assert tm % chunk == 0
    vec = lambda width: pl.BlockSpec((1, width), lambda i: (0, 0))
    return pl.pallas_call(
        functools.partial(_ln_mm_resnorm_kernel, tm=tm, chunk=chunk),
        grid=(m // tm,),
        in_specs=[pl.BlockSpec((tm, k), lambda i: (i, 0)), vec(k), vec(k),
                  pl.BlockSpec(w.shape, lambda i: (0, 0)), vec(n), vec(n),
                  pl.BlockSpec((tm, n), lambda i: (i, 0))],
        out_specs=pl.BlockSpec((tm, n), lambda i: (i, 0)),
        out_shape=jax.ShapeDtypeStruct((m, n), F32),
        compiler_params=_cparams(("parallel",)),
        name="ln_matmul_resnorm",
    )(c, lg, lb, w, b, g, res)


def matmul_resnorm(xs, ws, b, g, res, tm):
    m, n = res.shape
    tm = _row_tile(m, tm)
    n_in = len(xs)
    in_specs = ([pl.BlockSpec((tm, x.shape[1]), lambda i: (i, 0)) for x in xs]
                + [pl.BlockSpec(w.shape, lambda i: (0, 0)) for w in ws]
                + [pl.BlockSpec((1, n), lambda i: (0, 0)),
                   pl.BlockSpec((1, n), lambda i: (0, 0)),
                   pl.BlockSpec((tm, n), lambda i: (i, 0))])
    return pl.pallas_call(
        functools.partial(_mm_resnorm_kernel, n_in=n_in),
        grid=(m // tm,),
        in_specs=in_specs,
        out_specs=pl.BlockSpec((tm, n), lambda i: (i, 0)),
        out_shape=jax.ShapeDtypeStruct((m, n), F32),
        compiler_params=_cparams(("parallel",)),
        name="matmul_resnorm",
    )(*xs, *ws, b, g, res)


def _ffn_kernel(h_ref, g2_ref, wg_ref, wu_ref, wd_ref, g3_ref, o_ref, xn_ref):
    j = pl.program_id(1)

    @pl.when(j == 0)
    def _():
        xn_ref[...] = _rms(h_ref[...], g2_ref[...]).astype(BF16)
        o_ref[...] = jnp.zeros_like(o_ref)

    xn = xn_ref[...]
    gate = _dot(xn, wg_ref[...])
    up = _dot(xn, wu_ref[...])
    a = (gate * _sigmoid(gate) * up).astype(BF16)
    o_ref[...] += _dot(a, wd_ref[...])

    @pl.when(j == pl.num_programs(1) - 1)
    def _():
        o_ref[...] = h_ref[...] + _rms(o_ref[...], g3_ref[...])


def ffn(h, g2, wg, wu, wd, g3, layer, tm, tf):
    m, d = h.shape
    f = wg.shape[2]
    tm = _row_tile(m, tm)
    return pl.pallas_call(
        _ffn_kernel,
        grid=(m // tm, f // tf),
        in_specs=[pl.BlockSpec((tm, d), lambda i, j: (i, 0), pipeline_mode=pl.Buffered(1)),
                  pl.BlockSpec((1, d), lambda i, j: (0, 0)),
                  pl.BlockSpec((None, d, tf), lambda i, j: (layer, 0, j)),
                  pl.BlockSpec((None, d, tf), lambda i, j: (layer, 0, j)),
                  pl.BlockSpec((None, tf, d), lambda i, j: (layer, j, 0)),
                  pl.BlockSpec((1, d), lambda i, j: (0, 0))],
        out_specs=pl.BlockSpec((tm, d), lambda i, j: (i, 0)),
        out_shape=jax.ShapeDtypeStruct((m, d), F32),
        scratch_shapes=[pltpu.VMEM((tm, d), BF16)],
        compiler_params=_cparams(("parallel", "arbitrary")),
        name="ffn",
    )(h, g2, wg, wu, wd, g3)


def _mla_prep_kernel(qd_ref, kvd_ref, sm_ref, gq_ref, gkv_ref, wq_ref, wk_ref, wv_ref, cq_ref, sq_ref, ck_ref,
                     qn_ref, qr_ref, ckv_ref, kn_ref, v_ref, kro_ref, krp_ref, *, v_transposed):
    hw = MLA_HEADS * QK_NOPE
    scale = (QK_NOPE + QK_ROPE) ** -0.5
    cq = _rms(qd_ref[...], gq_ref[...]).astype(BF16)
    q = _dot(cq, wq_ref[...])
    qn_ref[...] = (q[:, :hw] * scale).astype(BF16)
    cos8 = jnp.tile(cq_ref[...], (1, MLA_HEADS))
    sin8 = jnp.tile(sq_ref[...], (1, MLA_HEADS))
    qr_ref[...] = ((q[:, hw:2 * hw] * cos8 + q[:, 2 * hw:] * sin8) * scale).astype(BF16)
    ckv = _rms(kvd_ref[...], gkv_ref[...])
    ckv_ref[...] = ckv
    ckv_b = ckv.astype(BF16)
    kn_ref[...] = _dot(ckv_b, wk_ref[...]).astype(BF16)
    if v_transposed:
        vt = _dot_nt(wv_ref[...], ckv_b).astype(BF16)
        for t in range(v_ref.shape[0]):
            v_ref[t] = vt[:, t * ATT_TILE:(t + 1) * ATT_TILE]
    else:
        v_ref[...] = _dot(ckv_b, wv_ref[...]).astype(BF16)
    y = sm_ref[:, :LANE] * ck_ref[...]
    kro = y + pltpu.roll(y, QK_ROPE, 1)
    kro_ref[...] = kro[:, :QK_ROPE]
    lane = lax.broadcasted_iota(jnp.int32, kro.shape, 1)
    krp_ref[...] = jnp.where(lane < QK_ROPE, kro, 0.0).astype(BF16)


def mla_prep(proj, gq, gkv, wq, wk, wv, cq_tab, sq_tab, ck_tab, tm, v_transposed):
    m = proj.shape[0]
    tm = _row_tile(min(m, cq_tab.shape[0]), tm)
    nt = cq_tab.shape[0] // tm
    hw = MLA_HEADS * QK_NOPE
    tab = lambda: pl.BlockSpec((tm, LANE), lambda i: (i % nt, 0))
    full = lambda a: pl.BlockSpec(a.shape, lambda i: (0, 0))
    if v_transposed:
        assert tm % ATT_TILE == 0
        v_spec = pl.BlockSpec((tm // ATT_TILE, hw, ATT_TILE), lambda i: (i, 0, 0))
        v_shape = jax.ShapeDtypeStruct((m // ATT_TILE, hw, ATT_TILE), BF16)
    else:
        v_spec = pl.BlockSpec((tm, hw), lambda i: (i, 0))
        v_shape = jax.ShapeDtypeStruct((m, hw), BF16)
    return pl.pallas_call(
        functools.partial(_mla_prep_kernel, v_transposed=v_transposed),
        grid=(m // tm,),
        in_specs=[pl.BlockSpec((tm, Q_LORA), lambda i: (i, COL_QD // Q_LORA)),
                  pl.BlockSpec((tm, KV_LORA), lambda i: (i, COL_KVD // KV_LORA)),
                  pl.BlockSpec((tm, SMALL_W), lambda i: (i, COL_SMALL // SMALL_W)),
                  full(gq), full(gkv), full(wq), full(wk), full(wv), tab(), tab(), tab()],
        out_specs=[pl.BlockSpec((tm, hw), lambda i: (i, 0)),
                   pl.BlockSpec((tm, hw), lambda i: (i, 0)),
                   pl.BlockSpec((tm, KV_LORA), lambda i: (i, 0)),
                   pl.BlockSpec((tm, hw), lambda i: (i, 0)),
                   v_spec,
                   pl.BlockSpec((tm, QK_ROPE), lambda i: (i, 0)),
                   pl.BlockSpec((tm, LANE), lambda i: (i, 0))],
        out_shape=[jax.ShapeDtypeStruct((m, hw), BF16),
                   jax.ShapeDtypeStruct((m, hw), BF16),
                   jax.ShapeDtypeStruct((m, KV_LORA), F32),
                   jax.ShapeDtypeStruct((m, hw), BF16),
                   v_shape,
                   jax.ShapeDtypeStruct((m, QK_ROPE), F32),
                   jax.ShapeDtypeStruct((m, LANE), BF16)],
        compiler_params=_cparams(("parallel",)),
        name="mla_prep",
    )(proj, proj, proj, gq, gkv, wq, wk, wv, cq_tab, sq_tab, ck_tab)


def _kv_up_kernel(ckv_ref, wkv_ref, kn_ref, v_ref):
    hw = MLA_HEADS * QK_NOPE
    kv = _dot(ckv_ref[...].astype(BF16), wkv_ref[...])
    kn_ref[...] = kv[:, :hw].astype(BF16)
    v_ref[...] = kv[:, hw:].astype(BF16)


def kv_up(ckv, wkv, tm):
    m = ckv.shape[0]
    tm = _row_tile(m, tm)
    hw = MLA_HEADS * QK_NOPE
    return pl.pallas_call(
        _kv_up_kernel,
        grid=(m // tm,),
        in_specs=[pl.BlockSpec((tm, KV_LORA), lambda i: (i, 0)),
                  pl.BlockSpec(wkv.shape, lambda i: (0, 0))],
        out_specs=[pl.BlockSpec((tm, hw), lambda i: (i, 0)),
                   pl.BlockSpec((tm, hw), lambda i: (i, 0))],
        out_shape=[jax.ShapeDtypeStruct((m, hw), BF16), jax.ShapeDtypeStruct((m, hw), BF16)],
        compiler_params=_cparams(("parallel",)),
        name="kv_up",
    )(ckv, wkv)


def _attn_kernel(hlen_ref, qn_ref, qr_ref, kn_ref, kr_ref, v_ref, hkn_ref, hkr_ref, hv_ref, o_ref,
                 m_ref, l_ref, acc_ref, a_ref, qc_ref, s_ref, p_ref, hs_ref, hp_ref, *, tile, lh):
    s_idx = pl.program_id(0)
    qt = pl.program_id(1)
    hlen = hlen_ref[s_idx]
    nh = MLA_HEADS
    heads = [slice(h * LANE, (h + 1) * LANE) for h in range(nh)]

    for h, hs in enumerate(heads):
        qc_ref[h] = jnp.concatenate([qn_ref[:, hs], qr_ref[:, hs]], axis=1)

    def process(get_k, get_v, s_scr, p_scr, valid, first):
        for h in range(nh):
            s_scr[h] = _dot_nt(qc_ref[h], get_k(h))
        for h, hs in enumerate(heads):
            s = s_scr[h]
            if valid is not None:
                s = jnp.where(valid, s, NEG_INF)
            m_cur = jnp.max(s, axis=1, keepdims=True)
            if first:
                m_new = jnp.broadcast_to(m_cur, (tile, LANE))
            else:
                m_prev = m_ref[:, hs]
                m_new = jnp.maximum(m_prev, m_cur)
            p = jnp.exp(s - m_new[:, :1])
            if valid is not None:
                p = jnp.where(valid, p, 0.0)
            p_scr[h] = p.astype(BF16)
            l_cur = jnp.sum(p, axis=1, keepdims=True)
            if first:
                l_ref[:, hs] = jnp.broadcast_to(l_cur, (tile, LANE))
            else:
                alpha = jnp.exp(m_prev - m_new)
                a_ref[:, hs] = alpha
                l_ref[:, hs] = alpha * l_ref[:, hs] + l_cur
            m_ref[:, hs] = m_new
        for h, hs in enumerate(heads):
            pv = _dot(p_scr[h], get_v(h))
            if first:
                acc_ref[:, hs] = pv
            else:
                acc_ref[:, hs] = a_ref[:, hs] * acc_ref[:, hs] + pv

    hvalid = lax.broadcasted_iota(jnp.int32, (tile, lh), 1) < hlen
    process(lambda h: jnp.concatenate([hkn_ref[:, heads[h]], hkr_ref[...]], axis=1),
            lambda h: hv_ref[:, heads[h]], hs_ref, hp_ref, hvalid, True)

    def own_tile(kt, valid):
        rows = pl.ds(pl.multiple_of(kt * tile, tile), tile)
        process(lambda h: jnp.concatenate([kn_ref[rows, heads[h]], kr_ref[rows, :]], axis=1),
                lambda h: v_ref[rows, heads[h]], s_ref, p_ref, valid, False)

    def body(kt, carry):
        own_tile(kt, None)
        return carry

    lax.fori_loop(0, qt, body, 0)
    if tile > CHUNK:
        row = lax.broadcasted_iota(jnp.int32, (tile, tile), 0) // CHUNK
        col = lax.broadcasted_iota(jnp.int32, (tile, tile), 1) // CHUNK
        dvalid = col <= row
    else:
        dvalid = None
    own_tile(qt, dvalid)

    for hs in heads:
        o_ref[:, hs] = (acc_ref[:, hs] / l_ref[:, hs]).astype(o_ref.dtype)


def attention(qn, qr, kn, krp, v, hkn, hkrp, hv, hlen, nseq, seq_len, tile):
    hw = MLA_HEADS * LANE
    tile = min(tile, seq_len)
    assert seq_len % tile == 0 and (tile % CHUNK == 0 or seq_len == tile <= CHUNK)
    nqt = seq_len // tile
    nhs, lh = hkn.shape[0], hkn.shape[1]
    assert nhs in (1, nseq)
    hidx = (lambda s, q, hl: (s, 0, 0)) if nhs > 1 else (lambda s, q, hl: (0, 0, 0))
    grid_spec = pltpu.PrefetchScalarGridSpec(
        num_scalar_prefetch=1,
        grid=(nseq, nqt),
        in_specs=[pl.BlockSpec((tile, hw), lambda s, q, hl: (s * nqt + q, 0)),
                  pl.BlockSpec((tile, hw), lambda s, q, hl: (s * nqt + q, 0)),
                  pl.BlockSpec((seq_len, hw), lambda s, q, hl: (s, 0)),
                  pl.BlockSpec((seq_len, LANE), lambda s, q, hl: (s, 0)),
                  pl.BlockSpec((seq_len, hw), lambda s, q, hl: (s, 0)),
                  pl.BlockSpec((None, lh, hw), hidx),
                  pl.BlockSpec((None, lh, LANE), hidx),
                  pl.BlockSpec((None, lh, hw), hidx)],
        out_specs=pl.BlockSpec((tile, hw), lambda s, q, hl: (s * nqt + q, 0)),
        scratch_shapes=[pltpu.VMEM((tile, hw), F32), pltpu.VMEM((tile, hw), F32),
                        pltpu.VMEM((tile, hw), F32), pltpu.VMEM((tile, hw), F32),
                        pltpu.VMEM((MLA_HEADS, tile, 2 * LANE), BF16),
                        pltpu.VMEM((MLA_HEADS, tile, tile), F32),
                        pltpu.VMEM((MLA_HEADS, tile, tile), BF16),
                        pltpu.VMEM((MLA_HEADS, tile, lh), F32),
                        pltpu.VMEM((MLA_HEADS, tile, lh), BF16)],
    )
    return pl.pallas_call(
        functools.partial(_attn_kernel, tile=tile, lh=lh),
        grid_spec=grid_spec,
        out_shape=jax.ShapeDtypeStruct((nseq * seq_len, hw), BF16),
        compiler_params=_cparams(("parallel", "arbitrary")),
        name="attention",
    )(hlen, qn, qr, kn, krp, v, hkn, hkrp, hv)


def _attn_t_kernel(hlen_ref, qn_ref, qr_ref, kn_ref, kr_ref, vt_ref, hkn_ref, hkr_ref, hvt_ref, o_ref,
                   m_ref, l_ref, a_ref, acc_ref, qc_ref, s_ref, p_ref, hs_ref, hp_ref, *, tile, lh):
    s_idx = pl.program_id(0)
    qt = pl.program_id(1)
    hlen = hlen_ref[s_idx]
    nh = MLA_HEADS
    heads = [slice(h * LANE, (h + 1) * LANE) for h in range(nh)]

    for h, hs in enumerate(heads):
        qc_ref[h] = jnp.concatenate([qn_ref[:, hs], qr_ref[:, hs]], axis=1)

    def process(get_k, get_vt, s_scr, p_scr, valid, first):
        for h in range(nh):
            s_scr[h] = _dot_nt(get_k(h), qc_ref[h])
        for h in range(nh):
            s = s_scr[h]
            if valid is not None:
                s = jnp.where(valid, s, NEG_INF)
            m_cur = jnp.max(s, axis=0, keepdims=True)
            if first:
                m_new = m_cur
            else:
                m_prev = m_ref[h]
                m_new = jnp.maximum(m_prev, m_cur)
            p = jnp.exp(s - m_new)
            if valid is not None:
                p = jnp.where(valid, p, 0.0)
            p_scr[h] = p.astype(BF16)
            l_cur = jnp.sum(p, axis=0, keepdims=True)
            if first:
                l_ref[h] = l_cur
            else:
                alpha = jnp.exp(m_prev - m_new)
                a_ref[h] = alpha
                l_ref[h] = alpha * l_ref[h] + l_cur
            m_ref[h] = m_new
        for h in range(nh):
            pv = _dot(get_vt(h), p_scr[h])
            if first:
                acc_ref[h] = pv
            else:
                acc_ref[h] = a_ref[h] * acc_ref[h] + pv

    hvalid = lax.broadcasted_iota(jnp.int32, (lh, tile), 0) < hlen
    process(lambda h: jnp.concatenate([hkn_ref[:, heads[h]], hkr_ref[...]], axis=1),
            lambda h: hvt_ref[heads[h], :], hs_ref, hp_ref, hvalid, True)

    def own_tile(kt, valid):
        rows = pl.ds(pl.multiple_of(kt * tile, tile), tile)
        process(lambda h: jnp.concatenate([kn_ref[rows, heads[h]], kr_ref[rows, :]], axis=1),
                lambda h: vt_ref[kt, heads[h], :], s_ref, p_ref, valid, False)

    def body(kt, carry):
        own_tile(kt, None)
        return carry

    lax.fori_loop(0, qt, body, 0)
    key_chunk = lax.broadcasted_iota(jnp.int32, (tile, tile), 0) // CHUNK
    qry_chunk = lax.broadcasted_iota(jnp.int32, (tile, tile), 1) // CHUNK
    own_tile(qt, key_chunk <= qry_chunk)

    for h, hs in enumerate(heads):
        o_ref[:, hs] = (acc_ref[h] / l_ref[h]).T.astype(o_ref.dtype)


def attention_t(qn, qr, kn, krp, vt, hkn, hkrp, hvt, hlen, nseq, seq_len):
    hw = MLA_HEADS * LANE
    tile = ATT_TILE
    assert seq_len % tile == 0 and tile % CHUNK == 0
    nqt = seq_len // tile
    lh = hkn.shape[0]
    nh = MLA_HEADS
    grid_spec = pltpu.PrefetchScalarGridSpec(
        num_scalar_prefetch=1,
        grid=(nseq, nqt),
        in_specs=[pl.BlockSpec((tile, hw), lambda s, q, hl: (s * nqt + q, 0)),
                  pl.BlockSpec((tile, hw), lambda s, q, hl: (s * nqt + q, 0)),
                  pl.BlockSpec((seq_len, hw), lambda s, q, hl: (s, 0)),
                  pl.BlockSpec((seq_len, LANE), lambda s, q, hl: (s, 0)),
                  pl.BlockSpec((nqt, hw, tile), lambda s, q, hl: (s, 0, 0)),
                  pl.BlockSpec((lh, hw), lambda s, q, hl: (0, 0)),
                  pl.BlockSpec((lh, LANE), lambda s, q, hl: (0, 0)),
                  pl.BlockSpec((hw, lh), lambda s, q, hl: (0, 0))],
        out_specs=pl.BlockSpec((tile, hw), lambda s, q, hl: (s * nqt + q, 0)),
        scratch_shapes=[pltpu.VMEM((nh, 1, tile), F32), pltpu.VMEM((nh, 1, tile), F32),
                        pltpu.VMEM((nh, 1, tile), F32),
                        pltpu.VMEM((nh, V_HEAD, tile), F32),
                        pltpu.VMEM((nh, tile, 2 * LANE), BF16),
                        pltpu.VMEM((nh, tile, tile), F32),
                        pltpu.VMEM((nh, tile, tile), BF16),
                        pltpu.VMEM((nh, lh, tile), F32),
                        pltpu.VMEM((nh, lh, tile), BF16)],
    )
    return pl.pallas_call(
        functools.partial(_attn_t_kernel, tile=tile, lh=lh),
        grid_spec=grid_spec,
        out_shape=jax.ShapeDtypeStruct((nseq * seq_len, hw), BF16),
        compiler_params=_cparams(("parallel", "arbitrary")),
        name="attention_t",
    )(hlen, qn, qr, kn, krp, vt, hkn, hkrp, hvt)


def _dn_prep_kernel(x_ref, sm_ref, cw_ref, hist_ref, ab_ref,
                    u_ref, w_ref, qd_ref, kd_ref, attn_ref, gl_ref,
                    ext_ref, act_ref, lhs_ref, kb_ref, dec_ref, rhs_ref, mm_ref, t_ref, xs_ref, *, c, g):
    j = pl.program_id(1)
    nh, dk, dv = DN_HEADS, DN_DK, DN_DV
    kw = nh * dk
    rows_all = c * g
    inst = [(b, h) for b in range(g) for h in range(nh)]

    @pl.when(j == 0)
    def _():
        ext_ref[0:8, :] = hist_ref[...]

    ext_ref[8:8 + rows_all, :] = x_ref[...]
    conv = cw_ref[0:1, :] * ext_ref[5:5 + rows_all, :]
    for t in range(1, DN_CONV):
        conv = conv + cw_ref[t:t + 1, :] * ext_ref[5 + t:5 + t + rows_all, :]
    ext_ref[0:8, :] = ext_ref[rows_all:rows_all + 8, :]
    act_ref[...] = conv * _sigmoid(conv)

    gates = sm_ref[:, LANE:2 * LANE]
    xa = gates + ab_ref[1:2, :]
    softplus = jnp.maximum(xa, 0.0) + jnp.log(1.0 + jnp.exp(-jnp.abs(xa)))
    g_all = -jnp.exp(ab_ref[0:1, :]) * softplus
    beta_all = _sigmoid(gates)

    r = lax.broadcasted_iota(jnp.int32, (c, c), 0)
    q = lax.broadcasted_iota(jnp.int32, (c, c), 1)
    incl = r >= q
    strict = r > q
    eye = jnp.where(r == q, 1.0, 0.0)
    tri = jnp.where(incl, 1.0, 0.0)

    for b in range(g):
        rows = slice(b * c, (b + 1) * c)
        gc = jnp.dot(tri, g_all[rows], preferred_element_type=F32, precision=lax.Precision.HIGHEST)
        if c < LANE:
            gc_sq = jnp.concatenate([gc, jnp.zeros((LANE - c, LANE), F32)], axis=0)
        else:
            gc_sq = gc
        gc_t = gc_sq.T
        egc = jnp.exp(gc)
        glast = gc[c - 1:c, :]
        edl = jnp.exp(glast - gc)
        gl_ref[rows, :] = jnp.broadcast_to(glast, (c, LANE))
        beta_b = beta_all[rows]
        for h in range(nh):
            i = b * nh + h
            hs = slice(h * dk, (h + 1) * dk)
            qh = act_ref[rows, hs]
            kh = act_ref[rows, kw + h * dk:kw + (h + 1) * dk]
            vh = act_ref[rows, 2 * kw + h * dv:2 * kw + (h + 1) * dv]
            qh = qh * (lax.rsqrt(jnp.sum(qh * qh, axis=1, keepdims=True) + EPS) * dk ** -0.5)
            kh = kh * lax.rsqrt(jnp.sum(kh * kh, axis=1, keepdims=True) + EPS)
            bcol = beta_b[:, 8 + h:9 + h]
            gcol = gc[:, h:h + 1]
            grow = gc_t[h:h + 1, :c]
            dec_ref[i] = jnp.where(incl, jnp.exp(jnp.where(incl, gcol - grow, 0.0)), 0.0)
            kb = kh * bcol
            ecol = egc[:, h:h + 1]
            lhs_ref[i] = jnp.concatenate([kb, qh], axis=0).astype(BF16)
            kb_ref[i] = kh.astype(BF16)
            rhs_ref[i] = jnp.concatenate([vh * bcol, kb * ecol], axis=1).astype(BF16)
            qd_ref[rows, hs] = (qh * ecol).astype(BF16)
            kd_ref[rows, hs] = (kh * edl[:, h:h + 1]).astype(BF16)

    for i, (b, h) in enumerate(inst):
        kq = _dot_nt(lhs_ref[i], kb_ref[i])
        dec = dec_ref[i]
        mm = jnp.where(strict, kq[:c] * dec, 0.0)
        mm_ref[i] = mm
        t_ref[i] = eye - jnp.where((r ^ q) == 1, mm, 0.0)
        attn_ref[b, h] = (kq[c:] * dec).astype(BF16)

    s = 2
    while s < c:
        sh = s.bit_length() - 1
        sel = ((r >> sh) ^ (q >> sh)) == 1
        for i in range(len(inst)):
            e = jnp.where(sel, mm_ref[i], 0.0).astype(BF16)
            xs_ref[i] = _dot(e, t_ref[i].astype(BF16)).astype(BF16)
        for i in range(len(inst)):
            t = t_ref[i]
            t_ref[i] = t - _dot(t.astype(BF16), xs_ref[i])
        s *= 2

    for i, (b, h) in enumerate(inst):
        rows = slice(b * c, (b + 1) * c)
        hs = slice(h * dk, (h + 1) * dk)
        uw = _dot(t_ref[i].astype(BF16), rhs_ref[i])
        u_ref[rows, hs] = uw[:, :dv]
        w_ref[rows, hs] = uw[:, dv:].astype(BF16)


def _dn_scan_kernel(u_ref, w_ref, qd_ref, kd_ref, attn_ref, gl_ref, z_ref, s0_ref, nw_ref,
                    y_ref, sfin_ref, s_ref, rr_ref, vn_ref, *, c, nseq, shared_s0):
    j = pl.program_id(0)
    nh, dk = DN_HEADS, DN_DK
    inst = [(s, h) for s in range(nseq) for h in range(nh)]
    heads = [slice(h * dk, (h + 1) * dk) for h in range(nh)]

    @pl.when(j == 0)
    def _():
        for s in range(nseq):
            s_ref[s] = s0_ref[0 if shared_s0 else s]

    for i, (s, h) in enumerate(inst):
        lhs = jnp.concatenate([w_ref[s, :, heads[h]], qd_ref[s, :, heads[h]]], axis=0)
        rr_ref[i] = _dot(lhs, s_ref[s, h].astype(BF16))
    for i, (s, h) in enumerate(inst):
        vn_ref[i] = (u_ref[s, :, heads[h]] - rr_ref[i, 0:c, :]).astype(BF16)
    for i, (s, h) in enumerate(inst):
        o = rr_ref[i, c:2 * c, :] + _dot(attn_ref[s, h], vn_ref[i])
        zh = z_ref[s, :, heads[h]]
        y_ref[s, :, heads[h]] = (_rms(o, nw_ref[...]) * (zh * _sigmoid(zh))).astype(y_ref.dtype)
    for i, (s, h) in enumerate(inst):
        ebd = jnp.exp(gl_ref[s, 0:1, h:h + 1])
        s_ref[s, h] = s_ref[s, h] * ebd + _dot_tn(kd_ref[s, :, heads[h]], vn_ref[i])

    @pl.when(j == pl.num_programs(0) - 1)
    def _():
        sfin_ref[...] = s_ref[...]


def deltanet(proj, cw, hist, s0, ab, nw, nseq, seq_len, c, g):
    c = min(c, seq_len)
    nblk = seq_len // c
    g = min(g, nblk)
    assert seq_len % c == 0 and c % 8 == 0 and nblk % g == 0
    ntile = nblk // g
    rows_t = c * g
    nhs = hist.shape[0]
    assert nhs in (1, nseq) and s0.shape[0] == nhs
    hidx3 = (lambda s, j: (s, 0, 0)) if nhs > 1 else (lambda s, j: (0, 0, 0))
    nh, dk, dv = DN_HEADS, DN_DK, DN_DV
    zw = nh * dv
    rows = nseq * seq_len
    ni = g * nh
    row_blk = lambda w: pl.BlockSpec((rows_t, w), lambda s, j: (s * ntile + j, 0))
    u, w, qd, kd, attn, gl = pl.pallas_call(
        functools.partial(_dn_prep_kernel, c=c, g=g),
        grid=(nseq, ntile),
        in_specs=[pl.BlockSpec((rows_t, DN_QKV), lambda s, j: (s * ntile + j, COL_QKV // DN_QKV)),
                  pl.BlockSpec((rows_t, SMALL_W), lambda s, j: (s * ntile + j, COL_SMALL // SMALL_W)),
                  pl.BlockSpec(cw.shape, lambda s, j: (0, 0)),
                  pl.BlockSpec((None, 8, DN_QKV), hidx3),
                  pl.BlockSpec(ab.shape, lambda s, j: (0, 0))],
        out_specs=[row_blk(zw), row_blk(zw), row_blk(zw), row_blk(zw),
                   pl.BlockSpec((g, nh, c, c), lambda s, j: (s * ntile + j, 0, 0, 0)),
                   row_blk(LANE)],
        out_shape=[jax.ShapeDtypeStruct((rows, zw), F32),
                   jax.ShapeDtypeStruct((rows, zw), BF16),
                   jax.ShapeDtypeStruct((rows, zw), BF16),
                   jax.ShapeDtypeStruct((rows, zw), BF16),
                   jax.ShapeDtypeStruct((nseq * nblk, nh, c, c), BF16),
                   jax.ShapeDtypeStruct((rows, LANE), F32)],
        scratch_shapes=[pltpu.VMEM((rows_t + 8, DN_QKV), F32),
                        pltpu.VMEM((rows_t, DN_QKV), F32),
                        pltpu.VMEM((ni, 2 * c, dk), BF16),
                        pltpu.VMEM((ni, c, dk), BF16),
                        pltpu.VMEM((ni, c, c), F32),
                        pltpu.VMEM((ni, c, dk + dv), BF16),
                        pltpu.VMEM((ni, c, c), F32),
                        pltpu.VMEM((ni, c, c), F32),
                        pltpu.VMEM((ni, c, c), BF16)],
        compiler_params=_cparams(("parallel", "arbitrary")),
        name="dn_prep",
    )(proj, proj, cw, hist, ab)

    seq3 = lambda a: a.reshape(nseq, seq_len, a.shape[1])
    blk3 = lambda wdt, col=0: pl.BlockSpec((nseq, c, wdt), lambda j: (0, j, col))
    y, sfin = pl.pallas_call(
        functools.partial(_dn_scan_kernel, c=c, nseq=nseq, shared_s0=nhs == 1),
        grid=(nblk,),
        in_specs=[blk3(zw), blk3(zw), blk3(zw), blk3(zw),
                  pl.BlockSpec((nseq, None, nh, c, c), lambda j: (0, j, 0, 0, 0)),
                  blk3(LANE), blk3(zw, COL_Z // zw),
                  pl.BlockSpec(s0.shape, lambda j: (0, 0, 0, 0)),
                  pl.BlockSpec(nw.shape, lambda j: (0, 0))],
        out_specs=[blk3(zw), pl.BlockSpec((nseq, nh, dk, dv), lambda j: (0, 0, 0, 0))],
        out_shape=[jax.ShapeDtypeStruct((nseq, seq_len, zw), BF16),
                   jax.ShapeDtypeStruct((nseq, nh, dk, dv), F32)],
        scratch_shapes=[pltpu.VMEM((nseq, nh, dk, dv), F32),
                        pltpu.VMEM((nseq * nh, 2 * c, dv), F32),
                        pltpu.VMEM((nseq * nh, c, dv), BF16)],
        compiler_params=_cparams(("arbitrary",)),
        name="dn_scan",
    )(seq3(u), seq3(w), seq3(qd), seq3(kd), attn.reshape(nseq, nblk, nh, c, c), seq3(gl), seq3(proj),
      s0, nw)
    return y.reshape(rows, zw), sfin


def _rope_tables(pos):
    half = QK_ROPE // 2
    inv = ROPE_THETA ** (-jnp.arange(half, dtype=F32) / half)
    ang = pos.astype(F32)[:, None] * inv[None, :]
    cos, sin = jnp.cos(ang), jnp.sin(ang)
    zeros = jnp.zeros((pos.shape[0], LANE - QK_ROPE), F32)
    cq = jnp.concatenate([cos, cos, zeros], axis=1)
    sq = jnp.concatenate([sin, sin, zeros], axis=1)
    ck = jnp.concatenate([cos, cos, sin, sin], axis=1)
    return cq, sq, ck


def _rot_cols(w):
    half = w.shape[-1] // 2
    return jnp.concatenate([-w[..., half:], w[..., :half]], axis=-1)


def _row(v, width=None):
    v = v.astype(F32).reshape(1, -1)
    if width is not None and v.shape[1] < width:
        v = jnp.pad(v, ((0, 0), (0, width - v.shape[1])))
    return v


def kernel(x_prompt, x_sample, cache_mla_ckv, cache_mla_krope, state_dn_s, state_dn_conv, state_cf_conv, meta_tokens, norm_gains, w_in, mla_gq, mla_gkv, w_uq, w_uk, w_uv, dn_conv_w, dn_a_log, dn_dt_bias, dn_norm_w, w_out, cf_w_pw1, cf_b_pw1, cf_w_dw, cf_b_dw, cf_ln_g, cf_ln_b, cf_w_pw2, cf_b_pw2, w_gate, w_up, w_down):
    bp, lp, d = x_prompt.shape
    bs, ls, _ = x_sample.shape
    n_meta = meta_tokens.shape[0]
    past = cache_mla_ckv.shape[2] - n_meta
    depth = norm_gains.shape[0]
    assert n_meta == N_META and ls == n_meta and n_meta <= CHUNK
    assert past % CHUNK == 0 and ls <= CHUNK and lp % CHUNK == 0
    ns = bs + 1
    hw = MLA_HEADS * QK_NOPE

    hp = x_prompt.reshape(bp * lp, d)
    hs = jnp.concatenate([x_sample.reshape(bs * ls, d), meta_tokens.astype(F32)], axis=0)
    meta_rows = slice(bs * ls, bs * ls + n_meta)

    pos_p = n_meta + jnp.arange(lp)
    pos_s = jnp.concatenate([jnp.tile(n_meta + past + jnp.arange(ls), bs), jnp.arange(n_meta)])
    tab_p = _rope_tables(pos_p)
    tab_s = _rope_tables(pos_s)
    zero_d = jnp.zeros((1, d), F32)
    wg_all, wu_all, wd_all = w_gate.astype(BF16), w_up.astype(BF16), w_down.astype(BF16)

    outs = {k: [] for k in ("p_ckv", "p_kr", "p_s", "p_conv", "p_cf", "s_ckv", "s_kr", "s_s", "s_conv", "s_cf")}
    for layer in range(depth):
        ng = norm_gains[layer].astype(F32)
        g0, g1, g2, g3 = (ng[i:i + 1] for i in range(4))
        if layer % 2 == 0:
            e = layer // 2
            offs = np.cumsum((Q_LORA, KV_LORA, QK_ROPE, DN_QKV, DN_HEADS * DN_DV, DN_HEADS, DN_HEADS))
            wi = w_in[e]
            w_qd, w_kvd, w_kr = wi[:, :offs[0]], wi[:, offs[0]:offs[1]], wi[:, offs[1]:offs[2]]
            w_qkv, w_z = wi[:, offs[2]:offs[3]], wi[:, offs[3]:offs[4]]
            w_a, w_b = wi[:, offs[4]:offs[5]], wi[:, offs[5]:offs[6]]
            w_proj = jnp.concatenate(
                [w_qkv, w_z, w_qd, w_kvd, w_kr, _rot_cols(w_kr), w_a, w_b,
                 jnp.zeros((d, SMALL_W - 2 * QK_ROPE - 2 * DN_HEADS), F32)], axis=1).astype(BF16)
            zero_proj = jnp.zeros((1, PROJ_W), F32)
            wq3 = w_uq[e]
            wq_n = wq3[:, :, :QK_NOPE].reshape(Q_LORA, hw)
            wq_r = wq3[:, :, QK_NOPE:]
            pad_r = lambda w: jnp.pad(w, ((0, 0), (0, 0), (0, LANE - QK_ROPE))).reshape(Q_LORA, hw)
            wq = jnp.concatenate([wq_n, pad_r(wq_r), pad_r(_rot_cols(wq_r))], axis=1).astype(BF16)
            wk = w_uk[e].reshape(KV_LORA, hw).astype(BF16)
            wv = w_uv[e].reshape(KV_LORA, hw).astype(BF16)
            wkv = jnp.concatenate([wk, wv], axis=1)
            gq, gkv = _row(mla_gq[e]), _row(mla_gkv[e])
            cw = jnp.pad(dn_conv_w[e].astype(F32), ((0, 8 - DN_CONV), (0, 0)))
            ab = jnp.concatenate([_row(dn_a_log[e], LANE), _row(dn_dt_bias[e], LANE),
                                  jnp.zeros((6, LANE), F32)], axis=0)
            nw = _row(dn_norm_w[e])
            wo = w_out[e].astype(BF16)
            wo_mla, wo_dn = wo[:hw], wo[hw:]

            proj_s = norm_matmul(hs, g0, w_proj, zero_proj, 1024, 768)
            qn_s, qr_s, ckv_s, kn_s, v_s, kro_s, krp_s = mla_prep(proj_s, gq, gkv, wq, wk, wv, *tab_s, 512, False)
            lh = n_meta + past
            lh_pad = -(-lh // LANE) * LANE
            hist_ckv = jnp.pad(cache_mla_ckv[e].astype(F32), ((0, 1), (0, lh_pad - lh), (0, 0)))
            hkn, hv = kv_up(hist_ckv.reshape(ns * lh_pad, KV_LORA), wkv, lh_pad)
            hkrp = jnp.pad(cache_mla_krope[e].astype(BF16),
                           ((0, 1), (0, lh_pad - lh), (0, LANE - QK_ROPE)))
            hlen_s = jnp.concatenate([jnp.full((bs,), lh, jnp.int32), jnp.zeros((1,), jnp.int32)])
            ymla_s = attention(qn_s, qr_s, kn_s, krp_s, v_s, hkn.reshape(ns, lh_pad, hw), hkrp,
                               hv.reshape(ns, lh_pad, hw), hlen_s, ns, ls, ls)
            conv_hist_s = jnp.pad(state_dn_conv[e].astype(F32), ((0, 1), (8 - (DN_CONV - 1), 0), (0, 0)))
            s0_s = jnp.pad(state_dn_s[e].astype(F32), ((0, 1), (0, 0), (0, 0), (0, 0)))
            ydn_s, sfin_s = deltanet(proj_s, cw, conv_hist_s, s0_s, ab, nw, ns, ls, CHUNK, 1)
            hs = matmul_resnorm([ymla_s, ydn_s], [wo_mla, wo_dn], zero_d, g1, hs, 512)

            proj_p = norm_matmul(hp, g0, w_proj, zero_proj, 1024, 768)
            qn_p, qr_p, ckv_p, kn_p, vt_p, kro_p, krp_p = mla_prep(proj_p, gq, gkv, wq, wk, wv.T, *tab_p,
                                                                   512, True)
            hlen_p = jnp.full((bp,), n_meta, jnp.int32)
            ymla_p = attention_t(qn_p, qr_p, kn_p, krp_p, vt_p, kn_s[meta_rows], krp_s[meta_rows],
                                 v_s[meta_rows].T, hlen_p, bp, lp)
            conv_hist_p = jnp.pad(proj_s[meta_rows, COL_QKV:COL_QKV + DN_QKV][-(DN_CONV - 1):],
                                  ((8 - (DN_CONV - 1), 0), (0, 0)))[None]
            ydn_p, sfin_p = deltanet(proj_p, cw, conv_hist_p, sfin_s[bs:], ab, nw, bp, lp, CHUNK, 2)
            hp = matmul_resnorm([ymla_p, ydn_p], [wo_mla, wo_dn], zero_d, g1, hp, 512)

            bc = lambda a: jnp.broadcast_to(a[None], (bp,) + a.shape)
            outs["p_ckv"].append(jnp.concatenate([bc(ckv_s[meta_rows]), ckv_p.reshape(bp, lp, KV_LORA)], axis=1))
            outs["p_kr"].append(jnp.concatenate([bc(kro_s[meta_rows]), kro_p.reshape(bp, lp, QK_ROPE)], axis=1))
            outs["p_s"].append(sfin_p)
            outs["p_conv"].append(proj_p.reshape(bp, lp, PROJ_W)[:, lp - (DN_CONV - 1):, COL_QKV:COL_QKV + DN_QKV])
            outs["s_ckv"].append(ckv_s[:bs * ls].reshape(bs, ls, KV_LORA))
            outs["s_kr"].append(kro_s[:bs * ls].reshape(bs, ls, QK_ROPE))
            outs["s_s"].append(sfin_s[:bs])
            xqkv_s = proj_s[:bs * ls, COL_QKV:COL_QKV + DN_QKV].reshape(bs, ls, DN_QKV)
            outs["s_conv"].append(jnp.concatenate([state_dn_conv[e].astype(F32), xqkv_s], axis=1)[:, -(DN_CONV - 1):])
        else:
            o = layer // 2
            w1 = cf_w_pw1[o].astype(BF16)
            b1 = _row(cf_b_pw1[o])
            wdw = jnp.broadcast_to(cf_w_dw[o].astype(F32)[:, None, :], (CF_KERNEL, 8, d))
            bdw, lg, lb = _row(cf_b_dw[o]), _row(cf_ln_g[o]), _row(cf_ln_b[o])
            w2 = cf_w_pw2[o].astype(BF16)
            b2 = _row(cf_b_pw2[o])
            keep = CF_KERNEL - 1

            hist_s = jnp.pad(state_cf_conv[o].astype(F32), ((0, 1), (CF_HALO - keep, 0), (0, 0)))
            c_s, tail_s = glu_conv(hs, g0, w1, b1, hist_s, wdw, bdw, ns, ls, 1024, 512, 256)
            hs = ln_matmul_resnorm(c_s, lg, lb, w2, b2, g1, hs, 512, 256)

            c_p, tail_p = glu_conv(hp, g0, w1, b1, tail_s[bs:], wdw, bdw, bp, lp, 1024, 512, 256)
            hp = ln_matmul_resnorm(c_p, lg, lb, w2, b2, g1, hp, 512, 256)

            outs["p_cf"].append(tail_p[:, CF_HALO - keep:])
            outs["s_cf"].append(tail_s[:bs, CF_HALO - keep:])
        hs = ffn(hs, g2, wg_all, wu_all, wd_all, g3, layer, 512, 512)
        hp = ffn(hp, g2, wg_all, wu_all, wd_all, g3, layer, 1024, 512)

    y_prompt = hp.reshape(bp, lp, d)
    y_sample = hs[:bs * ls].reshape(bs, ls, d)
    st = lambda k: jnp.stack(outs[k])
    return (y_prompt, y_sample, st("p_ckv"), st("p_kr"), st("p_s"), st("p_conv"), st("p_cf"),
            st("s_ckv"), st("s_kr"), st("s_s"), st("s_conv"), st("s_cf"))
```

```python
import functools

import numpy as np
import jax
import jax.numpy as jnp
from jax import lax
from jax.experimental import pallas as pl
from jax.experimental.pallas import tpu as pltpu

F32 = jnp.float32
BF16 = jnp.bfloat16

EPS = 1e-6
NEG_INF = -1e30
CHUNK = 64
N_META = 16
MLA_HEADS = 8
Q_LORA = 512
KV_LORA = 512
QK_NOPE = 128
QK_ROPE = 64
V_HEAD = 128
ROPE_THETA = 10000.0
DN_HEADS = 8
DN_DK = 128
DN_DV = 128
DN_CONV = 4
DN_QKV = DN_HEADS * (2 * DN_DK + DN_DV)
CF_KERNEL = 31

LANE = 128
ATT_TILE = 256
VMEM_LIMIT = 56 * 1024 * 1024

COL_QKV = 0
COL_Z = DN_QKV
COL_QD = COL_Z + DN_HEADS * DN_DV
COL_KVD = COL_QD + Q_LORA
COL_SMALL = COL_KVD + KV_LORA
SMALL_W = 256
PROJ_W = COL_SMALL + SMALL_W


def _cparams(sem):
    return pltpu.CompilerParams(dimension_semantics=sem, vmem_limit_bytes=VMEM_LIMIT)


def _rms(x, g):
    return x * lax.rsqrt(jnp.mean(x * x, axis=-1, keepdims=True) + EPS) * g


def _sigmoid(x):
    return 1.0 / (1.0 + jnp.exp(-x))


def _dot(a, b):
    return jnp.dot(a, b, preferred_element_type=F32)


def _dot_nt(a, b):
    return lax.dot_general(a, b, (((1,), (1,)), ((), ())), preferred_element_type=F32)


def _dot_tn(a, b):
    return lax.dot_general(a, b, (((0,), (0,)), ((), ())), preferred_element_type=F32)


def _row_tile(m, pref):
    t = min(pref, m)
    while m % t:
        t //= 2
    return t


def _norm_mm_kernel(x_ref, g_ref, w_ref, b_ref, o_ref, xn_ref):
    @pl.when(pl.program_id(1) == 0)
    def _():
        xn_ref[...] = _rms(x_ref[...], g_ref[...]).astype(BF16)

    o_ref[...] = (_dot(xn_ref[...], w_ref[...]) + b_ref[...]).astype(o_ref.dtype)


CF_HALO = 32


def _glu_conv_kernel(x_ref, g_ref, wa_ref, wg_ref, ba_ref, bg_ref, hist_ref, cw_ref, cb_ref,
                     c_ref, tail_ref, xn_ref, ext_ref, sh_ref, carry_ref, *, tm, chunk, tiles_per_seq,
                     col_major):
    first = CF_HALO - (CF_KERNEL - 1)
    sub = 8
    span = chunk + CF_HALO - sub

    if col_major:
        i = pl.program_id(1)
        j = pl.program_id(0)
        xn_ref[...] = _rms(x_ref[...], g_ref[...]).astype(BF16)
    else:
        i = pl.program_id(0)
        j = pl.program_id(1)

        @pl.when(j == 0)
        def _():
            xn_ref[...] = _rms(x_ref[...], g_ref[...]).astype(BF16)

    seq_start = (i % tiles_per_seq) == 0

    @pl.when(seq_start)
    def _():
        ext_ref[0:CF_HALO, :] = hist_ref[...]

    @pl.when(jnp.logical_not(seq_start))
    def _():
        ext_ref[0:CF_HALO, :] = carry_ref[j]

    def glu_chunk(c):
        lo = c * chunk
        xc = xn_ref[lo:lo + chunk, :]
        a = _dot(xc, wa_ref[...]) + ba_ref[...]
        gt = _dot(xc, wg_ref[...]) + bg_ref[...]
        ext_ref[CF_HALO + lo:CF_HALO + lo + chunk, :] = a * _sigmoid(gt)

    nchunk = tm // chunk
    glu_chunk(0)
    for c in range(nchunk):
        lo = c * chunk
        if c + 1 < nchunk:
            glu_chunk(c + 1)
        for sft in range(1, sub):
            sh_ref[sft - 1, 0:span, :] = ext_ref[lo + sft:lo + sft + span, :]
        rg = min(chunk, 4 * sub)
        for r0 in range(0, chunk, rg):
            acc = None
            for t in range(CF_KERNEL):
                sft = (first + t) % sub
                base = first + t - sft + r0
                if sft == 0:
                    src = ext_ref[lo + base:lo + base + rg, :]
                else:
                    src = sh_ref[sft - 1, base:base + rg, :]
                term = src.reshape(rg // sub, sub, src.shape[1]) * cw_ref[t]
                acc = term if acc is None else acc + term
            c_ref[lo + r0:lo + r0 + rg, :] = acc.reshape(rg, acc.shape[2]) + cb_ref[...]
    tail = ext_ref[tm:tm + CF_HALO, :]
    carry_ref[j] = tail
    tail_ref[...] = tail


def norm_matmul(x, g, w, b, tm, tn, out_dtype=F32):
    m, k = x.shape
    n = w.shape[1]
    tm = _row_tile(m, tm)
    return pl.pallas_call(
        _norm_mm_kernel,
        grid=(m // tm, n // tn),
        in_specs=[pl.BlockSpec((tm, k), lambda i, j: (i, 0)),
                  pl.BlockSpec((1, k), lambda i, j: (0, 0)),
                  pl.BlockSpec((k, tn), lambda i, j: (0, j)),
                  pl.BlockSpec((1, tn), lambda i, j: (0, j))],
        out_specs=pl.BlockSpec((tm, tn), lambda i, j: (i, j)),
        out_shape=jax.ShapeDtypeStruct((m, n), out_dtype),
        scratch_shapes=[pltpu.VMEM((tm, k), BF16)],
        compiler_params=_cparams(("parallel", "arbitrary")),
        name="norm_matmul",
    )(x, g, w, b)


def glu_conv(x, g, w, b, hist, cw, cb, nseq, seq_len, tm, tn, chunk):
    m, k = x.shape
    n = w.shape[1] // 2
    tm = _row_tile(seq_len, tm)
    chunk = min(chunk, tm)
    assert tm % chunk == 0 and chunk % 8 == 0
    tiles_per_seq = seq_len // tm
    nb = n // tn
    nhs = hist.shape[0]
    col_major = tiles_per_seq == 1 and tm < 256
    if col_major:
        grid = (nb, m // tm)
        spec = lambda shape, f: pl.BlockSpec(shape, lambda j, i: f(i, j))
    else:
        grid = (m // tm, nb)
        spec = lambda shape, f: pl.BlockSpec(shape, f)
    hidx = ((lambda i, j: (i // tiles_per_seq, 0, j)) if nhs > 1 else (lambda i, j: (0, 0, j)))
    c, tails = pl.pallas_call(
        functools.partial(_glu_conv_kernel, tm=tm, chunk=chunk, tiles_per_seq=tiles_per_seq,
                          col_major=col_major),
        grid=grid,
        in_specs=[spec((tm, k), lambda i, j: (i, 0)),
                  spec((1, k), lambda i, j: (0, 0)),
                  spec((k, tn), lambda i, j: (0, j)),
                  spec((k, tn), lambda i, j: (0, j + nb)),
                  spec((1, tn), lambda i, j: (0, j)),
                  spec((1, tn), lambda i, j: (0, j + nb)),
                  spec((None, CF_HALO, tn), hidx),
                  spec((cw.shape[0], 8, tn), lambda i, j: (0, 0, j)),
                  spec((1, tn), lambda i, j: (0, j))],
        out_specs=[spec((tm, tn), lambda i, j: (i, j)),
                   spec((None, CF_HALO, tn), lambda i, j: (i, 0, j))],
        out_shape=[jax.ShapeDtypeStruct((m, n), F32),
                   jax.ShapeDtypeStruct((m // tm, CF_HALO, n), F32)],
        scratch_shapes=[pltpu.VMEM((tm, k), BF16),
                        pltpu.VMEM((tm + CF_HALO, tn), F32),
                        pltpu.VMEM((7, chunk + CF_HALO - 8, tn), F32),
                        pltpu.VMEM((nb, CF_HALO, tn), F32)],
        compiler_params=_cparams(("arbitrary", "arbitrary")),
        name="glu_conv",
    )(x, g, w, w, b, b, hist, cw, cb)
    return c, tails.reshape(nseq, tiles_per_seq, CF_HALO, n)[:, -1]


def _mm_resnorm_kernel(*refs, n_in):
    xs = refs[:n_in]
    ws = refs[n_in:2 * n_in]
    b_ref, g_ref, res_ref, o_ref = refs[2 * n_in:]
    y = b_ref[...]
    for x_ref, w_ref in zip(xs, ws):
        y = y + _dot(x_ref[...], w_ref[...])
    o_ref[...] = res_ref[...] + _rms(y, g_ref[...])


def _ln_mm_resnorm_kernel(c_ref, lg_ref, lb_ref, w_ref, b_ref, g_ref, res_ref, o_ref, *, tm, chunk):
    for c in range(tm // chunk):
        rows = slice(c * chunk, (c + 1) * chunk)
        x = c_ref[rows, :]
        xc = x - jnp.mean(x, axis=-1, keepdims=True)
        y = xc * lax.rsqrt(jnp.mean(xc * xc, axis=-1, keepdims=True) + EPS) * lg_ref[...] + lb_ref[...]
        a = (y * _sigmoid(y)).astype(BF16)
        z = _dot(a, w_ref[...]) + b_ref[...]
        o_ref[rows, :] = res_ref[rows, :] + _rms(z, g_ref[...])


def ln_matmul_resnorm(c, lg, lb, w, b, g, res, tm, chunk):
    m, n = res.shape
    k = c.shape[1]
    tm = _row_tile(m, tm)
    chunk = min(chunk, tm)
    assert tm % chunk == 0
    vec = lambda width: pl.BlockSpec((1, width), lambda i: (0, 0))
    return pl.pallas_call(
        functools.partial(_ln_mm_resnorm_kernel, tm=tm, chunk=chunk),
        grid=(m // tm,),
        in_specs=[pl.BlockSpec((tm, k), lambda i: (i, 0)), vec(k), vec(k),
                  pl.BlockSpec(w.shape, lambda i: (0, 0)), vec(n), vec(n),
                  pl.BlockSpec((tm, n), lambda i: (i, 0))],
        out_specs=pl.BlockSpec((tm, n), lambda i: (i, 0)),
        out_shape=jax.ShapeDtypeStruct((m, n), F32),
        compiler_params=_cparams(("parallel",)),
        name="ln_matmul_resnorm",
    )(c, lg, lb, w, b, g, res)


def matmul_resnorm(xs, ws, b, g, res, tm):
    m, n = res.shape
    tm = _row_tile(m, tm)
    n_in = len(xs)
    in_specs = ([pl.BlockSpec((tm, x.shape[1]), lambda i: (i, 0)) for x in xs]
                + [pl.BlockSpec(w.shape, lambda i: (0, 0)) for w in ws]
                + [pl.BlockSpec((1, n), lambda i: (0, 0)),
                   pl.BlockSpec((1, n), lambda i: (0, 0)),
                   pl.BlockSpec((tm, n), lambda i: (i, 0))])
    return pl.pallas_call(
        functools.partial(_mm_resnorm_kernel, n_in=n_in),
        grid=(m // tm,),
        in_specs=in_specs,
        out_specs=pl.BlockSpec((tm, n), lambda i: (i, 0)),
        out_shape=jax.ShapeDtypeStruct((m, n), F32),
        compiler_params=_cparams(("parallel",)),
        name="matmul_resnorm",
    )(*xs, *ws, b, g, res)


def _ffn_kernel(h_ref, g2_ref, wg_ref, wu_ref, wd_ref, g3_ref, o_ref, xn_ref, acc_ref):
    j = pl.program_id(1)

    @pl.when(j == 0)
    def _():
        xn_ref[...] = _rms(h_ref[...], g2_ref[...]).astype(BF16)
        acc_ref[...] = jnp.zeros_like(acc_ref)

    xn = xn_ref[...]
    gate = _dot(xn, wg_ref[...])
    up = _dot(xn, wu_ref[...])
    a = (gate * _sigmoid(gate) * up).astype(BF16)
    acc_ref[...] += _dot(a, wd_ref[...])

    @pl.when(j == pl.num_programs(1) - 1)
    def _():
        o_ref[...] = h_ref[...] + _rms(acc_ref[...], g3_ref[...])


def ffn(h, g2, wg, wu, wd, g3, layer, tm, tf):
    m, d = h.shape
    f = wg.shape[2]
    tm = _row_tile(m, tm)
    return pl.pallas_call(
        _ffn_kernel,
        grid=(m // tm, f // tf),
        in_specs=[pl.BlockSpec((tm, d), lambda i, j: (i, 0)),
                  pl.BlockSpec((1, d), lambda i, j: (0, 0)),
                  pl.BlockSpec((None, d, tf), lambda i, j: (layer, 0, j)),
                  pl.BlockSpec((None, d, tf), lambda i, j: (layer, 0, j)),
                  pl.BlockSpec((None, tf, d), lambda i, j: (layer, j, 0)),
                  pl.BlockSpec((1, d), lambda i, j: (0, 0))],
        out_specs=pl.BlockSpec((tm, d), lambda i, j: (i, 0)),
        out_shape=jax.ShapeDtypeStruct((m, d), F32),
        scratch_shapes=[pltpu.VMEM((tm, d), BF16), pltpu.VMEM((tm, d), F32)],
        compiler_params=_cparams(("parallel", "arbitrary")),
        name="ffn",
    )(h, g2, wg, wu, wd, g3)


def _mla_prep_kernel(qd_ref, kvd_ref, sm_ref, gq_ref, gkv_ref, wq_ref, wk_ref, wv_ref, cq_ref, sq_ref, ck_ref,
                     qn_ref, qr_ref, ckv_ref, kn_ref, v_ref, kro_ref, krp_ref, *, v_transposed):
    hw = MLA_HEADS * QK_NOPE
    scale = (QK_NOPE + QK_ROPE) ** -0.5
    cq = _rms(qd_ref[...], gq_ref[...]).astype(BF16)
    q = _dot(cq, wq_ref[...])
    qn_ref[...] = (q[:, :hw] * scale).astype(BF16)
    cos8 = jnp.tile(cq_ref[...], (1, MLA_HEADS))
    sin8 = jnp.tile(sq_ref[...], (1, MLA_HEADS))
    qr_ref[...] = ((q[:, hw:2 * hw] * cos8 + q[:, 2 * hw:] * sin8) * scale).astype(BF16)
    ckv = _rms(kvd_ref[...], gkv_ref[...])
    ckv_ref[...] = ckv
    ckv_b = ckv.astype(BF16)
    kn_ref[...] = _dot(ckv_b, wk_ref[...]).astype(BF16)
    if v_transposed:
        vt = _dot_nt(wv_ref[...], ckv_b).astype(BF16)
        for t in range(v_ref.shape[0]):
            v_ref[t] = vt[:, t * ATT_TILE:(t + 1) * ATT_TILE]
    else:
        v_ref[...] = _dot(ckv_b, wv_ref[...]).astype(BF16)
    y = sm_ref[:, :LANE] * ck_ref[...]
    kro = y + pltpu.roll(y, QK_ROPE, 1)
    kro_ref[...] = kro[:, :QK_ROPE]
    lane = lax.broadcasted_iota(jnp.int32, kro.shape, 1)
    krp_ref[...] = jnp.where(lane < QK_ROPE, kro, 0.0).astype(BF16)


def mla_prep(proj, gq, gkv, wq, wk, wv, cq_tab, sq_tab, ck_tab, tm, v_transposed):
    m = proj.shape[0]
    tm = _row_tile(min(m, cq_tab.shape[0]), tm)
    nt = cq_tab.shape[0] // tm
    hw = MLA_HEADS * QK_NOPE
    tab = lambda: pl.BlockSpec((tm, LANE), lambda i: (i % nt, 0))
    full = lambda a: pl.BlockSpec(a.shape, lambda i: (0, 0))
    if v_transposed:
        assert tm % ATT_TILE == 0
        v_spec = pl.BlockSpec((tm // ATT_TILE, hw, ATT_TILE), lambda i: (i, 0, 0))
        v_shape = jax.ShapeDtypeStruct((m // ATT_TILE, hw, ATT_TILE), BF16)
    else:
        v_spec = pl.BlockSpec((tm, hw), lambda i: (i, 0))
        v_shape = jax.ShapeDtypeStruct((m, hw), BF16)
    return pl.pallas_call(
        functools.partial(_mla_prep_kernel, v_transposed=v_transposed),
        grid=(m // tm,),
        in_specs=[pl.BlockSpec((tm, Q_LORA), lambda i: (i, COL_QD // Q_LORA)),
                  pl.BlockSpec((tm, KV_LORA), lambda i: (i, COL_KVD // KV_LORA)),
                  pl.BlockSpec((tm, SMALL_W), lambda i: (i, COL_SMALL // SMALL_W)),
                  full(gq), full(gkv), full(wq), full(wk), full(wv), tab(), tab(), tab()],
        out_specs=[pl.BlockSpec((tm, hw), lambda i: (i, 0)),
                   pl.BlockSpec((tm, hw), lambda i: (i, 0)),
                   pl.BlockSpec((tm, KV_LORA), lambda i: (i, 0)),
                   pl.BlockSpec((tm, hw), lambda i: (i, 0)),
                   v_spec,
                   pl.BlockSpec((tm, QK_ROPE), lambda i: (i, 0)),
                   pl.BlockSpec((tm, LANE), lambda i: (i, 0))],
        out_shape=[jax.ShapeDtypeStruct((m, hw), BF16),
                   jax.ShapeDtypeStruct((m, hw), BF16),
                   jax.ShapeDtypeStruct((m, KV_LORA), F32),
                   jax.ShapeDtypeStruct((m, hw), BF16),
                   v_shape,
                   jax.ShapeDtypeStruct((m, QK_ROPE), F32),
                   jax.ShapeDtypeStruct((m, LANE), BF16)],
        compiler_params=_cparams(("parallel",)),
        name="mla_prep",
    )(proj, proj, proj, gq, gkv, wq, wk, wv, cq_tab, sq_tab, ck_tab)


def _kv_up_kernel(ckv_ref, wkv_ref, kn_ref, v_ref):
    hw = MLA_HEADS * QK_NOPE
    kv = _dot(ckv_ref[...].astype(BF16), wkv_ref[...])
    kn_ref[...] = kv[:, :hw].astype(BF16)
    v_ref[...] = kv[:, hw:].astype(BF16)


def kv_up(ckv, wkv, tm):
    m = ckv.shape[0]
    tm = _row_tile(m, tm)
    hw = MLA_HEADS * QK_NOPE
    return pl.pallas_call(
        _kv_up_kernel,
        grid=(m // tm,),
        in_specs=[pl.BlockSpec((tm, KV_LORA), lambda i: (i, 0)),
                  pl.BlockSpec(wkv.shape, lambda i: (0, 0))],
        out_specs=[pl.BlockSpec((tm, hw), lambda i: (i, 0)),
                   pl.BlockSpec((tm, hw), lambda i: (i, 0))],
        out_shape=[jax.ShapeDtypeStruct((m, hw), BF16), jax.ShapeDtypeStruct((m, hw), BF16)],
        compiler_params=_cparams(("parallel",)),
        name="kv_up",
    )(ckv, wkv)


def _attn_kernel(hlen_ref, qn_ref, qr_ref, kn_ref, kr_ref, v_ref, hkn_ref, hkr_ref, hv_ref, o_ref,
                 m_ref, l_ref, acc_ref, a_ref, qc_ref, s_ref, p_ref, hs_ref, hp_ref, *, tile, lh):
    s_idx = pl.program_id(0)
    qt = pl.program_id(1)
    hlen = hlen_ref[s_idx]
    nh = MLA_HEADS
    heads = [slice(h * LANE, (h + 1) * LANE) for h in range(nh)]

    for h, hs in enumerate(heads):
        qc_ref[h] = jnp.concatenate([qn_ref[:, hs], qr_ref[:, hs]], axis=1)

    def process(get_k, get_v, s_scr, p_scr, valid, first):
        for h in range(nh):
            s_scr[h] = _dot_nt(qc_ref[h], get_k(h))
        for h, hs in enumerate(heads):
            s = s_scr[h]
            if valid is not None:
                s = jnp.where(valid, s, NEG_INF)
            m_cur = jnp.max(s, axis=1, keepdims=True)
            if first:
                m_new = jnp.broadcast_to(m_cur, (tile, LANE))
            else:
                m_prev = m_ref[:, hs]
                m_new = jnp.maximum(m_prev, m_cur)
            p = jnp.exp(s - m_new[:, :1])
            if valid is not None:
                p = jnp.where(valid, p, 0.0)
            p_scr[h] = p.astype(BF16)
            l_cur = jnp.sum(p, axis=1, keepdims=True)
            if first:
                l_ref[:, hs] = jnp.broadcast_to(l_cur, (tile, LANE))
            else:
                alpha = jnp.exp(m_prev - m_new)
                a_ref[:, hs] = alpha
                l_ref[:, hs] = alpha * l_ref[:, hs] + l_cur
            m_ref[:, hs] = m_new
        for h, hs in enumerate(heads):
            pv = _dot(p_scr[h], get_v(h))
            if first:
                acc_ref[:, hs] = pv
            else:
                acc_ref[:, hs] = a_ref[:, hs] * acc_ref[:, hs] + pv

    hvalid = lax.broadcasted_iota(jnp.int32, (tile, lh), 1) < hlen
    process(lambda h: jnp.concatenate([hkn_ref[:, heads[h]], hkr_ref[...]], axis=1),
            lambda h: hv_ref[:, heads[h]], hs_ref, hp_ref, hvalid, True)

    def own_tile(kt, valid):
        rows = pl.ds(pl.multiple_of(kt * tile, tile), tile)
        process(lambda h: jnp.concatenate([kn_ref[rows, heads[h]], kr_ref[rows, :]], axis=1),
                lambda h: v_ref[rows, heads[h]], s_ref, p_ref, valid, False)

    def body(kt, carry):
        own_tile(kt, None)
        return carry

    lax.fori_loop(0, qt, body, 0)
    if tile > CHUNK:
        row = lax.broadcasted_iota(jnp.int32, (tile, tile), 0) // CHUNK
        col = lax.broadcasted_iota(jnp.int32, (tile, tile), 1) // CHUNK
        dvalid = col <= row
    else:
        dvalid = None
    own_tile(qt, dvalid)

    for hs in heads:
        o_ref[:, hs] = (acc_ref[:, hs] / l_ref[:, hs]).astype(o_ref.dtype)


def attention(qn, qr, kn, krp, v, hkn, hkrp, hv, hlen, nseq, seq_len, tile):
    hw = MLA_HEADS * LANE
    tile = min(tile, seq_len)
    assert seq_len % tile == 0 and (tile % CHUNK == 0 or seq_len == tile <= CHUNK)
    nqt = seq_len // tile
    nhs, lh = hkn.shape[0], hkn.shape[1]
    assert nhs in (1, nseq)
    hidx = (lambda s, q, hl: (s, 0, 0)) if nhs > 1 else (lambda s, q, hl: (0, 0, 0))
    grid_spec = pltpu.PrefetchScalarGridSpec(
        num_scalar_prefetch=1,
        grid=(nseq, nqt),
        in_specs=[pl.BlockSpec((tile, hw), lambda s, q, hl: (s * nqt + q, 0)),
                  pl.BlockSpec((tile, hw), lambda s, q, hl: (s * nqt + q, 0)),
                  pl.BlockSpec((seq_len, hw), lambda s, q, hl: (s, 0)),
                  pl.BlockSpec((seq_len, LANE), lambda s, q, hl: (s, 0)),
                  pl.BlockSpec((seq_len, hw), lambda s, q, hl: (s, 0)),
                  pl.BlockSpec((None, lh, hw), hidx),
                  pl.BlockSpec((None, lh, LANE), hidx),
                  pl.BlockSpec((None, lh, hw), hidx)],
        out_specs=pl.BlockSpec((tile, hw), lambda s, q, hl: (s * nqt + q, 0)),
        scratch_shapes=[pltpu.VMEM((tile, hw), F32), pltpu.VMEM((tile, hw), F32),
                        pltpu.VMEM((tile, hw), F32), pltpu.VMEM((tile, hw), F32),
                        pltpu.VMEM((MLA_HEADS, tile, 2 * LANE), BF16),
                        pltpu.VMEM((MLA_HEADS, tile, tile), F32),
                        pltpu.VMEM((MLA_HEADS, tile, tile), BF16),
                        pltpu.VMEM((MLA_HEADS, tile, lh), F32),
                        pltpu.VMEM((MLA_HEADS, tile, lh), BF16)],
    )
    return pl.pallas_call(
        functools.partial(_attn_kernel, tile=tile, lh=lh),
        grid_spec=grid_spec,
        out_shape=jax.ShapeDtypeStruct((nseq * seq_len, hw), BF16),
        compiler_params=_cparams(("parallel", "arbitrary")),
        name="attention",
    )(hlen, qn, qr, kn, krp, v, hkn, hkrp, hv)


def _attn_t_kernel(hlen_ref, qn_ref, qr_ref, kn_ref, kr_ref, vt_ref, hkn_ref, hkr_ref, hvt_ref, o_ref,
                   m_ref, l_ref, a_ref, acc_ref, qc_ref, s_ref, p_ref, s2_ref, p2_ref, hs_ref, hp_ref,
                   *, tile, lh):
    s_idx = pl.program_id(0)
    qt = pl.program_id(1)
    hlen = hlen_ref[s_idx]
    nh = MLA_HEADS
    heads = [slice(h * LANE, (h + 1) * LANE) for h in range(nh)]

    for h, hs in enumerate(heads):
        qc_ref[h] = jnp.concatenate([qn_ref[:, hs], qr_ref[:, hs]], axis=1)

    def process(get_k, get_vt, s_scr, p_scr, valid, first):
        for h in range(nh):
            s_scr[h] = _dot_nt(get_k(h), qc_ref[h])
        for h in range(nh):
            for q0 in range(0, tile, LANE):
                qs = slice(q0, q0 + LANE)
                s = s_scr[h, :, qs]
                ok = None if valid is None else valid(q0)
                if ok is not None:
                    s = jnp.where(ok, s, NEG_INF)
                m_cur = jnp.max(s, axis=0, keepdims=True)
                if first:
                    m_new = m_cur
                else:
                    m_prev = m_ref[h, :, qs]
                    m_new = jnp.maximum(m_prev, m_cur)
                p = jnp.exp(s - m_new)
                if ok is not None:
                    p = jnp.where(ok, p, 0.0)
                p_scr[h, :, qs] = p.astype(BF16)
                l_cur = jnp.sum(p, axis=0, keepdims=True)
                if first:
                    l_ref[h, :, qs] = l_cur
                else:
                    alpha = jnp.exp(m_prev - m_new)
                    a_ref[h, :, qs] = alpha
                    l_ref[h, :, qs] = alpha * l_ref[h, :, qs] + l_cur
                m_ref[h, :, qs] = m_new
        for h in range(nh):
            pv = _dot(get_vt(h), p_scr[h])
            if first:
                acc_ref[h] = pv
            else:
                acc_ref[h] = a_ref[h] * acc_ref[h] + pv

    hvalid = lax.broadcasted_iota(jnp.int32, (lh, LANE), 0) < hlen
    process(lambda h: jnp.concatenate([hkn_ref[:, heads[h]], hkr_ref[...]], axis=1),
            lambda h: hvt_ref[heads[h], :], hs_ref, hp_ref, lambda q0: hvalid, True)

    def own_tile(kt, valid):
        rows = pl.ds(pl.multiple_of(kt * tile, tile), tile)
        process(lambda h: jnp.concatenate([kn_ref[rows, heads[h]], kr_ref[rows, :]], axis=1),
                lambda h: vt_ref[kt, heads[h], :], s_ref, p_ref, valid, False)

    def own_pair(kp):
        rows = pl.ds(pl.multiple_of(kp * (2 * tile), 2 * tile), 2 * tile)
        process(lambda h: jnp.concatenate([kn_ref[rows, heads[h]], kr_ref[rows, :]], axis=1),
                lambda h: jnp.concatenate([vt_ref[2 * kp, heads[h], :], vt_ref[2 * kp + 1, heads[h], :]],
                                          axis=1),
                s2_ref, p2_ref, None, False)

    def body(kp, carry):
        own_pair(kp)
        return carry

    lax.fori_loop(0, qt // 2, body, 0)

    @pl.when(qt % 2 == 1)
    def _():
        own_tile(qt - 1, None)

    key_chunk = lax.broadcasted_iota(jnp.int32, (tile, LANE), 0) // CHUNK
    qry_lane = lax.broadcasted_iota(jnp.int32, (tile, LANE), 1)
    own_tile(qt, lambda q0: key_chunk <= (qry_lane + q0) // CHUNK)

    for h, hs in enumerate(heads):
        o_ref[:, hs] = (acc_ref[h] / l_ref[h]).T.astype(o_ref.dtype)


def attention_t(qn, qr, kn, krp, vt, hkn, hkrp, hvt, hlen, nseq, seq_len):
    hw = MLA_HEADS * LANE
    tile = ATT_TILE
    assert seq_len % tile == 0 and tile % CHUNK == 0
    nqt = seq_len // tile
    lh = hkn.shape[0]
    nh = MLA_HEADS
    grid_spec = pltpu.PrefetchScalarGridSpec(
        num_scalar_prefetch=1,
        grid=(nseq, nqt),
        in_specs=[pl.BlockSpec((tile, hw), lambda s, q, hl: (s * nqt + q, 0)),
                  pl.BlockSpec((tile, hw), lambda s, q, hl: (s * nqt + q, 0)),
                  pl.BlockSpec((seq_len, hw), lambda s, q, hl: (s, 0)),
                  pl.BlockSpec((seq_len, LANE), lambda s, q, hl: (s, 0)),
                  pl.BlockSpec((nqt, hw, tile), lambda s, q, hl: (s, 0, 0)),
                  pl.BlockSpec((lh, hw), lambda s, q, hl: (0, 0)),
                  pl.BlockSpec((lh, LANE), lambda s, q, hl: (0, 0)),
                  pl.BlockSpec((hw, lh), lambda s, q, hl: (0, 0))],
        out_specs=pl.BlockSpec((tile, hw), lambda s, q, hl: (s * nqt + q, 0)),
        scratch_shapes=[pltpu.VMEM((nh, 1, tile), F32), pltpu.VMEM((nh, 1, tile), F32),
                        pltpu.VMEM((nh, 1, tile), F32),
                        pltpu.VMEM((nh, V_HEAD, tile), F32),
                        pltpu.VMEM((nh, tile, 2 * LANE), BF16),
                        pltpu.VMEM((nh, tile, tile), F32),
                        pltpu.VMEM((nh, tile, tile), BF16),
                        pltpu.VMEM((nh, 2 * tile, tile), F32),
                        pltpu.VMEM((nh, 2 * tile, tile), BF16),
                        pltpu.VMEM((nh, lh, tile), F32),
                        pltpu.VMEM((nh, lh, tile), BF16)],
    )
    return pl.pallas_call(
        functools.partial(_attn_t_kernel, tile=tile, lh=lh),
        grid_spec=grid_spec,
        out_shape=jax.ShapeDtypeStruct((nseq * seq_len, hw), BF16),
        compiler_params=_cparams(("parallel", "arbitrary")),
        name="attention_t",
    )(hlen, qn, qr, kn, krp, vt, hkn, hkrp, hvt)


def _dn_prep_kernel(x_ref, sm_ref, cw_ref, hist_ref, ab_ref,
                    u_ref, w_ref, qd_ref, kd_ref, attn_ref, gl_ref,
                    ext_ref, act_ref, lhs_ref, kb_ref, dec_ref, rhs_ref, mm_ref, t_ref, xs_ref, *, c, g):
    j = pl.program_id(1)
    nh, dk, dv = DN_HEADS, DN_DK, DN_DV
    kw = nh * dk
    rows_all = c * g
    inst = [(b, h) for b in range(g) for h in range(nh)]

    @pl.when(j == 0)
    def _():
        ext_ref[0:8, :] = hist_ref[...]

    ext_ref[8:8 + rows_all, :] = x_ref[...]
    conv = cw_ref[0:1, :] * ext_ref[5:5 + rows_all, :]
    for t in range(1, DN_CONV):
        conv = conv + cw_ref[t:t + 1, :] * ext_ref[5 + t:5 + t + rows_all, :]
    ext_ref[0:8, :] = ext_ref[rows_all:rows_all + 8, :]
    act_ref[...] = conv * _sigmoid(conv)

    gates = sm_ref[:, LANE:2 * LANE]
    xa = gates + ab_ref[1:2, :]
    softplus = jnp.maximum(xa, 0.0) + jnp.log(1.0 + jnp.exp(-jnp.abs(xa)))
    g_all = -jnp.exp(ab_ref[0:1, :]) * softplus
    beta_all = _sigmoid(gates)

    r = lax.broadcasted_iota(jnp.int32, (c, c), 0)
    q = lax.broadcasted_iota(jnp.int32, (c, c), 1)
    incl = r >= q
    strict = r > q
    eye = jnp.where(r == q, 1.0, 0.0)
    tri = jnp.where(incl, 1.0, 0.0)

    for b in range(g):
        rows = slice(b * c, (b + 1) * c)
        gc = jnp.dot(tri, g_all[rows], preferred_element_type=F32, precision=lax.Precision.HIGHEST)
        if c < LANE:
            gc_sq = jnp.concatenate([gc, jnp.zeros((LANE - c, LANE), F32)], axis=0)
        else:
            gc_sq = gc
        gc_t = gc_sq.T
        egc = jnp.exp(gc)
        glast = gc[c - 1:c, :]
        edl = jnp.exp(glast - gc)
        gl_ref[rows, :] = jnp.broadcast_to(glast, (c, LANE))
        beta_b = beta_all[rows]
        for h in range(nh):
            i = b * nh + h
            hs = slice(h * dk, (h + 1) * dk)
            qh = act_ref[rows, hs]
            kh = act_ref[rows, kw + h * dk:kw + (h + 1) * dk]
            vh = act_ref[rows, 2 * kw + h * dv:2 * kw + (h + 1) * dv]
            qh = qh * (lax.rsqrt(jnp.sum(qh * qh, axis=1, keepdims=True) + EPS) * dk ** -0.5)
            kh = kh * lax.rsqrt(jnp.sum(kh * kh, axis=1, keepdims=True) + EPS)
            bcol = beta_b[:, 8 + h:9 + h]
            gcol = gc[:, h:h + 1]
            grow = gc_t[h:h + 1, :c]
            dec_ref[i] = jnp.where(incl, jnp.exp(jnp.where(incl, gcol - grow, 0.0)), 0.0)
            kb = kh * bcol
            ecol = egc[:, h:h + 1]
            lhs_ref[i] = jnp.concatenate([kb, qh], axis=0).astype(BF16)
            kb_ref[i] = kh.astype(BF16)
            rhs_ref[i] = jnp.concatenate([vh * bcol, kb * ecol], axis=1).astype(BF16)
            qd_ref[rows, hs] = (qh * ecol).astype(BF16)
            kd_ref[rows, hs] = (kh * edl[:, h:h + 1]).astype(BF16)

    for i, (b, h) in enumerate(inst):
        kq = _dot_nt(lhs_ref[i], kb_ref[i])
        dec = dec_ref[i]
        mm = jnp.where(strict, kq[:c] * dec, 0.0)
        mm_ref[i] = mm
        t_ref[i] = eye - jnp.where((r ^ q) == 1, mm, 0.0)
        attn_ref[b, h] = (kq[c:] * dec).astype(BF16)

    s = 2
    while s < c:
        sh = s.bit_length() - 1
        sel = ((r >> sh) ^ (q >> sh)) == 1
        for i in range(len(inst)):
            e = jnp.where(sel, mm_ref[i], 0.0).astype(BF16)
            xs_ref[i] = _dot(e, t_ref[i].astype(BF16)).astype(BF16)
        for i in range(len(inst)):
            t = t_ref[i]
            t_ref[i] = t - _dot(t.astype(BF16), xs_ref[i])
        s *= 2

    for i, (b, h) in enumerate(inst):
        rows = slice(b * c, (b + 1) * c)
        hs = slice(h * dk, (h + 1) * dk)
        uw = _dot(t_ref[i].astype(BF16), rhs_ref[i])
        u_ref[rows, hs] = uw[:, :dv]
        w_ref[rows, hs] = uw[:, dv:].astype(BF16)


def _dn_scan_kernel(u_ref, w_ref, qd_ref, kd_ref, attn_ref, gl_ref, z_ref, s0_ref, nw_ref,
                    y_ref, sfin_ref, s_ref, rr_ref, vn_ref, *, c, nseq, shared_s0):
    j = pl.program_id(0)
    nh, dk = DN_HEADS, DN_DK
    inst = [(s, h) for s in range(nseq) for h in range(nh)]
    heads = [slice(h * dk, (h + 1) * dk) for h in range(nh)]

    @pl.when(j == 0)
    def _():
        for s in range(nseq):
            s_ref[s] = s0_ref[0 if shared_s0 else s]

    for i, (s, h) in enumerate(inst):
        lhs = jnp.concatenate([w_ref[s, :, heads[h]], qd_ref[s, :, heads[h]]], axis=0)
        rr_ref[i] = _dot(lhs, s_ref[s, h].astype(BF16))
    for i, (s, h) in enumerate(inst):
        vn_ref[i] = (u_ref[s, :, heads[h]] - rr_ref[i, 0:c, :]).astype(BF16)
    for i, (s, h) in enumerate(inst):
        o = rr_ref[i, c:2 * c, :] + _dot(attn_ref[s, h], vn_ref[i])
        zh = z_ref[s, :, heads[h]]
        y_ref[s, :, heads[h]] = (_rms(o, nw_ref[...]) * (zh * _sigmoid(zh))).astype(y_ref.dtype)
    for i, (s, h) in enumerate(inst):
        ebd = jnp.exp(gl_ref[s, 0:1, h:h + 1])
        s_ref[s, h] = s_ref[s, h] * ebd + _dot_tn(kd_ref[s, :, heads[h]], vn_ref[i])

    @pl.when(j == pl.num_programs(0) - 1)
    def _():
        sfin_ref[...] = s_ref[...]


def deltanet(proj, cw, hist, s0, ab, nw, nseq, seq_len, c, g):
    c = min(c, seq_len)
    nblk = seq_len // c
    g = min(g, nblk)
    assert seq_len % c == 0 and c % 8 == 0 and nblk % g == 0
    ntile = nblk // g
    rows_t = c * g
    nhs = hist.shape[0]
    assert nhs in (1, nseq) and s0.shape[0] == nhs
    hidx3 = (lambda s, j: (s, 0, 0)) if nhs > 1 else (lambda s, j: (0, 0, 0))
    nh, dk, dv = DN_HEADS, DN_DK, DN_DV
    zw = nh * dv
    rows = nseq * seq_len
    ni = g * nh
    row_blk = lambda w: pl.BlockSpec((rows_t, w), lambda s, j: (s * ntile + j, 0))
    u, w, qd, kd, attn, gl = pl.pallas_call(
        functools.partial(_dn_prep_kernel, c=c, g=g),
        grid=(nseq, ntile),
        in_specs=[pl.BlockSpec((rows_t, DN_QKV), lambda s, j: (s * ntile + j, COL_QKV // DN_QKV)),
                  pl.BlockSpec((rows_t, SMALL_W), lambda s, j: (s * ntile + j, COL_SMALL // SMALL_W)),
                  pl.BlockSpec(cw.shape, lambda s, j: (0, 0)),
                  pl.BlockSpec((None, 8, DN_QKV), hidx3),
                  pl.BlockSpec(ab.shape, lambda s, j: (0, 0))],
        out_specs=[row_blk(zw), row_blk(zw), row_blk(zw), row_blk(zw),
                   pl.BlockSpec((g, nh, c, c), lambda s, j: (s * ntile + j, 0, 0, 0)),
                   row_blk(LANE)],
        out_shape=[jax.ShapeDtypeStruct((rows, zw), F32),
                   jax.ShapeDtypeStruct((rows, zw), BF16),
                   jax.ShapeDtypeStruct((rows, zw), BF16),
                   jax.ShapeDtypeStruct((rows, zw), BF16),
                   jax.ShapeDtypeStruct((nseq * nblk, nh, c, c), BF16),
                   jax.ShapeDtypeStruct((rows, LANE), F32)],
        scratch_shapes=[pltpu.VMEM((rows_t + 8, DN_QKV), F32),
                        pltpu.VMEM((rows_t, DN_QKV), F32),
                        pltpu.VMEM((ni, 2 * c, dk), BF16),
                        pltpu.VMEM((ni, c, dk), BF16),
                        pltpu.VMEM((ni, c, c), F32),
                        pltpu.VMEM((ni, c, dk + dv), BF16),
                        pltpu.VMEM((ni, c, c), F32),
                        pltpu.VMEM((ni, c, c), F32),
                        pltpu.VMEM((ni, c, c), BF16)],
        compiler_params=_cparams(("parallel", "arbitrary")),
        name="dn_prep",
    )(proj, proj, cw, hist, ab)

    seq3 = lambda a: a.reshape(nseq, seq_len, a.shape[1])
    blk3 = lambda wdt, col=0: pl.BlockSpec((nseq, c, wdt), lambda j: (0, j, col))
    y, sfin = pl.pallas_call(
        functools.partial(_dn_scan_kernel, c=c, nseq=nseq, shared_s0=nhs == 1),
        grid=(nblk,),
        in_specs=[blk3(zw), blk3(zw), blk3(zw), blk3(zw),
                  pl.BlockSpec((nseq, None, nh, c, c), lambda j: (0, j, 0, 0, 0)),
                  blk3(LANE), blk3(zw, COL_Z // zw),
                  pl.BlockSpec(s0.shape, lambda j: (0, 0, 0, 0)),
                  pl.BlockSpec(nw.shape, lambda j: (0, 0))],
        out_specs=[blk3(zw), pl.BlockSpec((nseq, nh, dk, dv), lambda j: (0, 0, 0, 0))],
        out_shape=[jax.ShapeDtypeStruct((nseq, seq_len, zw), BF16),
                   jax.ShapeDtypeStruct((nseq, nh, dk, dv), F32)],
        scratch_shapes=[pltpu.VMEM((nseq, nh, dk, dv), F32),
                        pltpu.VMEM((nseq * nh, 2 * c, dv), F32),
                        pltpu.VMEM((nseq * nh, c, dv), BF16)],
        compiler_params=_cparams(("arbitrary",)),
        name="dn_scan",
    )(seq3(u), seq3(w), seq3(qd), seq3(kd), attn.reshape(nseq, nblk, nh, c, c), seq3(gl), seq3(proj),
      s0, nw)
    return y.reshape(rows, zw), sfin


def _rope_tables(pos):
    half = QK_ROPE // 2
    inv = ROPE_THETA ** (-jnp.arange(half, dtype=F32) / half)
    ang = pos.astype(F32)[:, None] * inv[None, :]
    cos, sin = jnp.cos(ang), jnp.sin(ang)
    zeros = jnp.zeros((pos.shape[0], LANE - QK_ROPE), F32)
    cq = jnp.concatenate([cos, cos, zeros], axis=1)
    sq = jnp.concatenate([sin, sin, zeros], axis=1)
    ck = jnp.concatenate([cos, cos, sin, sin], axis=1)
    return cq, sq, ck


def _rot_cols(w):
    half = w.shape[-1] // 2
    return jnp.concatenate([-w[..., half:], w[..., :half]], axis=-1)


def _row(v, width=None):
    v = v.astype(F32).reshape(1, -1)
    if width is not None and v.shape[1] < width:
        v = jnp.pad(v, ((0, 0), (0, width - v.shape[1])))
    return v


def kernel(x_prompt, x_sample, cache_mla_ckv, cache_mla_krope, state_dn_s, state_dn_conv, state_cf_conv, meta_tokens, norm_gains, w_in, mla_gq, mla_gkv, w_uq, w_uk, w_uv, dn_conv_w, dn_a_log, dn_dt_bias, dn_norm_w, w_out, cf_w_pw1, cf_b_pw1, cf_w_dw, cf_b_dw, cf_ln_g, cf_ln_b, cf_w_pw2, cf_b_pw2, w_gate, w_up, w_down):
    bp, lp, d = x_prompt.shape
    bs, ls, _ = x_sample.shape
    n_meta = meta_tokens.shape[0]
    past = cache_mla_ckv.shape[2] - n_meta
    depth = norm_gains.shape[0]
    assert n_meta == N_META and ls == n_meta and n_meta <= CHUNK
    assert past % CHUNK == 0 and ls <= CHUNK and lp % CHUNK == 0
    ns = bs + 1
    hw = MLA_HEADS * QK_NOPE

    hp = x_prompt.reshape(bp * lp, d)
    hs = jnp.concatenate([x_sample.reshape(bs * ls, d), meta_tokens.astype(F32)], axis=0)
    meta_rows = slice(bs * ls, bs * ls + n_meta)

    pos_p = n_meta + jnp.arange(lp)
    pos_s = jnp.concatenate([jnp.tile(n_meta + past + jnp.arange(ls), bs), jnp.arange(n_meta)])
    tab_p = _rope_tables(pos_p)
    tab_s = _rope_tables(pos_s)
    zero_d = jnp.zeros((1, d), F32)
    wg_all, wu_all, wd_all = w_gate.astype(BF16), w_up.astype(BF16), w_down.astype(BF16)

    outs = {k: [] for k in ("p_ckv", "p_kr", "p_s", "p_conv", "p_cf", "s_ckv", "s_kr", "s_s", "s_conv", "s_cf")}
    for layer in range(depth):
        ng = norm_gains[layer].astype(F32)
        g0, g1, g2, g3 = (ng[i:i + 1] for i in range(4))
        if layer % 2 == 0:
            e = layer // 2
            offs = np.cumsum((Q_LORA, KV_LORA, QK_ROPE, DN_QKV, DN_HEADS * DN_DV, DN_HEADS, DN_HEADS))
            wi = w_in[e]
            w_qd, w_kvd, w_kr = wi[:, :offs[0]], wi[:, offs[0]:offs[1]], wi[:, offs[1]:offs[2]]
            w_qkv, w_z = wi[:, offs[2]:offs[3]], wi[:, offs[3]:offs[4]]
            w_a, w_b = wi[:, offs[4]:offs[5]], wi[:, offs[5]:offs[6]]
            w_proj = jnp.concatenate(
                [w_qkv, w_z, w_qd, w_kvd, w_kr, _rot_cols(w_kr), w_a, w_b,
                 jnp.zeros((d, SMALL_W - 2 * QK_ROPE - 2 * DN_HEADS), F32)], axis=1).astype(BF16)
            zero_proj = jnp.zeros((1, PROJ_W), F32)
            wq3 = w_uq[e]
            wq_n = wq3[:, :, :QK_NOPE].reshape(Q_LORA, hw)
            wq_r = wq3[:, :, QK_NOPE:]
            pad_r = lambda w: jnp.pad(w, ((0, 0), (0, 0), (0, LANE - QK_ROPE))).reshape(Q_LORA, hw)
            wq = jnp.concatenate([wq_n, pad_r(wq_r), pad_r(_rot_cols(wq_r))], axis=1).astype(BF16)
            wk = w_uk[e].reshape(KV_LORA, hw).astype(BF16)
            wv = w_uv[e].reshape(KV_LORA, hw).astype(BF16)
            wkv = jnp.concatenate([wk, wv], axis=1)
            gq, gkv = _row(mla_gq[e]), _row(mla_gkv[e])
            cw = jnp.pad(dn_conv_w[e].astype(F32), ((0, 8 - DN_CONV), (0, 0)))
            ab = jnp.concatenate([_row(dn_a_log[e], LANE), _row(dn_dt_bias[e], LANE),
                                  jnp.zeros((6, LANE), F32)], axis=0)
            nw = _row(dn_norm_w[e])
            wo = w_out[e].astype(BF16)
            wo_mla, wo_dn = wo[:hw], wo[hw:]

            proj_s = norm_matmul(hs, g0, w_proj, zero_proj, 1024, 768)
            qn_s, qr_s, ckv_s, kn_s, v_s, kro_s, krp_s = mla_prep(proj_s, gq, gkv, wq, wk, wv, *tab_s, 512, False)
            lh = n_meta + past
            lh_pad = -(-lh // LANE) * LANE
            hist_ckv = jnp.pad(cache_mla_ckv[e].astype(F32), ((0, 1), (0, lh_pad - lh), (0, 0)))
            hkn, hv = kv_up(hist_ckv.reshape(ns * lh_pad, KV_LORA), wkv, lh_pad)
            hkrp = jnp.pad(cache_mla_krope[e].astype(BF16),
                           ((0, 1), (0, lh_pad - lh), (0, LANE - QK_ROPE)))
            hlen_s = jnp.concatenate([jnp.full((bs,), lh, jnp.int32), jnp.zeros((1,), jnp.int32)])
            ymla_s = attention(qn_s, qr_s, kn_s, krp_s, v_s, hkn.reshape(ns, lh_pad, hw), hkrp,
                               hv.reshape(ns, lh_pad, hw), hlen_s, ns, ls, ls)
            conv_hist_s = jnp.pad(state_dn_conv[e].astype(F32), ((0, 1), (8 - (DN_CONV - 1), 0), (0, 0)))
            s0_s = jnp.pad(state_dn_s[e].astype(F32), ((0, 1), (0, 0), (0, 0), (0, 0)))
            ydn_s, sfin_s = deltanet(proj_s, cw, conv_hist_s, s0_s, ab, nw, ns, ls, CHUNK, 1)
            hs = matmul_resnorm([ymla_s, ydn_s], [wo_mla, wo_dn], zero_d, g1, hs, 512)

            proj_p = norm_matmul(hp, g0, w_proj, zero_proj, 1024, 768)
            qn_p, qr_p, ckv_p, kn_p, vt_p, kro_p, krp_p = mla_prep(proj_p, gq, gkv, wq, wk, wv.T, *tab_p,
                                                                   512, True)
            hlen_p = jnp.full((bp,), n_meta, jnp.int32)
            ymla_p = attention_t(qn_p, qr_p, kn_p, krp_p, vt_p, kn_s[meta_rows], krp_s[meta_rows],
                                 v_s[meta_rows].T, hlen_p, bp, lp)
            conv_hist_p = jnp.pad(proj_s[meta_rows, COL_QKV:COL_QKV + DN_QKV][-(DN_CONV - 1):],
                                  ((8 - (DN_CONV - 1), 0), (0, 0)))[None]
            ydn_p, sfin_p = deltanet(proj_p, cw, conv_hist_p, sfin_s[bs:], ab, nw, bp, lp, CHUNK, 4)
            hp = matmul_resnorm([ymla_p, ydn_p], [wo_mla, wo_dn], zero_d, g1, hp, 512)

            bc = lambda a: jnp.broadcast_to(a[None], (bp,) + a.shape)
            outs["p_ckv"].append(jnp.concatenate([bc(ckv_s[meta_rows]), ckv_p.reshape(bp, lp, KV_LORA)], axis=1))
            outs["p_kr"].append(jnp.concatenate([bc(kro_s[meta_rows]), kro_p.reshape(bp, lp, QK_ROPE)], axis=1))
            outs["p_s"].append(sfin_p)
            outs["p_conv"].append(proj_p.reshape(bp, lp, PROJ_W)[:, lp - (DN_CONV - 1):, COL_QKV:COL_QKV + DN_QKV])
            outs["s_ckv"].append(ckv_s[:bs * ls].reshape(bs, ls, KV_LORA))
            outs["s_kr"].append(kro_s[:bs * ls].reshape(bs, ls, QK_ROPE))
            outs["s_s"].append(sfin_s[:bs])
            xqkv_s = proj_s[:bs * ls, COL_QKV:COL_QKV + DN_QKV].reshape(bs, ls, DN_QKV)
            outs["s_conv"].append(jnp.concatenate([state_dn_conv[e].astype(F32), xqkv_s], axis=1)[:, -(DN_CONV - 1):])
        else:
            o = layer // 2
            w1 = cf_w_pw1[o].astype(BF16)
            b1 = _row(cf_b_pw1[o])
            wdw = jnp.broadcast_to(cf_w_dw[o].astype(F32)[:, None, :], (CF_KERNEL, 8, d))
            bdw, lg, lb = _row(cf_b_dw[o]), _row(cf_ln_g[o]), _row(cf_ln_b[o])
            w2 = cf_w_pw2[o].astype(BF16)
            b2 = _row(cf_b_pw2[o])
            keep = CF_KERNEL - 1

            hist_s = jnp.pad(state_cf_conv[o].astype(F32), ((0, 1), (CF_HALO - keep, 0), (0, 0)))
            c_s, tail_s = glu_conv(hs, g0, w1, b1, hist_s, wdw, bdw, ns, ls, 1024, 512, 256)
            hs = ln_matmul_resnorm(c_s, lg, lb, w2, b2, g1, hs, 512, 256)

            c_p, tail_p = glu_conv(hp, g0, w1, b1, tail_s[bs:], wdw, bdw, bp, lp, 1024, 512, 256)
            hp = ln_matmul_resnorm(c_p, lg, lb, w2, b2, g1, hp, 512, 256)

            outs["p_cf"].append(tail_p[:, CF_HALO - keep:])
            outs["s_cf"].append(tail_s[:bs, CF_HALO - keep:])
        hs = ffn(hs, g2, wg_all, wu_all, wd_all, g3, layer, 512, 512)
        hp = ffn(hp, g2, wg_all, wu_all, wd_all, g3, layer, 512, 512)

    y_prompt = hp.reshape(bp, lp, d)
    y_sample = hs[:bs * ls].reshape(bs, ls, d)
    st = lambda k: jnp.stack(outs[k])
    return (y_prompt, y_sample, st("p_ckv"), st("p_kr"), st("p_s"), st("p_conv"), st("p_cf"),
            st("s_ckv"), st("s_kr"), st("s_s"), st("s_conv"), st("s_cf"))
```

```python
import functools

import numpy as np
import jax
import jax.numpy as jnp
from jax import lax
from jax.experimental import pallas as pl
from jax.experimental.pallas import tpu as pltpu

F32 = jnp.float32
BF16 = jnp.bfloat16

EPS = 1e-6
NEG_INF = -1e30
CHUNK = 64
N_META = 16
MLA_HEADS = 8
Q_LORA = 512
KV_LORA = 512
QK_NOPE = 128
QK_ROPE = 64
V_HEAD = 128
ROPE_THETA = 10000.0
DN_HEADS = 8
DN_DK = 128
DN_DV = 128
DN_CONV = 4
DN_QKV = DN_HEADS * (2 * DN_DK + DN_DV)
CF_KERNEL = 31

LANE = 128
ATT_TILE = 256
VMEM_LIMIT = 56 * 1024 * 1024

PROJ_TM, PROJ_TN = 1024, 768
MLA_TM = 512
RES_TM = 512
LN_CHUNK = 256
FFN_TM, FFN_TF = 512, 512
GLU_TM, GLU_TN, GLU_CHUNK = 1024, 512, 256
DN_BLOCKS_PER_STEP = 4

COL_QKV = 0
COL_Z = DN_QKV
COL_QD = COL_Z + DN_HEADS * DN_DV
COL_KVD = COL_QD + Q_LORA
COL_SMALL = COL_KVD + KV_LORA
SMALL_W = 256
PROJ_W = COL_SMALL + SMALL_W


def _cparams(sem):
    return pltpu.CompilerParams(dimension_semantics=sem, vmem_limit_bytes=VMEM_LIMIT)


def _rms(x, g):
    return x * lax.rsqrt(jnp.mean(x * x, axis=-1, keepdims=True) + EPS) * g


def _sigmoid(x):
    return 1.0 / (1.0 + jnp.exp(-x))


def _dot(a, b):
    return jnp.dot(a, b, preferred_element_type=F32)


def _dot_nt(a, b):
    return lax.dot_general(a, b, (((1,), (1,)), ((), ())), preferred_element_type=F32)


def _dot_tn(a, b):
    return lax.dot_general(a, b, (((0,), (0,)), ((), ())), preferred_element_type=F32)


def _row_tile(m, pref):
    t = min(pref, m)
    while m % t:
        t //= 2
    return t


def _norm_mm_kernel(x_ref, g_ref, w_ref, b_ref, o_ref, xn_ref):
    @pl.when(pl.program_id(1) == 0)
    def _():
        xn_ref[...] = _rms(x_ref[...], g_ref[...]).astype(BF16)

    o_ref[...] = (_dot(xn_ref[...], w_ref[...]) + b_ref[...]).astype(o_ref.dtype)


CF_HALO = 32


def _glu_conv_kernel(x_ref, g_ref, wa_ref, wg_ref, ba_ref, bg_ref, hist_ref, cw_ref, cb_ref,
                     c_ref, tail_ref, xn_ref, ext_ref, sh_ref, carry_ref, *, tm, chunk, tiles_per_seq,
                     col_major):
    first = CF_HALO - (CF_KERNEL - 1)
    sub = 8
    span = chunk + CF_HALO - sub

    if col_major:
        i = pl.program_id(1)
        j = pl.program_id(0)
        xn_ref[...] = _rms(x_ref[...], g_ref[...]).astype(BF16)
    else:
        i = pl.program_id(0)
        j = pl.program_id(1)

        @pl.when(j == 0)
        def _():
            xn_ref[...] = _rms(x_ref[...], g_ref[...]).astype(BF16)

    seq_start = (i % tiles_per_seq) == 0

    @pl.when(seq_start)
    def _():
        ext_ref[0:CF_HALO, :] = hist_ref[...]

    @pl.when(jnp.logical_not(seq_start))
    def _():
        ext_ref[0:CF_HALO, :] = carry_ref[j]

    def glu_chunk(c):
        lo = c * chunk
        xc = xn_ref[lo:lo + chunk, :]
        a = _dot(xc, wa_ref[...]) + ba_ref[...]
        gt = _dot(xc, wg_ref[...]) + bg_ref[...]
        ext_ref[CF_HALO + lo:CF_HALO + lo + chunk, :] = a * _sigmoid(gt)

    nchunk = tm // chunk
    glu_chunk(0)
    for c in range(nchunk):
        lo = c * chunk
        if c + 1 < nchunk:
            glu_chunk(c + 1)
        for sft in range(1, sub):
            sh_ref[sft - 1, 0:span, :] = ext_ref[lo + sft:lo + sft + span, :]
        rg = min(chunk, 4 * sub)
        for r0 in range(0, chunk, rg):
            acc = None
            for t in range(CF_KERNEL):
                sft = (first + t) % sub
                base = first + t - sft + r0
                if sft == 0:
                    src = ext_ref[lo + base:lo + base + rg, :]
                else:
                    src = sh_ref[sft - 1, base:base + rg, :]
                term = src.reshape(rg // sub, sub, src.shape[1]) * cw_ref[t]
                acc = term if acc is None else acc + term
            c_ref[lo + r0:lo + r0 + rg, :] = acc.reshape(rg, acc.shape[2]) + cb_ref[...]
    tail = ext_ref[tm:tm + CF_HALO, :]
    carry_ref[j] = tail
    tail_ref[...] = tail


def norm_matmul(x, g, w, b, tm, tn, out_dtype=F32):
    m, k = x.shape
    n = w.shape[1]
    tm = _row_tile(m, tm)
    return pl.pallas_call(
        _norm_mm_kernel,
        grid=(m // tm, n // tn),
        in_specs=[pl.BlockSpec((tm, k), lambda i, j: (i, 0)),
                  pl.BlockSpec((1, k), lambda i, j: (0, 0)),
                  pl.BlockSpec((k, tn), lambda i, j: (0, j)),
                  pl.BlockSpec((1, tn), lambda i, j: (0, j))],
        out_specs=pl.BlockSpec((tm, tn), lambda i, j: (i, j)),
        out_shape=jax.ShapeDtypeStruct((m, n), out_dtype),
        scratch_shapes=[pltpu.VMEM((tm, k), BF16)],
        compiler_params=_cparams(("parallel", "arbitrary")),
        name="norm_matmul",
    )(x, g, w, b)


def glu_conv(x, g, w, b, hist, cw, cb, nseq, seq_len, tm, tn, chunk):
    m, k = x.shape
    n = w.shape[1] // 2
    tm = _row_tile(seq_len, tm)
    chunk = min(chunk, tm)
    assert tm % chunk == 0 and chunk % 8 == 0
    tiles_per_seq = seq_len // tm
    nb = n // tn
    nhs = hist.shape[0]
    col_major = tiles_per_seq == 1 and tm < 256
    if col_major:
        grid = (nb, m // tm)
        spec = lambda shape, f: pl.BlockSpec(shape, lambda j, i: f(i, j))
    else:
        grid = (m // tm, nb)
        spec = lambda shape, f: pl.BlockSpec(shape, f)
    hidx = ((lambda i, j: (i // tiles_per_seq, 0, j)) if nhs > 1 else (lambda i, j: (0, 0, j)))
    c, tails = pl.pallas_call(
        functools.partial(_glu_conv_kernel, tm=tm, chunk=chunk, tiles_per_seq=tiles_per_seq,
                          col_major=col_major),
        grid=grid,
        in_specs=[spec((tm, k), lambda i, j: (i, 0)),
                  spec((1, k), lambda i, j: (0, 0)),
                  spec((k, tn), lambda i, j: (0, j)),
                  spec((k, tn), lambda i, j: (0, j + nb)),
                  spec((1, tn), lambda i, j: (0, j)),
                  spec((1, tn), lambda i, j: (0, j + nb)),
                  spec((None, CF_HALO, tn), hidx),
                  spec((cw.shape[0], 8, tn), lambda i, j: (0, 0, j)),
                  spec((1, tn), lambda i, j: (0, j))],
        out_specs=[spec((tm, tn), lambda i, j: (i, j)),
                   spec((None, CF_HALO, tn), lambda i, j: (i, 0, j))],
        out_shape=[jax.ShapeDtypeStruct((m, n), F32),
                   jax.ShapeDtypeStruct((m // tm, CF_HALO, n), F32)],
        scratch_shapes=[pltpu.VMEM((tm, k), BF16),
                        pltpu.VMEM((tm + CF_HALO, tn), F32),
                        pltpu.VMEM((7, chunk + CF_HALO - 8, tn), F32),
                        pltpu.VMEM((nb, CF_HALO, tn), F32)],
        compiler_params=_cparams(("arbitrary", "arbitrary")),
        name="glu_conv",
    )(x, g, w, w, b, b, hist, cw, cb)
    return c, tails.reshape(nseq, tiles_per_seq, CF_HALO, n)[:, -1]


def _mm_resnorm_kernel(*refs, n_in):
    xs = refs[:n_in]
    ws = refs[n_in:2 * n_in]
    b_ref, g_ref, res_ref, o_ref = refs[2 * n_in:]
    y = b_ref[...]
    for x_ref, w_ref in zip(xs, ws):
        y = y + _dot(x_ref[...], w_ref[...])
    o_ref[...] = res_ref[...] + _rms(y, g_ref[...])


def _ln_mm_resnorm_kernel(c_ref, lg_ref, lb_ref, w_ref, b_ref, g_ref, res_ref, o_ref, *, tm, chunk):
    for c in range(tm // chunk):
        rows = slice(c * chunk, (c + 1) * chunk)
        x = c_ref[rows, :]
        xc = x - jnp.mean(x, axis=-1, keepdims=True)
        y = xc * lax.rsqrt(jnp.mean(xc * xc, axis=-1, keepdims=True) + EPS) * lg_ref[...] + lb_ref[...]
        a = (y * _sigmoid(y)).astype(BF16)
        z = _dot(a, w_ref[...]) + b_ref[...]
        o_ref[rows, :] = res_ref[rows, :] + _rms(z, g_ref[...])


def ln_matmul_resnorm(c, lg, lb, w, b, g, res, tm, chunk):
    m, n = res.shape
    k = c.shape[1]
    tm = _row_tile(m, tm)
    chunk = min(chunk, tm)
    assert tm % chunk == 0
    vec = lambda width: pl.BlockSpec((1, width), lambda i: (0, 0))
    return pl.pallas_call(
        functools.partial(_ln_mm_resnorm_kernel, tm=tm, chunk=chunk),
        grid=(m // tm,),
        in_specs=[pl.BlockSpec((tm, k), lambda i: (i, 0)), vec(k), vec(k),
                  pl.BlockSpec(w.shape, lambda i: (0, 0)), vec(n), vec(n),
                  pl.BlockSpec((tm, n), lambda i: (i, 0))],
        out_specs=pl.BlockSpec((tm, n), lambda i: (i, 0)),
        out_shape=jax.ShapeDtypeStruct((m, n), F32),
        compiler_params=_cparams(("parallel",)),
        name="ln_matmul_resnorm",
    )(c, lg, lb, w, b, g, res)


def matmul_resnorm(xs, ws, b, g, res, tm):
    m, n = res.shape
    tm = _row_tile(m, tm)
    n_in = len(xs)
    in_specs = ([pl.BlockSpec((tm, x.shape[1]), lambda i: (i, 0)) for x in xs]
                + [pl.BlockSpec(w.shape, lambda i: (0, 0)) for w in ws]
                + [pl.BlockSpec((1, n), lambda i: (0, 0)),
                   pl.BlockSpec((1, n), lambda i: (0, 0)),
                   pl.BlockSpec((tm, n), lambda i: (i, 0))])
    return pl.pallas_call(
        functools.partial(_mm_resnorm_kernel, n_in=n_in),
        grid=(m // tm,),
        in_specs=in_specs,
        out_specs=pl.BlockSpec((tm, n), lambda i: (i, 0)),
        out_shape=jax.ShapeDtypeStruct((m, n), F32),
        compiler_params=_cparams(("parallel",)),
        name="matmul_resnorm",
    )(*xs, *ws, b, g, res)


def _ffn_kernel(h_ref, g2_ref, wg_ref, wu_ref, wd_ref, g3_ref, o_ref, xn_ref, acc_ref):
    j = pl.program_id(1)

    @pl.when(j == 0)
    def _():
        xn_ref[...] = _rms(h_ref[...], g2_ref[...]).astype(BF16)
        acc_ref[...] = jnp.zeros_like(acc_ref)

    xn = xn_ref[...]
    gate = _dot(xn, wg_ref[...])
    up = _dot(xn, wu_ref[...])
    a = (gate * _sigmoid(gate) * up).astype(BF16)
    acc_ref[...] += _dot(a, wd_ref[...])

    @pl.when(j == pl.num_programs(1) - 1)
    def _():
        o_ref[...] = h_ref[...] + _rms(acc_ref[...], g3_ref[...])


def ffn(h, g2, wg, wu, wd, g3, layer, tm, tf):
    m, d = h.shape
    f = wg.shape[2]
    tm = _row_tile(m, tm)
    return pl.pallas_call(
        _ffn_kernel,
        grid=(m // tm, f // tf),
        in_specs=[pl.BlockSpec((tm, d), lambda i, j: (i, 0)),
                  pl.BlockSpec((1, d), lambda i, j: (0, 0)),
                  pl.BlockSpec((None, d, tf), lambda i, j: (layer, 0, j)),
                  pl.BlockSpec((None, d, tf), lambda i, j: (layer, 0, j)),
                  pl.BlockSpec((None, tf, d), lambda i, j: (layer, j, 0)),
                  pl.BlockSpec((1, d), lambda i, j: (0, 0))],
        out_specs=pl.BlockSpec((tm, d), lambda i, j: (i, 0)),
        out_shape=jax.ShapeDtypeStruct((m, d), F32),
        scratch_shapes=[pltpu.VMEM((tm, d), BF16), pltpu.VMEM((tm, d), F32)],
        compiler_params=_cparams(("parallel", "arbitrary")),
        name="ffn",
    )(h, g2, wg, wu, wd, g3)


def _mla_prep_kernel(qd_ref, kvd_ref, sm_ref, gq_ref, gkv_ref, wq_ref, wk_ref, wv_ref, cq_ref, sq_ref, ck_ref,
                     qn_ref, qr_ref, ckv_ref, kn_ref, v_ref, kro_ref, krp_ref, *, v_transposed):
    hw = MLA_HEADS * QK_NOPE
    scale = (QK_NOPE + QK_ROPE) ** -0.5
    cq = _rms(qd_ref[...], gq_ref[...]).astype(BF16)
    q = _dot(cq, wq_ref[...])
    qn_ref[...] = (q[:, :hw] * scale).astype(BF16)
    cos8 = jnp.tile(cq_ref[...], (1, MLA_HEADS))
    sin8 = jnp.tile(sq_ref[...], (1, MLA_HEADS))
    qr_ref[...] = ((q[:, hw:2 * hw] * cos8 + q[:, 2 * hw:] * sin8) * scale).astype(BF16)
    ckv = _rms(kvd_ref[...], gkv_ref[...])
    ckv_ref[...] = ckv
    ckv_b = ckv.astype(BF16)
    kn_ref[...] = _dot(ckv_b, wk_ref[...]).astype(BF16)
    if v_transposed:
        vt = _dot_nt(wv_ref[...], ckv_b).astype(BF16)
        for t in range(v_ref.shape[0]):
            v_ref[t] = vt[:, t * ATT_TILE:(t + 1) * ATT_TILE]
    else:
        v_ref[...] = _dot(ckv_b, wv_ref[...]).astype(BF16)
    y = sm_ref[:, :LANE] * ck_ref[...]
    kro = y + pltpu.roll(y, QK_ROPE, 1)
    kro_ref[...] = kro[:, :QK_ROPE]
    lane = lax.broadcasted_iota(jnp.int32, kro.shape, 1)
    krp_ref[...] = jnp.where(lane < QK_ROPE, kro, 0.0).astype(BF16)


def mla_prep(proj, gq, gkv, wq, wk, wv, cq_tab, sq_tab, ck_tab, tm, v_transposed):
    m = proj.shape[0]
    tm = _row_tile(min(m, cq_tab.shape[0]), tm)
    nt = cq_tab.shape[0] // tm
    hw = MLA_HEADS * QK_NOPE
    tab = lambda: pl.BlockSpec((tm, LANE), lambda i: (i % nt, 0))
    full = lambda a: pl.BlockSpec(a.shape, lambda i: (0, 0))
    if v_transposed:
        assert tm % ATT_TILE == 0
        v_spec = pl.BlockSpec((tm // ATT_TILE, hw, ATT_TILE), lambda i: (i, 0, 0))
        v_shape = jax.ShapeDtypeStruct((m // ATT_TILE, hw, ATT_TILE), BF16)
    else:
        v_spec = pl.BlockSpec((tm, hw), lambda i: (i, 0))
        v_shape = jax.ShapeDtypeStruct((m, hw), BF16)
    return pl.pallas_call(
        functools.partial(_mla_prep_kernel, v_transposed=v_transposed),
        grid=(m // tm,),
        in_specs=[pl.BlockSpec((tm, Q_LORA), lambda i: (i, COL_QD // Q_LORA)),
                  pl.BlockSpec((tm, KV_LORA), lambda i: (i, COL_KVD // KV_LORA)),
                  pl.BlockSpec((tm, SMALL_W), lambda i: (i, COL_SMALL // SMALL_W)),
                  full(gq), full(gkv), full(wq), full(wk), full(wv), tab(), tab(), tab()],
        out_specs=[pl.BlockSpec((tm, hw), lambda i: (i, 0)),
                   pl.BlockSpec((tm, hw), lambda i: (i, 0)),
                   pl.BlockSpec((tm, KV_LORA), lambda i: (i, 0)),
                   pl.BlockSpec((tm, hw), lambda i: (i, 0)),
                   v_spec,
                   pl.BlockSpec((tm, QK_ROPE), lambda i: (i, 0)),
                   pl.BlockSpec((tm, LANE), lambda i: (i, 0))],
        out_shape=[jax.ShapeDtypeStruct((m, hw), BF16),
                   jax.ShapeDtypeStruct((m, hw), BF16),
                   jax.ShapeDtypeStruct((m, KV_LORA), F32),
                   jax.ShapeDtypeStruct((m, hw), BF16),
                   v_shape,
                   jax.ShapeDtypeStruct((m, QK_ROPE), F32),
                   jax.ShapeDtypeStruct((m, LANE), BF16)],
        compiler_params=_cparams(("parallel",)),
        name="mla_prep",
    )(proj, proj, proj, gq, gkv, wq, wk, wv, cq_tab, sq_tab, ck_tab)


def _kv_up_kernel(ckv_ref, wkv_ref, kn_ref, v_ref):
    hw = MLA_HEADS * QK_NOPE
    kv = _dot(ckv_ref[...].astype(BF16), wkv_ref[...])
    kn_ref[...] = kv[:, :hw].astype(BF16)
    v_ref[...] = kv[:, hw:].astype(BF16)


def kv_up(ckv, wkv, tm):
    m = ckv.shape[0]
    tm = _row_tile(m, tm)
    hw = MLA_HEADS * QK_NOPE
    return pl.pallas_call(
        _kv_up_kernel,
        grid=(m // tm,),
        in_specs=[pl.BlockSpec((tm, KV_LORA), lambda i: (i, 0)),
                  pl.BlockSpec(wkv.shape, lambda i: (0, 0))],
        out_specs=[pl.BlockSpec((tm, hw), lambda i: (i, 0)),
                   pl.BlockSpec((tm, hw), lambda i: (i, 0))],
        out_shape=[jax.ShapeDtypeStruct((m, hw), BF16), jax.ShapeDtypeStruct((m, hw), BF16)],
        compiler_params=_cparams(("parallel",)),
        name="kv_up",
    )(ckv, wkv)


def _attn_kernel(hlen_ref, qn_ref, qr_ref, kn_ref, kr_ref, v_ref, hkn_ref, hkr_ref, hv_ref, o_ref,
                 m_ref, l_ref, acc_ref, a_ref, qc_ref, s_ref, p_ref, hs_ref, hp_ref, *, tile, lh):
    s_idx = pl.program_id(0)
    qt = pl.program_id(1)
    hlen = hlen_ref[s_idx]
    nh = MLA_HEADS
    heads = [slice(h * LANE, (h + 1) * LANE) for h in range(nh)]

    for h, hs in enumerate(heads):
        qc_ref[h] = jnp.concatenate([qn_ref[:, hs], qr_ref[:, hs]], axis=1)

    def process(get_k, get_v, s_scr, p_scr, valid, first):
        for h in range(nh):
            s_scr[h] = _dot_nt(qc_ref[h], get_k(h))
        for h, hs in enumerate(heads):
            s = s_scr[h]
            if valid is not None:
                s = jnp.where(valid, s, NEG_INF)
            m_cur = jnp.max(s, axis=1, keepdims=True)
            if first:
                m_new = jnp.broadcast_to(m_cur, (tile, LANE))
            else:
                m_prev = m_ref[:, hs]
                m_new = jnp.maximum(m_prev, m_cur)
            p = jnp.exp(s - m_new[:, :1])
            if valid is not None:
                p = jnp.where(valid, p, 0.0)
            p_scr[h] = p.astype(BF16)
            l_cur = jnp.sum(p, axis=1, keepdims=True)
            if first:
                l_ref[:, hs] = jnp.broadcast_to(l_cur, (tile, LANE))
            else:
                alpha = jnp.exp(m_prev - m_new)
                a_ref[:, hs] = alpha
                l_ref[:, hs] = alpha * l_ref[:, hs] + l_cur
            m_ref[:, hs] = m_new
        for h, hs in enumerate(heads):
            pv = _dot(p_scr[h], get_v(h))
            if first:
                acc_ref[:, hs] = pv
            else:
                acc_ref[:, hs] = a_ref[:, hs] * acc_ref[:, hs] + pv

    hvalid = lax.broadcasted_iota(jnp.int32, (tile, lh), 1) < hlen
    process(lambda h: jnp.concatenate([hkn_ref[:, heads[h]], hkr_ref[...]], axis=1),
            lambda h: hv_ref[:, heads[h]], hs_ref, hp_ref, hvalid, True)

    def own_tile(kt, valid):
        rows = pl.ds(pl.multiple_of(kt * tile, tile), tile)
        process(lambda h: jnp.concatenate([kn_ref[rows, heads[h]], kr_ref[rows, :]], axis=1),
                lambda h: v_ref[rows, heads[h]], s_ref, p_ref, valid, False)

    def body(kt, carry):
        own_tile(kt, None)
        return carry

    lax.fori_loop(0, qt, body, 0)
    if tile > CHUNK:
        row = lax.broadcasted_iota(jnp.int32, (tile, tile), 0) // CHUNK
        col = lax.broadcasted_iota(jnp.int32, (tile, tile), 1) // CHUNK
        dvalid = col <= row
    else:
        dvalid = None
    own_tile(qt, dvalid)

    for hs in heads:
        o_ref[:, hs] = (acc_ref[:, hs] / l_ref[:, hs]).astype(o_ref.dtype)


def attention(qn, qr, kn, krp, v, hkn, hkrp, hv, hlen, nseq, seq_len, tile):
    hw = MLA_HEADS * LANE
    tile = min(tile, seq_len)
    assert seq_len % tile == 0 and (tile % CHUNK == 0 or seq_len == tile <= CHUNK)
    nqt = seq_len // tile
    nhs, lh = hkn.shape[0], hkn.shape[1]
    assert nhs in (1, nseq)
    hidx = (lambda s, q, hl: (s, 0, 0)) if nhs > 1 else (lambda s, q, hl: (0, 0, 0))
    grid_spec = pltpu.PrefetchScalarGridSpec(
        num_scalar_prefetch=1,
        grid=(nseq, nqt),
        in_specs=[pl.BlockSpec((tile, hw), lambda s, q, hl: (s * nqt + q, 0)),
                  pl.BlockSpec((tile, hw), lambda s, q, hl: (s * nqt + q, 0)),
                  pl.BlockSpec((seq_len, hw), lambda s, q, hl: (s, 0)),
                  pl.BlockSpec((seq_len, LANE), lambda s, q, hl: (s, 0)),
                  pl.BlockSpec((seq_len, hw), lambda s, q, hl: (s, 0)),
                  pl.BlockSpec((None, lh, hw), hidx),
                  pl.BlockSpec((None, lh, LANE), hidx),
                  pl.BlockSpec((None, lh, hw), hidx)],
        out_specs=pl.BlockSpec((tile, hw), lambda s, q, hl: (s * nqt + q, 0)),
        scratch_shapes=[pltpu.VMEM((tile, hw), F32), pltpu.VMEM((tile, hw), F32),
                        pltpu.VMEM((tile, hw), F32), pltpu.VMEM((tile, hw), F32),
                        pltpu.VMEM((MLA_HEADS, tile, 2 * LANE), BF16),
                        pltpu.VMEM((MLA_HEADS, tile, tile), F32),
                        pltpu.VMEM((MLA_HEADS, tile, tile), BF16),
                        pltpu.VMEM((MLA_HEADS, tile, lh), F32),
                        pltpu.VMEM((MLA_HEADS, tile, lh), BF16)],
    )
    return pl.pallas_call(
        functools.partial(_attn_kernel, tile=tile, lh=lh),
        grid_spec=grid_spec,
        out_shape=jax.ShapeDtypeStruct((nseq * seq_len, hw), BF16),
        compiler_params=_cparams(("parallel", "arbitrary")),
        name="attention",
    )(hlen, qn, qr, kn, krp, v, hkn, hkrp, hv)


def _attn_t_kernel(hlen_ref, qn_ref, qr_ref, kn_ref, kr_ref, vt_ref, hkn_ref, hkr_ref, hvt_ref, o_ref,
                   m_ref, l_ref, a_ref, acc_ref, qc_ref, s_ref, p_ref, s2_ref, p2_ref, hs_ref, hp_ref,
                   *, tile, lh):
    s_idx = pl.program_id(0)
    qt = pl.program_id(1)
    hlen = hlen_ref[s_idx]
    nh = MLA_HEADS
    heads = [slice(h * LANE, (h + 1) * LANE) for h in range(nh)]

    for h, hs in enumerate(heads):
        qc_ref[h] = jnp.concatenate([qn_ref[:, hs], qr_ref[:, hs]], axis=1)

    def process(get_k, get_vt, s_scr, p_scr, valid, first):
        for h in range(nh):
            s_scr[h] = _dot_nt(get_k(h), qc_ref[h])
        for h in range(nh):
            for q0 in range(0, tile, LANE):
                qs = slice(q0, q0 + LANE)
                s = s_scr[h, :, qs]
                ok = None if valid is None else valid(q0)
                if ok is not None:
                    s = jnp.where(ok, s, NEG_INF)
                m_cur = jnp.max(s, axis=0, keepdims=True)
                if first:
                    m_new = m_cur
                else:
                    m_prev = m_ref[h, :, qs]
                    m_new = jnp.maximum(m_prev, m_cur)
                p = jnp.exp(s - m_new)
                if ok is not None:
                    p = jnp.where(ok, p, 0.0)
                p_scr[h, :, qs] = p.astype(BF16)
                l_cur = jnp.sum(p, axis=0, keepdims=True)
                if first:
                    l_ref[h, :, qs] = l_cur
                else:
                    alpha = jnp.exp(m_prev - m_new)
                    a_ref[h, :, qs] = alpha
                    l_ref[h, :, qs] = alpha * l_ref[h, :, qs] + l_cur
                m_ref[h, :, qs] = m_new
        for h in range(nh):
            pv = _dot(get_vt(h), p_scr[h])
            if first:
                acc_ref[h] = pv
            else:
                acc_ref[h] = a_ref[h] * acc_ref[h] + pv

    hvalid = lax.broadcasted_iota(jnp.int32, (lh, LANE), 0) < hlen
    process(lambda h: jnp.concatenate([hkn_ref[:, heads[h]], hkr_ref[...]], axis=1),
            lambda h: hvt_ref[heads[h], :], hs_ref, hp_ref, lambda q0: hvalid, True)

    def own_tile(kt, valid):
        rows = pl.ds(pl.multiple_of(kt * tile, tile), tile)
        process(lambda h: jnp.concatenate([kn_ref[rows, heads[h]], kr_ref[rows, :]], axis=1),
                lambda h: vt_ref[kt, heads[h], :], s_ref, p_ref, valid, False)

    def own_pair(kp):
        rows = pl.ds(pl.multiple_of(kp * (2 * tile), 2 * tile), 2 * tile)
        process(lambda h: jnp.concatenate([kn_ref[rows, heads[h]], kr_ref[rows, :]], axis=1),
                lambda h: jnp.concatenate([vt_ref[2 * kp, heads[h], :], vt_ref[2 * kp + 1, heads[h], :]],
                                          axis=1),
                s2_ref, p2_ref, None, False)

    def body(kp, carry):
        own_pair(kp)
        return carry

    lax.fori_loop(0, qt // 2, body, 0)

    @pl.when(qt % 2 == 1)
    def _():
        own_tile(qt - 1, None)

    key_chunk = lax.broadcasted_iota(jnp.int32, (tile, LANE), 0) // CHUNK
    qry_lane = lax.broadcasted_iota(jnp.int32, (tile, LANE), 1)
    own_tile(qt, lambda q0: key_chunk <= (qry_lane + q0) // CHUNK)

    for h, hs in enumerate(heads):
        o_ref[:, hs] = (acc_ref[h] / l_ref[h]).T.astype(o_ref.dtype)


def attention_t(qn, qr, kn, krp, vt, hkn, hkrp, hvt, hlen, nseq, seq_len):
    hw = MLA_HEADS * LANE
    tile = ATT_TILE
    assert seq_len % tile == 0 and tile % CHUNK == 0
    nqt = seq_len // tile
    lh = hkn.shape[0]
    nh = MLA_HEADS
    grid_spec = pltpu.PrefetchScalarGridSpec(
        num_scalar_prefetch=1,
        grid=(nseq, nqt),
        in_specs=[pl.BlockSpec((tile, hw), lambda s, q, hl: (s * nqt + q, 0)),
                  pl.BlockSpec((tile, hw), lambda s, q, hl: (s * nqt + q, 0)),
                  pl.BlockSpec((seq_len, hw), lambda s, q, hl: (s, 0)),
                  pl.BlockSpec((seq_len, LANE), lambda s, q, hl: (s, 0)),
                  pl.BlockSpec((nqt, hw, tile), lambda s, q, hl: (s, 0, 0)),
                  pl.BlockSpec((lh, hw), lambda s, q, hl: (0, 0)),
                  pl.BlockSpec((lh, LANE), lambda s, q, hl: (0, 0)),
                  pl.BlockSpec((hw, lh), lambda s, q, hl: (0, 0))],
        out_specs=pl.BlockSpec((tile, hw), lambda s, q, hl: (s * nqt + q, 0)),
        scratch_shapes=[pltpu.VMEM((nh, 1, tile), F32), pltpu.VMEM((nh, 1, tile), F32),
                        pltpu.VMEM((nh, 1, tile), F32),
                        pltpu.VMEM((nh, V_HEAD, tile), F32),
                        pltpu.VMEM((nh, tile, 2 * LANE), BF16),
                        pltpu.VMEM((nh, tile, tile), F32),
                        pltpu.VMEM((nh, tile, tile), BF16),
                        pltpu.VMEM((nh, 2 * tile, tile), F32),
                        pltpu.VMEM((nh, 2 * tile, tile), BF16),
                        pltpu.VMEM((nh, lh, tile), F32),
                        pltpu.VMEM((nh, lh, tile), BF16)],
    )
    return pl.pallas_call(
        functools.partial(_attn_t_kernel, tile=tile, lh=lh),
        grid_spec=grid_spec,
        out_shape=jax.ShapeDtypeStruct((nseq * seq_len, hw), BF16),
        compiler_params=_cparams(("parallel", "arbitrary")),
        name="attention_t",
    )(hlen, qn, qr, kn, krp, vt, hkn, hkrp, hvt)


def _dn_prep_kernel(x_ref, sm_ref, cw_ref, hist_ref, ab_ref,
                    u_ref, w_ref, qd_ref, kd_ref, attn_ref, gl_ref,
                    ext_ref, act_ref, lhs_ref, kb_ref, dec_ref, rhs_ref, mm_ref, t_ref, xs_ref, *, c, g):
    j = pl.program_id(1)
    nh, dk, dv = DN_HEADS, DN_DK, DN_DV
    kw = nh * dk
    rows_all = c * g
    inst = [(b, h) for b in range(g) for h in range(nh)]

    @pl.when(j == 0)
    def _():
        ext_ref[0:8, :] = hist_ref[...]

    ext_ref[8:8 + rows_all, :] = x_ref[...]
    conv = cw_ref[0:1, :] * ext_ref[5:5 + rows_all, :]
    for t in range(1, DN_CONV):
        conv = conv + cw_ref[t:t + 1, :] * ext_ref[5 + t:5 + t + rows_all, :]
    ext_ref[0:8, :] = ext_ref[rows_all:rows_all + 8, :]
    act_ref[...] = conv * _sigmoid(conv)

    gates = sm_ref[:, LANE:2 * LANE]
    xa = gates + ab_ref[1:2, :]
    softplus = jnp.maximum(xa, 0.0) + jnp.log(1.0 + jnp.exp(-jnp.abs(xa)))
    g_all = -jnp.exp(ab_ref[0:1, :]) * softplus
    beta_all = _sigmoid(gates)

    r = lax.broadcasted_iota(jnp.int32, (c, c), 0)
    q = lax.broadcasted_iota(jnp.int32, (c, c), 1)
    incl = r >= q
    strict = r > q
    eye = jnp.where(r == q, 1.0, 0.0)
    tri = jnp.where(incl, 1.0, 0.0)

    for b in range(g):
        rows = slice(b * c, (b + 1) * c)
        gc = jnp.dot(tri, g_all[rows], preferred_element_type=F32, precision=lax.Precision.HIGHEST)
        if c < LANE:
            gc_sq = jnp.concatenate([gc, jnp.zeros((LANE - c, LANE), F32)], axis=0)
        else:
            gc_sq = gc
        gc_t = gc_sq.T
        egc = jnp.exp(gc)
        glast = gc[c - 1:c, :]
        edl = jnp.exp(glast - gc)
        gl_ref[rows, :] = jnp.broadcast_to(glast, (c, LANE))
        beta_b = beta_all[rows]
        for h in range(nh):
            i = b * nh + h
            hs = slice(h * dk, (h + 1) * dk)
            qh = act_ref[rows, hs]
            kh = act_ref[rows, kw + h * dk:kw + (h + 1) * dk]
            vh = act_ref[rows, 2 * kw + h * dv:2 * kw + (h + 1) * dv]
            qh = qh * (lax.rsqrt(jnp.sum(qh * qh, axis=1, keepdims=True) + EPS) * dk ** -0.5)
            kh = kh * lax.rsqrt(jnp.sum(kh * kh, axis=1, keepdims=True) + EPS)
            bcol = beta_b[:, 8 + h:9 + h]
            gcol = gc[:, h:h + 1]
            grow = gc_t[h:h + 1, :c]
            dec_ref[i] = jnp.where(incl, jnp.exp(jnp.where(incl, gcol - grow, 0.0)), 0.0)
            kb = kh * bcol
            ecol = egc[:, h:h + 1]
            lhs_ref[i] = jnp.concatenate([kb, qh], axis=0).astype(BF16)
            kb_ref[i] = kh.astype(BF16)
            rhs_ref[i] = jnp.concatenate([vh * bcol, kb * ecol], axis=1).astype(BF16)
            qd_ref[rows, hs] = (qh * ecol).astype(BF16)
            kd_ref[rows, hs] = (kh * edl[:, h:h + 1]).astype(BF16)

    for i, (b, h) in enumerate(inst):
        kq = _dot_nt(lhs_ref[i], kb_ref[i])
        dec = dec_ref[i]
        mm = jnp.where(strict, kq[:c] * dec, 0.0)
        mm_ref[i] = mm
        t_ref[i] = eye - jnp.where((r ^ q) == 1, mm, 0.0)
        attn_ref[b, h] = (kq[c:] * dec).astype(BF16)

    s = 2
    while s < c:
        sh = s.bit_length() - 1
        sel = ((r >> sh) ^ (q >> sh)) == 1
        for i in range(len(inst)):
            e = jnp.where(sel, mm_ref[i], 0.0).astype(BF16)
            xs_ref[i] = _dot(e, t_ref[i].astype(BF16)).astype(BF16)
        for i in range(len(inst)):
            t = t_ref[i]
            t_ref[i] = t - _dot(t.astype(BF16), xs_ref[i])
        s *= 2

    for i, (b, h) in enumerate(inst):
        rows = slice(b * c, (b + 1) * c)
        hs = slice(h * dk, (h + 1) * dk)
        uw = _dot(t_ref[i].astype(BF16), rhs_ref[i])
        u_ref[rows, hs] = uw[:, :dv]
        w_ref[rows, hs] = uw[:, dv:].astype(BF16)


def _dn_scan_kernel(u_ref, w_ref, qd_ref, kd_ref, attn_ref, gl_ref, z_ref, s0_ref, nw_ref,
                    y_ref, sfin_ref, s_ref, rr_ref, vn_ref, *, c, nseq, shared_s0):
    j = pl.program_id(0)
    nh, dk = DN_HEADS, DN_DK
    inst = [(s, h) for s in range(nseq) for h in range(nh)]
    heads = [slice(h * dk, (h + 1) * dk) for h in range(nh)]

    @pl.when(j == 0)
    def _():
        for s in range(nseq):
            s_ref[s] = s0_ref[0 if shared_s0 else s]

    for i, (s, h) in enumerate(inst):
        lhs = jnp.concatenate([w_ref[s, :, heads[h]], qd_ref[s, :, heads[h]]], axis=0)
        rr_ref[i] = _dot(lhs, s_ref[s, h].astype(BF16))
    for i, (s, h) in enumerate(inst):
        vn_ref[i] = (u_ref[s, :, heads[h]] - rr_ref[i, 0:c, :]).astype(BF16)
    for i, (s, h) in enumerate(inst):
        o = rr_ref[i, c:2 * c, :] + _dot(attn_ref[s, h], vn_ref[i])
        zh = z_ref[s, :, heads[h]]
        y_ref[s, :, heads[h]] = (_rms(o, nw_ref[...]) * (zh * _sigmoid(zh))).astype(y_ref.dtype)
    for i, (s, h) in enumerate(inst):
        ebd = jnp.exp(gl_ref[s, 0:1, h:h + 1])
        s_ref[s, h] = s_ref[s, h] * ebd + _dot_tn(kd_ref[s, :, heads[h]], vn_ref[i])

    @pl.when(j == pl.num_programs(0) - 1)
    def _():
        sfin_ref[...] = s_ref[...]


def deltanet(proj, cw, hist, s0, ab, nw, nseq, seq_len, c, g):
    c = min(c, seq_len)
    nblk = seq_len // c
    g = min(g, nblk)
    assert seq_len % c == 0 and c % 8 == 0 and nblk % g == 0
    ntile = nblk // g
    rows_t = c * g
    nhs = hist.shape[0]
    assert nhs in (1, nseq) and s0.shape[0] == nhs
    hidx3 = (lambda s, j: (s, 0, 0)) if nhs > 1 else (lambda s, j: (0, 0, 0))
    nh, dk, dv = DN_HEADS, DN_DK, DN_DV
    zw = nh * dv
    rows = nseq * seq_len
    ni = g * nh
    row_blk = lambda w: pl.BlockSpec((rows_t, w), lambda s, j: (s * ntile + j, 0))
    u, w, qd, kd, attn, gl = pl.pallas_call(
        functools.partial(_dn_prep_kernel, c=c, g=g),
        grid=(nseq, ntile),
        in_specs=[pl.BlockSpec((rows_t, DN_QKV), lambda s, j: (s * ntile + j, COL_QKV // DN_QKV)),
                  pl.BlockSpec((rows_t, SMALL_W), lambda s, j: (s * ntile + j, COL_SMALL // SMALL_W)),
                  pl.BlockSpec(cw.shape, lambda s, j: (0, 0)),
                  pl.BlockSpec((None, 8, DN_QKV), hidx3),
                  pl.BlockSpec(ab.shape, lambda s, j: (0, 0))],
        out_specs=[row_blk(zw), row_blk(zw), row_blk(zw), row_blk(zw),
                   pl.BlockSpec((g, nh, c, c), lambda s, j: (s * ntile + j, 0, 0, 0)),
                   row_blk(LANE)],
        out_shape=[jax.ShapeDtypeStruct((rows, zw), F32),
                   jax.ShapeDtypeStruct((rows, zw), BF16),
                   jax.ShapeDtypeStruct((rows, zw), BF16),
                   jax.ShapeDtypeStruct((rows, zw), BF16),
                   jax.ShapeDtypeStruct((nseq * nblk, nh, c, c), BF16),
                   jax.ShapeDtypeStruct((rows, LANE), F32)],
        scratch_shapes=[pltpu.VMEM((rows_t + 8, DN_QKV), F32),
                        pltpu.VMEM((rows_t, DN_QKV), F32),
                        pltpu.VMEM((ni, 2 * c, dk), BF16),
                        pltpu.VMEM((ni, c, dk), BF16),
                        pltpu.VMEM((ni, c, c), F32),
                        pltpu.VMEM((ni, c, dk + dv), BF16),
                        pltpu.VMEM((ni, c, c), F32),
                        pltpu.VMEM((ni, c, c), F32),
                        pltpu.VMEM((ni, c, c), BF16)],
        compiler_params=_cparams(("parallel", "arbitrary")),
        name="dn_prep",
    )(proj, proj, cw, hist, ab)

    seq3 = lambda a: a.reshape(nseq, seq_len, a.shape[1])
    blk3 = lambda wdt, col=0: pl.BlockSpec((nseq, c, wdt), lambda j: (0, j, col))
    y, sfin = pl.pallas_call(
        functools.partial(_dn_scan_kernel, c=c, nseq=nseq, shared_s0=nhs == 1),
        grid=(nblk,),
        in_specs=[blk3(zw), blk3(zw), blk3(zw), blk3(zw),
                  pl.BlockSpec((nseq, None, nh, c, c), lambda j: (0, j, 0, 0, 0)),
                  blk3(LANE), blk3(zw, COL_Z // zw),
                  pl.BlockSpec(s0.shape, lambda j: (0, 0, 0, 0)),
                  pl.BlockSpec(nw.shape, lambda j: (0, 0))],
        out_specs=[blk3(zw), pl.BlockSpec((nseq, nh, dk, dv), lambda j: (0, 0, 0, 0))],
        out_shape=[jax.ShapeDtypeStruct((nseq, seq_len, zw), BF16),
                   jax.ShapeDtypeStruct((nseq, nh, dk, dv), F32)],
        scratch_shapes=[pltpu.VMEM((nseq, nh, dk, dv), F32),
                        pltpu.VMEM((nseq * nh, 2 * c, dv), F32),
                        pltpu.VMEM((nseq * nh, c, dv), BF16)],
        compiler_params=_cparams(("arbitrary",)),
        name="dn_scan",
    )(seq3(u), seq3(w), seq3(qd), seq3(kd), attn.reshape(nseq, nblk, nh, c, c), seq3(gl), seq3(proj),
      s0, nw)
    return y.reshape(rows, zw), sfin


def _rope_tables(pos):
    half = QK_ROPE // 2
    inv = ROPE_THETA ** (-jnp.arange(half, dtype=F32) / half)
    ang = pos.astype(F32)[:, None] * inv[None, :]
    cos, sin = jnp.cos(ang), jnp.sin(ang)
    zeros = jnp.zeros((pos.shape[0], LANE - QK_ROPE), F32)
    cq = jnp.concatenate([cos, cos, zeros], axis=1)
    sq = jnp.concatenate([sin, sin, zeros], axis=1)
    ck = jnp.concatenate([cos, cos, sin, sin], axis=1)
    return cq, sq, ck


def _rot_cols(w):
    half = w.shape[-1] // 2
    return jnp.concatenate([-w[..., half:], w[..., :half]], axis=-1)


def _row(v, width=None):
    v = v.astype(F32).reshape(1, -1)
    if width is not None and v.shape[1] < width:
        v = jnp.pad(v, ((0, 0), (0, width - v.shape[1])))
    return v


def kernel(x_prompt, x_sample, cache_mla_ckv, cache_mla_krope, state_dn_s, state_dn_conv, state_cf_conv, meta_tokens, norm_gains, w_in, mla_gq, mla_gkv, w_uq, w_uk, w_uv, dn_conv_w, dn_a_log, dn_dt_bias, dn_norm_w, w_out, cf_w_pw1, cf_b_pw1, cf_w_dw, cf_b_dw, cf_ln_g, cf_ln_b, cf_w_pw2, cf_b_pw2, w_gate, w_up, w_down):
    bp, lp, d = x_prompt.shape
    bs, ls, _ = x_sample.shape
    n_meta = meta_tokens.shape[0]
    past = cache_mla_ckv.shape[2] - n_meta
    depth = norm_gains.shape[0]
    assert n_meta == N_META and ls == n_meta and n_meta <= CHUNK
    assert past % CHUNK == 0 and ls <= CHUNK and lp % CHUNK == 0
    ns = bs + 1
    hw = MLA_HEADS * QK_NOPE

    hp = x_prompt.reshape(bp * lp, d)
    hs = jnp.concatenate([x_sample.reshape(bs * ls, d), meta_tokens.astype(F32)], axis=0)
    meta_rows = slice(bs * ls, bs * ls + n_meta)

    pos_p = n_meta + jnp.arange(lp)
    pos_s = jnp.concatenate([jnp.tile(n_meta + past + jnp.arange(ls), bs), jnp.arange(n_meta)])
    tab_p = _rope_tables(pos_p)
    tab_s = _rope_tables(pos_s)
    zero_d = jnp.zeros((1, d), F32)
    wg_all, wu_all, wd_all = w_gate.astype(BF16), w_up.astype(BF16), w_down.astype(BF16)

    outs = {k: [] for k in ("p_ckv", "p_kr", "p_s", "p_conv", "p_cf", "s_ckv", "s_kr", "s_s", "s_conv", "s_cf")}
    for layer in range(depth):
        ng = norm_gains[layer].astype(F32)
        g0, g1, g2, g3 = (ng[i:i + 1] for i in range(4))
        if layer % 2 == 0:
            e = layer // 2
            offs = np.cumsum((Q_LORA, KV_LORA, QK_ROPE, DN_QKV, DN_HEADS * DN_DV, DN_HEADS, DN_HEADS))
            wi = w_in[e]
            w_qd, w_kvd, w_kr = wi[:, :offs[0]], wi[:, offs[0]:offs[1]], wi[:, offs[1]:offs[2]]
            w_qkv, w_z = wi[:, offs[2]:offs[3]], wi[:, offs[3]:offs[4]]
            w_a, w_b = wi[:, offs[4]:offs[5]], wi[:, offs[5]:offs[6]]
            w_proj = jnp.concatenate(
                [w_qkv, w_z, w_qd, w_kvd, w_kr, _rot_cols(w_kr), w_a, w_b,
                 jnp.zeros((d, SMALL_W - 2 * QK_ROPE - 2 * DN_HEADS), F32)], axis=1).astype(BF16)
            zero_proj = jnp.zeros((1, PROJ_W), F32)
            wq3 = w_uq[e]
            wq_n = wq3[:, :, :QK_NOPE].reshape(Q_LORA, hw)
            wq_r = wq3[:, :, QK_NOPE:]
            pad_r = lambda w: jnp.pad(w, ((0, 0), (0, 0), (0, LANE - QK_ROPE))).reshape(Q_LORA, hw)
            wq = jnp.concatenate([wq_n, pad_r(wq_r), pad_r(_rot_cols(wq_r))], axis=1).astype(BF16)
            wk = w_uk[e].reshape(KV_LORA, hw).astype(BF16)
            wv = w_uv[e].reshape(KV_LORA, hw).astype(BF16)
            wkv = jnp.concatenate([wk, wv], axis=1)
            gq, gkv = _row(mla_gq[e]), _row(mla_gkv[e])
            cw = jnp.pad(dn_conv_w[e].astype(F32), ((0, 8 - DN_CONV), (0, 0)))
            ab = jnp.concatenate([_row(dn_a_log[e], LANE), _row(dn_dt_bias[e], LANE),
                                  jnp.zeros((6, LANE), F32)], axis=0)
            nw = _row(dn_norm_w[e])
            wo = w_out[e].astype(BF16)
            wo_mla, wo_dn = wo[:hw], wo[hw:]

            proj_s = norm_matmul(hs, g0, w_proj, zero_proj, PROJ_TM, PROJ_TN)
            qn_s, qr_s, ckv_s, kn_s, v_s, kro_s, krp_s = mla_prep(proj_s, gq, gkv, wq, wk, wv, *tab_s,
                                                                  MLA_TM, False)
            lh = n_meta + past
            lh_pad = -(-lh // LANE) * LANE
            hist_ckv = jnp.pad(cache_mla_ckv[e].astype(F32), ((0, 1), (0, lh_pad - lh), (0, 0)))
            hkn, hv = kv_up(hist_ckv.reshape(ns * lh_pad, KV_LORA), wkv, lh_pad)
            hkrp = jnp.pad(cache_mla_krope[e].astype(BF16),
                           ((0, 1), (0, lh_pad - lh), (0, LANE - QK_ROPE)))
            hlen_s = jnp.concatenate([jnp.full((bs,), lh, jnp.int32), jnp.zeros((1,), jnp.int32)])
            ymla_s = attention(qn_s, qr_s, kn_s, krp_s, v_s, hkn.reshape(ns, lh_pad, hw), hkrp,
                               hv.reshape(ns, lh_pad, hw), hlen_s, ns, ls, ls)
            conv_hist_s = jnp.pad(state_dn_conv[e].astype(F32), ((0, 1), (8 - (DN_CONV - 1), 0), (0, 0)))
            s0_s = jnp.pad(state_dn_s[e].astype(F32), ((0, 1), (0, 0), (0, 0), (0, 0)))
            ydn_s, sfin_s = deltanet(proj_s, cw, conv_hist_s, s0_s, ab, nw, ns, ls, CHUNK, 1)
            hs = matmul_resnorm([ymla_s, ydn_s], [wo_mla, wo_dn], zero_d, g1, hs, RES_TM)

            proj_p = norm_matmul(hp, g0, w_proj, zero_proj, PROJ_TM, PROJ_TN)
            qn_p, qr_p, ckv_p, kn_p, vt_p, kro_p, krp_p = mla_prep(proj_p, gq, gkv, wq, wk, wv.T, *tab_p,
                                                                   MLA_TM, True)
            hlen_p = jnp.full((bp,), n_meta, jnp.int32)
            ymla_p = attention_t(qn_p, qr_p, kn_p, krp_p, vt_p, kn_s[meta_rows], krp_s[meta_rows],
                                 v_s[meta_rows].T, hlen_p, bp, lp)
            conv_hist_p = jnp.pad(proj_s[meta_rows, COL_QKV:COL_QKV + DN_QKV][-(DN_CONV - 1):],
                                  ((8 - (DN_CONV - 1), 0), (0, 0)))[None]
            ydn_p, sfin_p = deltanet(proj_p, cw, conv_hist_p, sfin_s[bs:], ab, nw, bp, lp, CHUNK,
                                     DN_BLOCKS_PER_STEP)
            hp = matmul_resnorm([ymla_p, ydn_p], [wo_mla, wo_dn], zero_d, g1, hp, RES_TM)

            bc = lambda a: jnp.broadcast_to(a[None], (bp,) + a.shape)
            outs["p_ckv"].append(jnp.concatenate([bc(ckv_s[meta_rows]), ckv_p.reshape(bp, lp, KV_LORA)], axis=1))
            outs["p_kr"].append(jnp.concatenate([bc(kro_s[meta_rows]), kro_p.reshape(bp, lp, QK_ROPE)], axis=1))
            outs["p_s"].append(sfin_p)
            outs["p_conv"].append(proj_p.reshape(bp, lp, PROJ_W)[:, lp - (DN_CONV - 1):, COL_QKV:COL_QKV + DN_QKV])
            outs["s_ckv"].append(ckv_s[:bs * ls].reshape(bs, ls, KV_LORA))
            outs["s_kr"].append(kro_s[:bs * ls].reshape(bs, ls, QK_ROPE))
            outs["s_s"].append(sfin_s[:bs])
            xqkv_s = proj_s[:bs * ls, COL_QKV:COL_QKV + DN_QKV].reshape(bs, ls, DN_QKV)
            outs["s_conv"].append(jnp.concatenate([state_dn_conv[e].astype(F32), xqkv_s], axis=1)[:, -(DN_CONV - 1):])
        else:
            o = layer // 2
            w1 = cf_w_pw1[o].astype(BF16)
            b1 = _row(cf_b_pw1[o])
            wdw = jnp.broadcast_to(cf_w_dw[o].astype(F32)[:, None, :], (CF_KERNEL, 8, d))
            bdw, lg, lb = _row(cf_b_dw[o]), _row(cf_ln_g[o]), _row(cf_ln_b[o])
            w2 = cf_w_pw2[o].astype(BF16)
            b2 = _row(cf_b_pw2[o])
            keep = CF_KERNEL - 1

            hist_s = jnp.pad(state_cf_conv[o].astype(F32), ((0, 1), (CF_HALO - keep, 0), (0, 0)))
            c_s, tail_s = glu_conv(hs, g0, w1, b1, hist_s, wdw, bdw, ns, ls, GLU_TM, GLU_TN, GLU_CHUNK)
            hs = ln_matmul_resnorm(c_s, lg, lb, w2, b2, g1, hs, RES_TM, LN_CHUNK)

            c_p, tail_p = glu_conv(hp, g0, w1, b1, tail_s[bs:], wdw, bdw, bp, lp, GLU_TM, GLU_TN, GLU_CHUNK)
            hp = ln_matmul_resnorm(c_p, lg, lb, w2, b2, g1, hp, RES_TM, LN_CHUNK)

            outs["p_cf"].append(tail_p[:, CF_HALO - keep:])
            outs["s_cf"].append(tail_s[:bs, CF_HALO - keep:])
        hs = ffn(hs, g2, wg_all, wu_all, wd_all, g3, layer, FFN_TM, FFN_TF)
        hp = ffn(hp, g2, wg_all, wu_all, wd_all, g3, layer, FFN_TM, FFN_TF)

    y_prompt = hp.reshape(bp, lp, d)
    y_sample = hs[:bs * ls].reshape(bs, ls, d)
    st = lambda k: jnp.stack(outs[k])
    return (y_prompt, y_sample, st("p_ckv"), st("p_kr"), st("p_s"), st("p_conv"), st("p_cf"),
            st("s_ckv"), st("s_kr"), st("s_s"), st("s_conv"), st("s_cf"))
```

```python
import functools

import numpy as np
import jax
import jax.numpy as jnp
from jax import lax
from jax.experimental import pallas as pl
from jax.experimental.pallas import tpu as pltpu

F32 = jnp.float32
BF16 = jnp.bfloat16

EPS = 1e-6
NEG_INF = -1e30
CHUNK = 64
N_META = 16
MLA_HEADS = 8
Q_LORA = 512
KV_LORA = 512
QK_NOPE = 128
QK_ROPE = 64
V_HEAD = 128
ROPE_THETA = 10000.0
DN_HEADS = 8
DN_DK = 128
DN_DV = 128
DN_CONV = 4
DN_QKV = DN_HEADS * (2 * DN_DK + DN_DV)
CF_KERNEL = 31

LANE = 128
ATT_TILE = 256
VMEM_LIMIT = 56 * 1024 * 1024

PROJ_TM, PROJ_TN = 1024, 896
MLA_TM = 1024
RES_TM = 512
LN_CHUNK = 512
FFN_TM, FFN_TF = 512, 512
GLU_TM, GLU_TN, GLU_CHUNK = 1024, 512, 512
DN_BLOCKS_PER_STEP = 4

COL_QKV = 0
COL_Z = DN_QKV
COL_QD = COL_Z + DN_HEADS * DN_DV
COL_KVD = COL_QD + Q_LORA
COL_SMALL = COL_KVD + KV_LORA
SMALL_W = 256
PROJ_W = COL_SMALL + SMALL_W


def _cparams(sem):
    return pltpu.CompilerParams(dimension_semantics=sem, vmem_limit_bytes=VMEM_LIMIT)


def _rms(x, g):
    return x * lax.rsqrt(jnp.mean(x * x, axis=-1, keepdims=True) + EPS) * g


def _sigmoid(x):
    return 1.0 / (1.0 + jnp.exp(-x))


def _dot(a, b):
    return jnp.dot(a, b, preferred_element_type=F32)


def _dot_nt(a, b):
    return lax.dot_general(a, b, (((1,), (1,)), ((), ())), preferred_element_type=F32)


def _dot_tn(a, b):
    return lax.dot_general(a, b, (((0,), (0,)), ((), ())), preferred_element_type=F32)


def _row_tile(m, pref):
    t = min(pref, m)
    while m % t:
        t //= 2
    return t


def _norm_mm_kernel(x_ref, g_ref, w_ref, b_ref, o_ref, xn_ref):
    @pl.when(pl.program_id(1) == 0)
    def _():
        xn_ref[...] = _rms(x_ref[...], g_ref[...]).astype(BF16)

    o_ref[...] = (_dot(xn_ref[...], w_ref[...]) + b_ref[...]).astype(o_ref.dtype)


CF_HALO = 32


def _glu_conv_kernel(x_ref, g_ref, wa_ref, wg_ref, ba_ref, bg_ref, hist_ref, cw_ref, cb_ref,
                     c_ref, tail_ref, xn_ref, ext_ref, sh_ref, carry_ref, *, tm, chunk, tiles_per_seq,
                     col_major):
    first = CF_HALO - (CF_KERNEL - 1)
    sub = 8
    span = chunk + CF_HALO - sub

    if col_major:
        i = pl.program_id(1)
        j = pl.program_id(0)
        xn_ref[...] = _rms(x_ref[...], g_ref[...]).astype(BF16)
    else:
        i = pl.program_id(0)
        j = pl.program_id(1)

        @pl.when(j == 0)
        def _():
            xn_ref[...] = _rms(x_ref[...], g_ref[...]).astype(BF16)

    seq_start = (i % tiles_per_seq) == 0

    @pl.when(seq_start)
    def _():
        ext_ref[0:CF_HALO, :] = hist_ref[...]

    @pl.when(jnp.logical_not(seq_start))
    def _():
        ext_ref[0:CF_HALO, :] = carry_ref[j]

    def glu_chunk(c):
        lo = c * chunk
        xc = xn_ref[lo:lo + chunk, :]
        a = _dot(xc, wa_ref[...]) + ba_ref[...]
        gt = _dot(xc, wg_ref[...]) + bg_ref[...]
        ext_ref[CF_HALO + lo:CF_HALO + lo + chunk, :] = a * _sigmoid(gt)

    nchunk = tm // chunk
    glu_chunk(0)
    for c in range(nchunk):
        lo = c * chunk
        if c + 1 < nchunk:
            glu_chunk(c + 1)
        for sft in range(1, sub):
            sh_ref[sft - 1, 0:span, :] = ext_ref[lo + sft:lo + sft + span, :]
        rg = min(chunk, 4 * sub)
        for r0 in range(0, chunk, rg):
            acc = None
            for t in range(CF_KERNEL):
                sft = (first + t) % sub
                base = first + t - sft + r0
                if sft == 0:
                    src = ext_ref[lo + base:lo + base + rg, :]
                else:
                    src = sh_ref[sft - 1, base:base + rg, :]
                term = src.reshape(rg // sub, sub, src.shape[1]) * cw_ref[t]
                acc = term if acc is None else acc + term
            c_ref[lo + r0:lo + r0 + rg, :] = acc.reshape(rg, acc.shape[2]) + cb_ref[...]
    tail = ext_ref[tm:tm + CF_HALO, :]
    carry_ref[j] = tail
    tail_ref[...] = tail


def norm_matmul(x, g, w, b, tm, tn, out_dtype=F32):
    m, k = x.shape
    n = w.shape[1]
    tm = _row_tile(m, tm)
    return pl.pallas_call(
        _norm_mm_kernel,
        grid=(m // tm, n // tn),
        in_specs=[pl.BlockSpec((tm, k), lambda i, j: (i, 0)),
                  pl.BlockSpec((1, k), lambda i, j: (0, 0)),
                  pl.BlockSpec((k, tn), lambda i, j: (0, j)),
                  pl.BlockSpec((1, tn), lambda i, j: (0, j))],
        out_specs=pl.BlockSpec((tm, tn), lambda i, j: (i, j)),
        out_shape=jax.ShapeDtypeStruct((m, n), out_dtype),
        scratch_shapes=[pltpu.VMEM((tm, k), BF16)],
        compiler_params=_cparams(("parallel", "arbitrary")),
        name="norm_matmul",
    )(x, g, w, b)


def glu_conv(x, g, w, b, hist, cw, cb, nseq, seq_len, tm, tn, chunk):
    m, k = x.shape
    n = w.shape[1] // 2
    tm = _row_tile(seq_len, tm)
    chunk = min(chunk, tm)
    assert tm % chunk == 0 and chunk % 8 == 0
    tiles_per_seq = seq_len // tm
    nb = n // tn
    nhs = hist.shape[0]
    col_major = tiles_per_seq == 1 and tm < 256
    if col_major:
        grid = (nb, m // tm)
        spec = lambda shape, f: pl.BlockSpec(shape, lambda j, i: f(i, j))
    else:
        grid = (m // tm, nb)
        spec = lambda shape, f: pl.BlockSpec(shape, f)
    hidx = ((lambda i, j: (i // tiles_per_seq, 0, j)) if nhs > 1 else (lambda i, j: (0, 0, j)))
    c, tails = pl.pallas_call(
        functools.partial(_glu_conv_kernel, tm=tm, chunk=chunk, tiles_per_seq=tiles_per_seq,
                          col_major=col_major),
        grid=grid,
        in_specs=[spec((tm, k), lambda i, j: (i, 0)),
                  spec((1, k), lambda i, j: (0, 0)),
                  spec((k, tn), lambda i, j: (0, j)),
                  spec((k, tn), lambda i, j: (0, j + nb)),
                  spec((1, tn), lambda i, j: (0, j)),
                  spec((1, tn), lambda i, j: (0, j + nb)),
                  spec((None, CF_HALO, tn), hidx),
                  spec((cw.shape[0], 8, tn), lambda i, j: (0, 0, j)),
                  spec((1, tn), lambda i, j: (0, j))],
        out_specs=[spec((tm, tn), lambda i, j: (i, j)),
                   spec((None, CF_HALO, tn), lambda i, j: (i, 0, j))],
        out_shape=[jax.ShapeDtypeStruct((m, n), F32),
                   jax.ShapeDtypeStruct((m // tm, CF_HALO, n), F32)],
        scratch_shapes=[pltpu.VMEM((tm, k), BF16),
                        pltpu.VMEM((tm + CF_HALO, tn), F32),
                        pltpu.VMEM((7, chunk + CF_HALO - 8, tn), F32),
                        pltpu.VMEM((nb, CF_HALO, tn), F32)],
        compiler_params=_cparams(("arbitrary", "arbitrary")),
        name="glu_conv",
    )(x, g, w, w, b, b, hist, cw, cb)
    return c, tails.reshape(nseq, tiles_per_seq, CF_HALO, n)[:, -1]


def _mm_resnorm_kernel(*refs, n_in):
    xs = refs[:n_in]
    ws = refs[n_in:2 * n_in]
    b_ref, g_ref, res_ref, o_ref = refs[2 * n_in:]
    y = b_ref[...]
    for x_ref, w_ref in zip(xs, ws):
        y = y + _dot(x_ref[...], w_ref[...])
    o_ref[...] = res_ref[...] + _rms(y, g_ref[...])


def _ln_mm_resnorm_kernel(c_ref, lg_ref, lb_ref, w_ref, b_ref, g_ref, res_ref, o_ref, *, tm, chunk):
    for c in range(tm // chunk):
        rows = slice(c * chunk, (c + 1) * chunk)
        x = c_ref[rows, :]
        xc = x - jnp.mean(x, axis=-1, keepdims=True)
        y = xc * lax.rsqrt(jnp.mean(xc * xc, axis=-1, keepdims=True) + EPS) * lg_ref[...] + lb_ref[...]
        a = (y * _sigmoid(y)).astype(BF16)
        z = _dot(a, w_ref[...]) + b_ref[...]
        o_ref[rows, :] = res_ref[rows, :] + _rms(z, g_ref[...])


def ln_matmul_resnorm(c, lg, lb, w, b, g, res, tm, chunk):
    m, n = res.shape
    k = c.shape[1]
    tm = _row_tile(m, tm)
    chunk = min(chunk, tm)
    assert tm % chunk == 0
    vec = lambda width: pl.BlockSpec((1, width), lambda i: (0, 0))
    return pl.pallas_call(
        functools.partial(_ln_mm_resnorm_kernel, tm=tm, chunk=chunk),
        grid=(m // tm,),
        in_specs=[pl.BlockSpec((tm, k), lambda i: (i, 0)), vec(k), vec(k),
                  pl.BlockSpec(w.shape, lambda i: (0, 0)), vec(n), vec(n),
                  pl.BlockSpec((tm, n), lambda i: (i, 0))],
        out_specs=pl.BlockSpec((tm, n), lambda i: (i, 0)),
        out_shape=jax.ShapeDtypeStruct((m, n), F32),
        compiler_params=_cparams(("parallel",)),
        name="ln_matmul_resnorm",
    )(c, lg, lb, w, b, g, res)


def matmul_resnorm(xs, ws, b, g, res, tm):
    m, n = res.shape
    tm = _row_tile(m, tm)
    n_in = len(xs)
    in_specs = ([pl.BlockSpec((tm, x.shape[1]), lambda i: (i, 0)) for x in xs]
                + [pl.BlockSpec(w.shape, lambda i: (0, 0)) for w in ws]
                + [pl.BlockSpec((1, n), lambda i: (0, 0)),
                   pl.BlockSpec((1, n), lambda i: (0, 0)),
                   pl.BlockSpec((tm, n), lambda i: (i, 0))])
    return pl.pallas_call(
        functools.partial(_mm_resnorm_kernel, n_in=n_in),
        grid=(m // tm,),
        in_specs=in_specs,
        out_specs=pl.BlockSpec((tm, n), lambda i: (i, 0)),
        out_shape=jax.ShapeDtypeStruct((m, n), F32),
        compiler_params=_cparams(("parallel",)),
        name="matmul_resnorm",
    )(*xs, *ws, b, g, res)


def _ffn_kernel(h_ref, g2_ref, wg_ref, wu_ref, wd_ref, g3_ref, o_ref, xn_ref, acc_ref):
    j = pl.program_id(1)

    @pl.when(j == 0)
    def _():
        xn_ref[...] = _rms(h_ref[...], g2_ref[...]).astype(BF16)
        acc_ref[...] = jnp.zeros_like(acc_ref)

    xn = xn_ref[...]
    gate = _dot(xn, wg_ref[...])
    up = _dot(xn, wu_ref[...])
    a = (gate * _sigmoid(gate) * up).astype(BF16)
    acc_ref[...] += _dot(a, wd_ref[...])

    @pl.when(j == pl.num_programs(1) - 1)
    def _():
        o_ref[...] = h_ref[...] + _rms(acc_ref[...], g3_ref[...])


def ffn(h, g2, wg, wu, wd, g3, layer, tm, tf):
    m, d = h.shape
    f = wg.shape[2]
    tm = _row_tile(m, tm)
    return pl.pallas_call(
        _ffn_kernel,
        grid=(m // tm, f // tf),
        in_specs=[pl.BlockSpec((tm, d), lambda i, j: (i, 0)),
                  pl.BlockSpec((1, d), lambda i, j: (0, 0)),
                  pl.BlockSpec((None, d, tf), lambda i, j: (layer, 0, j)),
                  pl.BlockSpec((None, d, tf), lambda i, j: (layer, 0, j)),
                  pl.BlockSpec((None, tf, d), lambda i, j: (layer, j, 0)),
                  pl.BlockSpec((1, d), lambda i, j: (0, 0))],
        out_specs=pl.BlockSpec((tm, d), lambda i, j: (i, 0)),
        out_shape=jax.ShapeDtypeStruct((m, d), F32),
        scratch_shapes=[pltpu.VMEM((tm, d), BF16), pltpu.VMEM((tm, d), F32)],
        compiler_params=_cparams(("parallel", "arbitrary")),
        name="ffn",
    )(h, g2, wg, wu, wd, g3)


def _mla_prep_kernel(qd_ref, kvd_ref, sm_ref, gq_ref, gkv_ref, wq_ref, wk_ref, wv_ref, cq_ref, sq_ref, ck_ref,
                     qn_ref, qr_ref, ckv_ref, kn_ref, v_ref, kro_ref, krp_ref, *, v_transposed):
    hw = MLA_HEADS * QK_NOPE
    scale = (QK_NOPE + QK_ROPE) ** -0.5
    cq = _rms(qd_ref[...], gq_ref[...]).astype(BF16)
    q = _dot(cq, wq_ref[...])
    qn_ref[...] = (q[:, :hw] * scale).astype(BF16)
    cos8 = jnp.tile(cq_ref[...], (1, MLA_HEADS))
    sin8 = jnp.tile(sq_ref[...], (1, MLA_HEADS))
    qr_ref[...] = ((q[:, hw:2 * hw] * cos8 + q[:, 2 * hw:] * sin8) * scale).astype(BF16)
    ckv = _rms(kvd_ref[...], gkv_ref[...])
    ckv_ref[...] = ckv
    ckv_b = ckv.astype(BF16)
    kn_ref[...] = _dot(ckv_b, wk_ref[...]).astype(BF16)
    if v_transposed:
        vt = _dot_nt(wv_ref[...], ckv_b).astype(BF16)
        for t in range(v_ref.shape[0]):
            v_ref[t] = vt[:, t * ATT_TILE:(t + 1) * ATT_TILE]
    else:
        v_ref[...] = _dot(ckv_b, wv_ref[...]).astype(BF16)
    y = sm_ref[:, :LANE] * ck_ref[...]
    kro = y + pltpu.roll(y, QK_ROPE, 1)
    kro_ref[...] = kro[:, :QK_ROPE]
    lane = lax.broadcasted_iota(jnp.int32, kro.shape, 1)
    krp_ref[...] = jnp.where(lane < QK_ROPE, kro, 0.0).astype(BF16)


def mla_prep(proj, gq, gkv, wq, wk, wv, cq_tab, sq_tab, ck_tab, tm, v_transposed):
    m = proj.shape[0]
    tm = _row_tile(min(m, cq_tab.shape[0]), tm)
    nt = cq_tab.shape[0] // tm
    hw = MLA_HEADS * QK_NOPE
    tab = lambda: pl.BlockSpec((tm, LANE), lambda i: (i % nt, 0))
    full = lambda a: pl.BlockSpec(a.shape, lambda i: (0, 0))
    if v_transposed:
        assert tm % ATT_TILE == 0
        v_spec = pl.BlockSpec((tm // ATT_TILE, hw, ATT_TILE), lambda i: (i, 0, 0))
        v_shape = jax.ShapeDtypeStruct((m // ATT_TILE, hw, ATT_TILE), BF16)
    else:
        v_spec = pl.BlockSpec((tm, hw), lambda i: (i, 0))
        v_shape = jax.ShapeDtypeStruct((m, hw), BF16)
    return pl.pallas_call(
        functools.partial(_mla_prep_kernel, v_transposed=v_transposed),
        grid=(m // tm,),
        in_specs=[pl.BlockSpec((tm, Q_LORA), lambda i: (i, COL_QD // Q_LORA)),
                  pl.BlockSpec((tm, KV_LORA), lambda i: (i, COL_KVD // KV_LORA)),
                  pl.BlockSpec((tm, SMALL_W), lambda i: (i, COL_SMALL // SMALL_W)),
                  full(gq), full(gkv), full(wq), full(wk), full(wv), tab(), tab(), tab()],
        out_specs=[pl.BlockSpec((tm, hw), lambda i: (i, 0)),
                   pl.BlockSpec((tm, hw), lambda i: (i, 0)),
                   pl.BlockSpec((tm, KV_LORA), lambda i: (i, 0)),
                   pl.BlockSpec((tm, hw), lambda i: (i, 0)),
                   v_spec,
                   pl.BlockSpec((tm, QK_ROPE), lambda i: (i, 0)),
                   pl.BlockSpec((tm, LANE), lambda i: (i, 0))],
        out_shape=[jax.ShapeDtypeStruct((m, hw), BF16),
                   jax.ShapeDtypeStruct((m, hw), BF16),
                   jax.ShapeDtypeStruct((m, KV_LORA), F32),
                   jax.ShapeDtypeStruct((m, hw), BF16),
                   v_shape,
                   jax.ShapeDtypeStruct((m, QK_ROPE), F32),
                   jax.ShapeDtypeStruct((m, LANE), BF16)],
        compiler_params=_cparams(("parallel",)),
        name="mla_prep",
    )(proj, proj, proj, gq, gkv, wq, wk, wv, cq_tab, sq_tab, ck_tab)


def _kv_up_kernel(ckv_ref, wkv_ref, kn_ref, v_ref):
    hw = MLA_HEADS * QK_NOPE
    kv = _dot(ckv_ref[...].astype(BF16), wkv_ref[...])
    kn_ref[...] = kv[:, :hw].astype(BF16)
    v_ref[...] = kv[:, hw:].astype(BF16)


def kv_up(ckv, wkv, tm):
    m = ckv.shape[0]
    tm = _row_tile(m, tm)
    hw = MLA_HEADS * QK_NOPE
    return pl.pallas_call(
        _kv_up_kernel,
        grid=(m // tm,),
        in_specs=[pl.BlockSpec((tm, KV_LORA), lambda i: (i, 0)),
                  pl.BlockSpec(wkv.shape, lambda i: (0, 0))],
        out_specs=[pl.BlockSpec((tm, hw), lambda i: (i, 0)),
                   pl.BlockSpec((tm, hw), lambda i: (i, 0))],
        out_shape=[jax.ShapeDtypeStruct((m, hw), BF16), jax.ShapeDtypeStruct((m, hw), BF16)],
        compiler_params=_cparams(("parallel",)),
        name="kv_up",
    )(ckv, wkv)


def _attn_kernel(hlen_ref, qn_ref, qr_ref, kn_ref, kr_ref, v_ref, hkn_ref, hkr_ref, hv_ref, o_ref,
                 m_ref, l_ref, acc_ref, a_ref, qc_ref, s_ref, p_ref, hs_ref, hp_ref, *, tile, lh):
    s_idx = pl.program_id(0)
    qt = pl.program_id(1)
    hlen = hlen_ref[s_idx]
    nh = MLA_HEADS
    heads = [slice(h * LANE, (h + 1) * LANE) for h in range(nh)]

    for h, hs in enumerate(heads):
        qc_ref[h] = jnp.concatenate([qn_ref[:, hs], qr_ref[:, hs]], axis=1)

    def process(get_k, get_v, s_scr, p_scr, valid, first):
        for h in range(nh):
            s_scr[h] = _dot_nt(qc_ref[h], get_k(h))
        for h, hs in enumerate(heads):
            s = s_scr[h]
            if valid is not None:
                s = jnp.where(valid, s, NEG_INF)
            m_cur = jnp.max(s, axis=1, keepdims=True)
            if first:
                m_new = jnp.broadcast_to(m_cur, (tile, LANE))
            else:
                m_prev = m_ref[:, hs]
                m_new = jnp.maximum(m_prev, m_cur)
            p = jnp.exp(s - m_new[:, :1])
            if valid is not None:
                p = jnp.where(valid, p, 0.0)
            p_scr[h] = p.astype(BF16)
            l_cur = jnp.sum(p, axis=1, keepdims=True)
            if first:
                l_ref[:, hs] = jnp.broadcast_to(l_cur, (tile, LANE))
            else:
                alpha = jnp.exp(m_prev - m_new)
                a_ref[:, hs] = alpha
                l_ref[:, hs] = alpha * l_ref[:, hs] + l_cur
            m_ref[:, hs] = m_new
        for h, hs in enumerate(heads):
            pv = _dot(p_scr[h], get_v(h))
            if first:
                acc_ref[:, hs] = pv
            else:
                acc_ref[:, hs] = a_ref[:, hs] * acc_ref[:, hs] + pv

    hvalid = lax.broadcasted_iota(jnp.int32, (tile, lh), 1) < hlen
    process(lambda h: jnp.concatenate([hkn_ref[:, heads[h]], hkr_ref[...]], axis=1),
            lambda h: hv_ref[:, heads[h]], hs_ref, hp_ref, hvalid, True)

    def own_tile(kt, valid):
        rows = pl.ds(pl.multiple_of(kt * tile, tile), tile)
        process(lambda h: jnp.concatenate([kn_ref[rows, heads[h]], kr_ref[rows, :]], axis=1),
                lambda h: v_ref[rows, heads[h]], s_ref, p_ref, valid, False)

    def body(kt, carry):
        own_tile(kt, None)
        return carry

    lax.fori_loop(0, qt, body, 0)
    if tile > CHUNK:
        row = lax.broadcasted_iota(jnp.int32, (tile, tile), 0) // CHUNK
        col = lax.broadcasted_iota(jnp.int32, (tile, tile), 1) // CHUNK
        dvalid = col <= row
    else:
        dvalid = None
    own_tile(qt, dvalid)

    for hs in heads:
        o_ref[:, hs] = (acc_ref[:, hs] / l_ref[:, hs]).astype(o_ref.dtype)


def attention(qn, qr, kn, krp, v, hkn, hkrp, hv, hlen, nseq, seq_len, tile):
    hw = MLA_HEADS * LANE
    tile = min(tile, seq_len)
    assert seq_len % tile == 0 and (tile % CHUNK == 0 or seq_len == tile <= CHUNK)
    nqt = seq_len // tile
    nhs, lh = hkn.shape[0], hkn.shape[1]
    assert nhs in (1, nseq)
    hidx = (lambda s, q, hl: (s, 0, 0)) if nhs > 1 else (lambda s, q, hl: (0, 0, 0))
    grid_spec = pltpu.PrefetchScalarGridSpec(
        num_scalar_prefetch=1,
        grid=(nseq, nqt),
        in_specs=[pl.BlockSpec((tile, hw), lambda s, q, hl: (s * nqt + q, 0)),
                  pl.BlockSpec((tile, hw), lambda s, q, hl: (s * nqt + q, 0)),
                  pl.BlockSpec((seq_len, hw), lambda s, q, hl: (s, 0)),
                  pl.BlockSpec((seq_len, LANE), lambda s, q, hl: (s, 0)),
                  pl.BlockSpec((seq_len, hw), lambda s, q, hl: (s, 0)),
                  pl.BlockSpec((None, lh, hw), hidx),
                  pl.BlockSpec((None, lh, LANE), hidx),
                  pl.BlockSpec((None, lh, hw), hidx)],
        out_specs=pl.BlockSpec((tile, hw), lambda s, q, hl: (s * nqt + q, 0)),
        scratch_shapes=[pltpu.VMEM((tile, hw), F32), pltpu.VMEM((tile, hw), F32),
                        pltpu.VMEM((tile, hw), F32), pltpu.VMEM((tile, hw), F32),
                        pltpu.VMEM((MLA_HEADS, tile, 2 * LANE), BF16),
                        pltpu.VMEM((MLA_HEADS, tile, tile), F32),
                        pltpu.VMEM((MLA_HEADS, tile, tile), BF16),
                        pltpu.VMEM((MLA_HEADS, tile, lh), F32),
                        pltpu.VMEM((MLA_HEADS, tile, lh), BF16)],
    )
    return pl.pallas_call(
        functools.partial(_attn_kernel, tile=tile, lh=lh),
        grid_spec=grid_spec,
        out_shape=jax.ShapeDtypeStruct((nseq * seq_len, hw), BF16),
        compiler_params=_cparams(("parallel", "arbitrary")),
        name="attention",
    )(hlen, qn, qr, kn, krp, v, hkn, hkrp, hv)


def _attn_t_kernel(hlen_ref, qn_ref, qr_ref, kn_ref, kr_ref, vt_ref, hkn_ref, hkr_ref, hvt_ref, o_ref,
                   m_ref, l_ref, a_ref, acc_ref, qc_ref, s_ref, p_ref, s2_ref, p2_ref, hs_ref, hp_ref,
                   *, tile, lh):
    s_idx = pl.program_id(0)
    qt = pl.program_id(1)
    hlen = hlen_ref[s_idx]
    nh = MLA_HEADS
    heads = [slice(h * LANE, (h + 1) * LANE) for h in range(nh)]

    for h, hs in enumerate(heads):
        qc_ref[h] = jnp.concatenate([qn_ref[:, hs], qr_ref[:, hs]], axis=1)

    def process(get_k, get_vt, s_scr, p_scr, valid, first):
        for h in range(nh):
            s_scr[h] = _dot_nt(get_k(h), qc_ref[h])
        for h in range(nh):
            for q0 in range(0, tile, LANE):
                qs = slice(q0, q0 + LANE)
                s = s_scr[h, :, qs]
                ok = None if valid is None else valid(q0)
                if ok is not None:
                    s = jnp.where(ok, s, NEG_INF)
                m_cur = jnp.max(s, axis=0, keepdims=True)
                if first:
                    m_new = m_cur
                else:
                    m_prev = m_ref[h, :, qs]
                    m_new = jnp.maximum(m_prev, m_cur)
                p = jnp.exp(s - m_new)
                if ok is not None:
                    p = jnp.where(ok, p, 0.0)
                p_scr[h, :, qs] = p.astype(BF16)
                l_cur = jnp.sum(p, axis=0, keepdims=True)
                if first:
                    l_ref[h, :, qs] = l_cur
                else:
                    alpha = jnp.exp(m_prev - m_new)
                    a_ref[h, :, qs] = alpha
                    l_ref[h, :, qs] = alpha * l_ref[h, :, qs] + l_cur
                m_ref[h, :, qs] = m_new
        for h in range(nh):
            pv = _dot(get_vt(h), p_scr[h])
            if first:
                acc_ref[h] = pv
            else:
                acc_ref[h] = a_ref[h] * acc_ref[h] + pv

    hvalid = lax.broadcasted_iota(jnp.int32, (lh, LANE), 0) < hlen
    process(lambda h: jnp.concatenate([hkn_ref[:, heads[h]], hkr_ref[...]], axis=1),
            lambda h: hvt_ref[heads[h], :], hs_ref, hp_ref, lambda q0: hvalid, True)

    def own_tile(kt, valid):
        rows = pl.ds(pl.multiple_of(kt * tile, tile), tile)
        process(lambda h: jnp.concatenate([kn_ref[rows, heads[h]], kr_ref[rows, :]], axis=1),
                lambda h: vt_ref[kt, heads[h], :], s_ref, p_ref, valid, False)

    def own_pair(kp):
        rows = pl.ds(pl.multiple_of(kp * (2 * tile), 2 * tile), 2 * tile)
        process(lambda h: jnp.concatenate([kn_ref[rows, heads[h]], kr_ref[rows, :]], axis=1),
                lambda h: jnp.concatenate([vt_ref[2 * kp, heads[h], :], vt_ref[2 * kp + 1, heads[h], :]],
                                          axis=1),
                s2_ref, p2_ref, None, False)

    def body(kp, carry):
        own_pair(kp)
        return carry

    lax.fori_loop(0, qt // 2, body, 0)

    @pl.when(qt % 2 == 1)
    def _():
        own_tile(qt - 1, None)

    key_chunk = lax.broadcasted_iota(jnp.int32, (tile, LANE), 0) // CHUNK
    qry_lane = lax.broadcasted_iota(jnp.int32, (tile, LANE), 1)
    own_tile(qt, lambda q0: key_chunk <= (qry_lane + q0) // CHUNK)

    for h, hs in enumerate(heads):
        o_ref[:, hs] = (acc_ref[h] / l_ref[h]).T.astype(o_ref.dtype)


def attention_t(qn, qr, kn, krp, vt, hkn, hkrp, hvt, hlen, nseq, seq_len):
    hw = MLA_HEADS * LANE
    tile = ATT_TILE
    assert seq_len % tile == 0 and tile % CHUNK == 0
    nqt = seq_len // tile
    lh = hkn.shape[0]
    nh = MLA_HEADS
    grid_spec = pltpu.PrefetchScalarGridSpec(
        num_scalar_prefetch=1,
        grid=(nseq, nqt),
        in_specs=[pl.BlockSpec((tile, hw), lambda s, q, hl: (s * nqt + q, 0)),
                  pl.BlockSpec((tile, hw), lambda s, q, hl: (s * nqt + q, 0)),
                  pl.BlockSpec((seq_len, hw), lambda s, q, hl: (s, 0)),
                  pl.BlockSpec((seq_len, LANE), lambda s, q, hl: (s, 0)),
                  pl.BlockSpec((nqt, hw, tile), lambda s, q, hl: (s, 0, 0)),
                  pl.BlockSpec((lh, hw), lambda s, q, hl: (0, 0)),
                  pl.BlockSpec((lh, LANE), lambda s, q, hl: (0, 0)),
                  pl.BlockSpec((hw, lh), lambda s, q, hl: (0, 0))],
        out_specs=pl.BlockSpec((tile, hw), lambda s, q, hl: (s * nqt + q, 0)),
        scratch_shapes=[pltpu.VMEM((nh, 1, tile), F32), pltpu.VMEM((nh, 1, tile), F32),
                        pltpu.VMEM((nh, 1, tile), F32),
                        pltpu.VMEM((nh, V_HEAD, tile), F32),
                        pltpu.VMEM((nh, tile, 2 * LANE), BF16),
                        pltpu.VMEM((nh, tile, tile), F32),
                        pltpu.VMEM((nh, tile, tile), BF16),
                        pltpu.VMEM((nh, 2 * tile, tile), F32),
                        pltpu.VMEM((nh, 2 * tile, tile), BF16),
                        pltpu.VMEM((nh, lh, tile), F32),
                        pltpu.VMEM((nh, lh, tile), BF16)],
    )
    return pl.pallas_call(
        functools.partial(_attn_t_kernel, tile=tile, lh=lh),
        grid_spec=grid_spec,
        out_shape=jax.ShapeDtypeStruct((nseq * seq_len, hw), BF16),
        compiler_params=_cparams(("parallel", "arbitrary")),
        name="attention_t",
    )(hlen, qn, qr, kn, krp, vt, hkn, hkrp, hvt)


def _dn_prep_kernel(x_ref, sm_ref, cw_ref, hist_ref, ab_ref,
                    u_ref, w_ref, qd_ref, kd_ref, attn_ref, gl_ref,
                    ext_ref, act_ref, lhs_ref, kb_ref, dec_ref, rhs_ref, mm_ref, t_ref, xs_ref, *, c, g):
    j = pl.program_id(1)
    nh, dk, dv = DN_HEADS, DN_DK, DN_DV
    kw = nh * dk
    rows_all = c * g
    inst = [(b, h) for b in range(g) for h in range(nh)]

    @pl.when(j == 0)
    def _():
        ext_ref[0:8, :] = hist_ref[...]

    ext_ref[8:8 + rows_all, :] = x_ref[...]
    conv = cw_ref[0:1, :] * ext_ref[5:5 + rows_all, :]
    for t in range(1, DN_CONV):
        conv = conv + cw_ref[t:t + 1, :] * ext_ref[5 + t:5 + t + rows_all, :]
    ext_ref[0:8, :] = ext_ref[rows_all:rows_all + 8, :]
    act_ref[...] = conv * _sigmoid(conv)

    gates = sm_ref[:, LANE:2 * LANE]
    xa = gates + ab_ref[1:2, :]
    softplus = jnp.maximum(xa, 0.0) + jnp.log(1.0 + jnp.exp(-jnp.abs(xa)))
    g_all = -jnp.exp(ab_ref[0:1, :]) * softplus
    beta_all = _sigmoid(gates)

    r = lax.broadcasted_iota(jnp.int32, (c, c), 0)
    q = lax.broadcasted_iota(jnp.int32, (c, c), 1)
    incl = r >= q
    strict = r > q
    eye = jnp.where(r == q, 1.0, 0.0)
    tri = jnp.where(incl, 1.0, 0.0)

    for b in range(g):
        rows = slice(b * c, (b + 1) * c)
        gc = jnp.dot(tri, g_all[rows], preferred_element_type=F32, precision=lax.Precision.HIGHEST)
        if c < LANE:
            gc_sq = jnp.concatenate([gc, jnp.zeros((LANE - c, LANE), F32)], axis=0)
        else:
            gc_sq = gc
        gc_t = gc_sq.T
        egc = jnp.exp(gc)
        glast = gc[c - 1:c, :]
        edl = jnp.exp(glast - gc)
        gl_ref[rows, :] = jnp.broadcast_to(glast, (c, LANE))
        beta_b = beta_all[rows]
        for h in range(nh):
            i = b * nh + h
            hs = slice(h * dk, (h + 1) * dk)
            qh = act_ref[rows, hs]
            kh = act_ref[rows, kw + h * dk:kw + (h + 1) * dk]
            vh = act_ref[rows, 2 * kw + h * dv:2 * kw + (h + 1) * dv]
            qh = qh * (lax.rsqrt(jnp.sum(qh * qh, axis=1, keepdims=True) + EPS) * dk ** -0.5)
            kh = kh * lax.rsqrt(jnp.sum(kh * kh, axis=1, keepdims=True) + EPS)
            bcol = beta_b[:, 8 + h:9 + h]
            gcol = gc[:, h:h + 1]
            grow = gc_t[h:h + 1, :c]
            dec_ref[i] = jnp.where(incl, jnp.exp(jnp.where(incl, gcol - grow, 0.0)), 0.0)
            kb = kh * bcol
            ecol = egc[:, h:h + 1]
            lhs_ref[i] = jnp.concatenate([kb, qh], axis=0).astype(BF16)
            kb_ref[i] = kh.astype(BF16)
            rhs_ref[i] = jnp.concatenate([vh * bcol, kb * ecol], axis=1).astype(BF16)
            qd_ref[rows, hs] = (qh * ecol).astype(BF16)
            kd_ref[rows, hs] = (kh * edl[:, h:h + 1]).astype(BF16)

    for i, (b, h) in enumerate(inst):
        kq = _dot_nt(lhs_ref[i], kb_ref[i])
        dec = dec_ref[i]
        mm = jnp.where(strict, kq[:c] * dec, 0.0)
        mm_ref[i] = mm
        t_ref[i] = eye - jnp.where((r ^ q) == 1, mm, 0.0)
        attn_ref[b, h] = (kq[c:] * dec).astype(BF16)

    s = 2
    while s < c:
        sh = s.bit_length() - 1
        sel = ((r >> sh) ^ (q >> sh)) == 1
        for i in range(len(inst)):
            e = jnp.where(sel, mm_ref[i], 0.0).astype(BF16)
            xs_ref[i] = _dot(e, t_ref[i].astype(BF16)).astype(BF16)
        for i in range(len(inst)):
            t = t_ref[i]
            t_ref[i] = t - _dot(t.astype(BF16), xs_ref[i])
        s *= 2

    for i, (b, h) in enumerate(inst):
        rows = slice(b * c, (b + 1) * c)
        hs = slice(h * dk, (h + 1) * dk)
        uw = _dot(t_ref[i].astype(BF16), rhs_ref[i])
        u_ref[rows, hs] = uw[:, :dv]
        w_ref[rows, hs] = uw[:, dv:].astype(BF16)


def _dn_scan_kernel(u_ref, w_ref, qd_ref, kd_ref, attn_ref, gl_ref, z_ref, s0_ref, nw_ref,
                    y_ref, sfin_ref, s_ref, rr_ref, vn_ref, *, c, nseq, shared_s0):
    j = pl.program_id(0)
    nh, dk = DN_HEADS, DN_DK
    inst = [(s, h) for s in range(nseq) for h in range(nh)]
    heads = [slice(h * dk, (h + 1) * dk) for h in range(nh)]

    @pl.when(j == 0)
    def _():
        for s in range(nseq):
            s_ref[s] = s0_ref[0 if shared_s0 else s]

    for i, (s, h) in enumerate(inst):
        lhs = jnp.concatenate([w_ref[s, :, heads[h]], qd_ref[s, :, heads[h]]], axis=0)
        rr_ref[i] = _dot(lhs, s_ref[s, h].astype(BF16))
    for i, (s, h) in enumerate(inst):
        vn_ref[i] = (u_ref[s, :, heads[h]] - rr_ref[i, 0:c, :]).astype(BF16)
    for i, (s, h) in enumerate(inst):
        o = rr_ref[i, c:2 * c, :] + _dot(attn_ref[s, h], vn_ref[i])
        zh = z_ref[s, :, heads[h]]
        y_ref[s, :, heads[h]] = (_rms(o, nw_ref[...]) * (zh * _sigmoid(zh))).astype(y_ref.dtype)
    for i, (s, h) in enumerate(inst):
        ebd = jnp.exp(gl_ref[s, 0:1, h:h + 1])
        s_ref[s, h] = s_ref[s, h] * ebd + _dot_tn(kd_ref[s, :, heads[h]], vn_ref[i])

    @pl.when(j == pl.num_programs(0) - 1)
    def _():
        sfin_ref[...] = s_ref[...]


def deltanet(proj, cw, hist, s0, ab, nw, nseq, seq_len, c, g):
    c = min(c, seq_len)
    nblk = seq_len // c
    g = min(g, nblk)
    assert seq_len % c == 0 and c % 8 == 0 and nblk % g == 0
    ntile = nblk // g
    rows_t = c * g
    nhs = hist.shape[0]
    assert nhs in (1, nseq) and s0.shape[0] == nhs
    hidx3 = (lambda s, j: (s, 0, 0)) if nhs > 1 else (lambda s, j: (0, 0, 0))
    nh, dk, dv = DN_HEADS, DN_DK, DN_DV
    zw = nh * dv
    rows = nseq * seq_len
    ni = g * nh
    row_blk = lambda w: pl.BlockSpec((rows_t, w), lambda s, j: (s * ntile + j, 0))
    u, w, qd, kd, attn, gl = pl.pallas_call(
        functools.partial(_dn_prep_kernel, c=c, g=g),
        grid=(nseq, ntile),
        in_specs=[pl.BlockSpec((rows_t, DN_QKV), lambda s, j: (s * ntile + j, COL_QKV // DN_QKV)),
                  pl.BlockSpec((rows_t, SMALL_W), lambda s, j: (s * ntile + j, COL_SMALL // SMALL_W)),
                  pl.BlockSpec(cw.shape, lambda s, j: (0, 0)),
                  pl.BlockSpec((None, 8, DN_QKV), hidx3),
                  pl.BlockSpec(ab.shape, lambda s, j: (0, 0))],
        out_specs=[row_blk(zw), row_blk(zw), row_blk(zw), row_blk(zw),
                   pl.BlockSpec((g, nh, c, c), lambda s, j: (s * ntile + j, 0, 0, 0)),
                   row_blk(LANE)],
        out_shape=[jax.ShapeDtypeStruct((rows, zw), F32),
                   jax.ShapeDtypeStruct((rows, zw), BF16),
                   jax.ShapeDtypeStruct((rows, zw), BF16),
                   jax.ShapeDtypeStruct((rows, zw), BF16),
                   jax.ShapeDtypeStruct((nseq * nblk, nh, c, c), BF16),
                   jax.ShapeDtypeStruct((rows, LANE), F32)],
        scratch_shapes=[pltpu.VMEM((rows_t + 8, DN_QKV), F32),
                        pltpu.VMEM((rows_t, DN_QKV), F32),
                        pltpu.VMEM((ni, 2 * c, dk), BF16),
                        pltpu.VMEM((ni, c, dk), BF16),
                        pltpu.VMEM((ni, c, c), F32),
                        pltpu.VMEM((ni, c, dk + dv), BF16),
                        pltpu.VMEM((ni, c, c), F32),
                        pltpu.VMEM((ni, c, c), F32),
                        pltpu.VMEM((ni, c, c), BF16)],
        compiler_params=_cparams(("parallel", "arbitrary")),
        name="dn_prep",
    )(proj, proj, cw, hist, ab)

    seq3 = lambda a: a.reshape(nseq, seq_len, a.shape[1])
    blk3 = lambda wdt, col=0: pl.BlockSpec((nseq, c, wdt), lambda j: (0, j, col))
    y, sfin = pl.pallas_call(
        functools.partial(_dn_scan_kernel, c=c, nseq=nseq, shared_s0=nhs == 1),
        grid=(nblk,),
        in_specs=[blk3(zw), blk3(zw), blk3(zw), blk3(zw),
                  pl.BlockSpec((nseq, None, nh, c, c), lambda j: (0, j, 0, 0, 0)),
                  blk3(LANE), blk3(zw, COL_Z // zw),
                  pl.BlockSpec(s0.shape, lambda j: (0, 0, 0, 0)),
                  pl.BlockSpec(nw.shape, lambda j: (0, 0))],
        out_specs=[blk3(zw), pl.BlockSpec((nseq, nh, dk, dv), lambda j: (0, 0, 0, 0))],
        out_shape=[jax.ShapeDtypeStruct((nseq, seq_len, zw), BF16),
                   jax.ShapeDtypeStruct((nseq, nh, dk, dv), F32)],
        scratch_shapes=[pltpu.VMEM((nseq, nh, dk, dv), F32),
                        pltpu.VMEM((nseq * nh, 2 * c, dv), F32),
                        pltpu.VMEM((nseq * nh, c, dv), BF16)],
        compiler_params=_cparams(("arbitrary",)),
        name="dn_scan",
    )(seq3(u), seq3(w), seq3(qd), seq3(kd), attn.reshape(nseq, nblk, nh, c, c), seq3(gl), seq3(proj),
      s0, nw)
    return y.reshape(rows, zw), sfin


def _rope_tables(pos):
    half = QK_ROPE // 2
    inv = ROPE_THETA ** (-jnp.arange(half, dtype=F32) / half)
    ang = pos.astype(F32)[:, None] * inv[None, :]
    cos, sin = jnp.cos(ang), jnp.sin(ang)
    zeros = jnp.zeros((pos.shape[0], LANE - QK_ROPE), F32)
    cq = jnp.concatenate([cos, cos, zeros], axis=1)
    sq = jnp.concatenate([sin, sin, zeros], axis=1)
    ck = jnp.concatenate([cos, cos, sin, sin], axis=1)
    return cq, sq, ck


def _rot_cols(w):
    half = w.shape[-1] // 2
    return jnp.concatenate([-w[..., half:], w[..., :half]], axis=-1)


def _row(v, width=None):
    v = v.astype(F32).reshape(1, -1)
    if width is not None and v.shape[1] < width:
        v = jnp.pad(v, ((0, 0), (0, width - v.shape[1])))
    return v


def kernel(x_prompt, x_sample, cache_mla_ckv, cache_mla_krope, state_dn_s, state_dn_conv, state_cf_conv, meta_tokens, norm_gains, w_in, mla_gq, mla_gkv, w_uq, w_uk, w_uv, dn_conv_w, dn_a_log, dn_dt_bias, dn_norm_w, w_out, cf_w_pw1, cf_b_pw1, cf_w_dw, cf_b_dw, cf_ln_g, cf_ln_b, cf_w_pw2, cf_b_pw2, w_gate, w_up, w_down):
    bp, lp, d = x_prompt.shape
    bs, ls, _ = x_sample.shape
    n_meta = meta_tokens.shape[0]
    past = cache_mla_ckv.shape[2] - n_meta
    depth = norm_gains.shape[0]
    assert n_meta == N_META and ls == n_meta and n_meta <= CHUNK
    assert past % CHUNK == 0 and ls <= CHUNK and lp % CHUNK == 0
    ns = bs + 1
    hw = MLA_HEADS * QK_NOPE

    hp = x_prompt.reshape(bp * lp, d)
    hs = jnp.concatenate([x_sample.reshape(bs * ls, d), meta_tokens.astype(F32)], axis=0)
    meta_rows = slice(bs * ls, bs * ls + n_meta)

    pos_p = n_meta + jnp.arange(lp)
    pos_s = jnp.concatenate([jnp.tile(n_meta + past + jnp.arange(ls), bs), jnp.arange(n_meta)])
    tab_p = _rope_tables(pos_p)
    tab_s = _rope_tables(pos_s)
    zero_d = jnp.zeros((1, d), F32)
    wg_all, wu_all, wd_all = w_gate.astype(BF16), w_up.astype(BF16), w_down.astype(BF16)

    outs = {k: [] for k in ("p_ckv", "p_kr", "p_s", "p_conv", "p_cf", "s_ckv", "s_kr", "s_s", "s_conv", "s_cf")}
    for layer in range(depth):
        ng = norm_gains[layer].astype(F32)
        g0, g1, g2, g3 = (ng[i:i + 1] for i in range(4))
        if layer % 2 == 0:
            e = layer // 2
            offs = np.cumsum((Q_LORA, KV_LORA, QK_ROPE, DN_QKV, DN_HEADS * DN_DV, DN_HEADS, DN_HEADS))
            wi = w_in[e]
            w_qd, w_kvd, w_kr = wi[:, :offs[0]], wi[:, offs[0]:offs[1]], wi[:, offs[1]:offs[2]]
            w_qkv, w_z = wi[:, offs[2]:offs[3]], wi[:, offs[3]:offs[4]]
            w_a, w_b = wi[:, offs[4]:offs[5]], wi[:, offs[5]:offs[6]]
            w_proj = jnp.concatenate(
                [w_qkv, w_z, w_qd, w_kvd, w_kr, _rot_cols(w_kr), w_a, w_b,
                 jnp.zeros((d, SMALL_W - 2 * QK_ROPE - 2 * DN_HEADS), F32)], axis=1).astype(BF16)
            zero_proj = jnp.zeros((1, PROJ_W), F32)
            wq3 = w_uq[e]
            wq_n = wq3[:, :, :QK_NOPE].reshape(Q_LORA, hw)
            wq_r = wq3[:, :, QK_NOPE:]
            pad_r = lambda w: jnp.pad(w, ((0, 0), (0, 0), (0, LANE - QK_ROPE))).reshape(Q_LORA, hw)
            wq = jnp.concatenate([wq_n, pad_r(wq_r), pad_r(_rot_cols(wq_r))], axis=1).astype(BF16)
            wk = w_uk[e].reshape(KV_LORA, hw).astype(BF16)
            wv = w_uv[e].reshape(KV_LORA, hw).astype(BF16)
            wkv = jnp.concatenate([wk, wv], axis=1)
            gq, gkv = _row(mla_gq[e]), _row(mla_gkv[e])
            cw = jnp.pad(dn_conv_w[e].astype(F32), ((0, 8 - DN_CONV), (0, 0)))
            ab = jnp.concatenate([_row(dn_a_log[e], LANE), _row(dn_dt_bias[e], LANE),
                                  jnp.zeros((6, LANE), F32)], axis=0)
            nw = _row(dn_norm_w[e])
            wo = w_out[e].astype(BF16)
            wo_mla, wo_dn = wo[:hw], wo[hw:]

            proj_s = norm_matmul(hs, g0, w_proj, zero_proj, PROJ_TM, PROJ_TN)
            qn_s, qr_s, ckv_s, kn_s, v_s, kro_s, krp_s = mla_prep(proj_s, gq, gkv, wq, wk, wv, *tab_s,
                                                                  MLA_TM, False)
            lh = n_meta + past
            lh_pad = -(-lh // LANE) * LANE
            hist_ckv = jnp.pad(cache_mla_ckv[e].astype(F32), ((0, 1), (0, lh_pad - lh), (0, 0)))
            hkn, hv = kv_up(hist_ckv.reshape(ns * lh_pad, KV_LORA), wkv, lh_pad)
            hkrp = jnp.pad(cache_mla_krope[e].astype(BF16),
                           ((0, 1), (0, lh_pad - lh), (0, LANE - QK_ROPE)))
            hlen_s = jnp.concatenate([jnp.full((bs,), lh, jnp.int32), jnp.zeros((1,), jnp.int32)])
            ymla_s = attention(qn_s, qr_s, kn_s, krp_s, v_s, hkn.reshape(ns, lh_pad, hw), hkrp,
                               hv.reshape(ns, lh_pad, hw), hlen_s, ns, ls, ls)
            conv_hist_s = jnp.pad(state_dn_conv[e].astype(F32), ((0, 1), (8 - (DN_CONV - 1), 0), (0, 0)))
            s0_s = jnp.pad(state_dn_s[e].astype(F32), ((0, 1), (0, 0), (0, 0), (0, 0)))
            ydn_s, sfin_s = deltanet(proj_s, cw, conv_hist_s, s0_s, ab, nw, ns, ls, CHUNK, 1)
            hs = matmul_resnorm([ymla_s, ydn_s], [wo_mla, wo_dn], zero_d, g1, hs, RES_TM)

            proj_p = norm_matmul(hp, g0, w_proj, zero_proj, PROJ_TM, PROJ_TN)
            qn_p, qr_p, ckv_p, kn_p, vt_p, kro_p, krp_p = mla_prep(proj_p, gq, gkv, wq, wk, wv.T, *tab_p,
                                                                   MLA_TM, True)
            hlen_p = jnp.full((bp,), n_meta, jnp.int32)
            ymla_p = attention_t(qn_p, qr_p, kn_p, krp_p, vt_p, kn_s[meta_rows], krp_s[meta_rows],
                                 v_s[meta_rows].T, hlen_p, bp, lp)
            conv_hist_p = jnp.pad(proj_s[meta_rows, COL_QKV:COL_QKV + DN_QKV][-(DN_CONV - 1):],
                                  ((8 - (DN_CONV - 1), 0), (0, 0)))[None]
            ydn_p, sfin_p = deltanet(proj_p, cw, conv_hist_p, sfin_s[bs:], ab, nw, bp, lp, CHUNK,
                                     DN_BLOCKS_PER_STEP)
            hp = matmul_resnorm([ymla_p, ydn_p], [wo_mla, wo_dn], zero_d, g1, hp, RES_TM)

            bc = lambda a: jnp.broadcast_to(a[None], (bp,) + a.shape)
            outs["p_ckv"].append(jnp.concatenate([bc(ckv_s[meta_rows]), ckv_p.reshape(bp, lp, KV_LORA)], axis=1))
            outs["p_kr"].append(jnp.concatenate([bc(kro_s[meta_rows]), kro_p.reshape(bp, lp, QK_ROPE)], axis=1))
            outs["p_s"].append(sfin_p)
            outs["p_conv"].append(proj_p.reshape(bp, lp, PROJ_W)[:, lp - (DN_CONV - 1):, COL_QKV:COL_QKV + DN_QKV])
            outs["s_ckv"].append(ckv_s[:bs * ls].reshape(bs, ls, KV_LORA))
            outs["s_kr"].append(kro_s[:bs * ls].reshape(bs, ls, QK_ROPE))
            outs["s_s"].append(sfin_s[:bs])
            xqkv_s = proj_s[:bs * ls, COL_QKV:COL_QKV + DN_QKV].reshape(bs, ls, DN_QKV)
            outs["s_conv"].append(jnp.concatenate([state_dn_conv[e].astype(F32), xqkv_s], axis=1)[:, -(DN_CONV - 1):])
        else:
            o = layer // 2
            w1 = cf_w_pw1[o].astype(BF16)
            b1 = _row(cf_b_pw1[o])
            wdw = jnp.broadcast_to(cf_w_dw[o].astype(F32)[:, None, :], (CF_KERNEL, 8, d))
            bdw, lg, lb = _row(cf_b_dw[o]), _row(cf_ln_g[o]), _row(cf_ln_b[o])
            w2 = cf_w_pw2[o].astype(BF16)
            b2 = _row(cf_b_pw2[o])
            keep = CF_KERNEL - 1

            hist_s = jnp.pad(state_cf_conv[o].astype(F32), ((0, 1), (CF_HALO - keep, 0), (0, 0)))
            c_s, tail_s = glu_conv(hs, g0, w1, b1, hist_s, wdw, bdw, ns, ls, GLU_TM, GLU_TN, GLU_CHUNK)
            hs = ln_matmul_resnorm(c_s, lg, lb, w2, b2, g1, hs, RES_TM, LN_CHUNK)

            c_p, tail_p = glu_conv(hp, g0, w1, b1, tail_s[bs:], wdw, bdw, bp, lp, GLU_TM, GLU_TN, GLU_CHUNK)
            hp = ln_matmul_resnorm(c_p, lg, lb, w2, b2, g1, hp, RES_TM, LN_CHUNK)

            outs["p_cf"].append(tail_p[:, CF_HALO - keep:])
            outs["s_cf"].append(tail_s[:bs, CF_HALO - keep:])
        hs = ffn(hs, g2, wg_all, wu_all, wd_all, g3, layer, FFN_TM, FFN_TF)
        hp = ffn(hp, g2, wg_all, wu_all, wd_all, g3, layer, FFN_TM, FFN_TF)

    y_prompt = hp.reshape(bp, lp, d)
    y_sample = hs[:bs * ls].reshape(bs, ls, d)
    st = lambda k: jnp.stack(outs[k])
    return (y_prompt, y_sample, st("p_ckv"), st("p_kr"), st("p_s"), st("p_conv"), st("p_cf"),
            st("s_ckv"), st("s_kr"), st("s_s"), st("s_conv"), st("s_cf"))
```

```python
import functools

import numpy as np
import jax
import jax.numpy as jnp
from jax import lax
from jax.experimental import pallas as pl
from jax.experimental.pallas import tpu as pltpu

F32 = jnp.float32
BF16 = jnp.bfloat16

EPS = 1e-6
NEG_INF = -1e30
CHUNK = 64
N_META = 16
MLA_HEADS = 8
Q_LORA = 512
KV_LORA = 512
QK_NOPE = 128
QK_ROPE = 64
V_HEAD = 128
ROPE_THETA = 10000.0
DN_HEADS = 8
DN_DK = 128
DN_DV = 128
DN_CONV = 4
DN_QKV = DN_HEADS * (2 * DN_DK + DN_DV)
CF_KERNEL = 31

LANE = 128
ATT_TILE = 256
VMEM_LIMIT = 56 * 1024 * 1024

PROJ_TM, PROJ_TN = 1024, 768
MLA_TM = 1024
RES_TM = 512
LN_CHUNK = 512
FFN_TM, FFN_TF = 512, 512
GLU_TM, GLU_TN, GLU_CHUNK = 1024, 512, 512
DN_BLOCKS_PER_STEP = 4

COL_QKV = 0
COL_Z = DN_QKV
COL_QD = COL_Z + DN_HEADS * DN_DV
COL_KVD = COL_QD + Q_LORA
COL_SMALL = COL_KVD + KV_LORA
SMALL_W = 256
PROJ_W = COL_SMALL + SMALL_W


def _cparams(sem):
    return pltpu.CompilerParams(dimension_semantics=sem, vmem_limit_bytes=VMEM_LIMIT)


def _rms(x, g):
    return x * lax.rsqrt(jnp.mean(x * x, axis=-1, keepdims=True) + EPS) * g


def _sigmoid(x):
    return 1.0 / (1.0 + jnp.exp(-x))


def _dot(a, b):
    return jnp.dot(a, b, preferred_element_type=F32)


def _dot_nt(a, b):
    return lax.dot_general(a, b, (((1,), (1,)), ((), ())), preferred_element_type=F32)


def _dot_tn(a, b):
    return lax.dot_general(a, b, (((0,), (0,)), ((), ())), preferred_element_type=F32)


def _row_tile(m, pref):
    t = min(pref, m)
    while m % t:
        t //= 2
    return t


def _norm_mm_kernel(x_ref, g_ref, w_ref, b_ref, o_ref, xn_ref):
    @pl.when(pl.program_id(1) == 0)
    def _():
        xn_ref[...] = _rms(x_ref[...], g_ref[...]).astype(BF16)

    o_ref[...] = (_dot(xn_ref[...], w_ref[...]) + b_ref[...]).astype(o_ref.dtype)


CF_HALO = 32


def _glu_conv_kernel(x_ref, g_ref, wa_ref, wg_ref, ba_ref, bg_ref, hist_ref, cw_ref, cb_ref,
                     c_ref, tail_ref, xn_ref, ext_ref, sh_ref, carry_ref, *, tm, chunk, tiles_per_seq,
                     col_major):
    first = CF_HALO - (CF_KERNEL - 1)
    sub = 8
    span = chunk + CF_HALO - sub

    if col_major:
        i = pl.program_id(1)
        j = pl.program_id(0)
        xn_ref[...] = _rms(x_ref[...], g_ref[...]).astype(BF16)
    else:
        i = pl.program_id(0)
        j = pl.program_id(1)

        @pl.when(j == 0)
        def _():
            xn_ref[...] = _rms(x_ref[...], g_ref[...]).astype(BF16)

    seq_start = (i % tiles_per_seq) == 0

    @pl.when(seq_start)
    def _():
        ext_ref[0:CF_HALO, :] = hist_ref[...]

    @pl.when(jnp.logical_not(seq_start))
    def _():
        ext_ref[0:CF_HALO, :] = carry_ref[j]

    def glu_chunk(c):
        lo = c * chunk
        xc = xn_ref[lo:lo + chunk, :]
        a = _dot(xc, wa_ref[...]) + ba_ref[...]
        gt = _dot(xc, wg_ref[...]) + bg_ref[...]
        ext_ref[CF_HALO + lo:CF_HALO + lo + chunk, :] = a * _sigmoid(gt)

    nchunk = tm // chunk
    glu_chunk(0)
    for c in range(nchunk):
        lo = c * chunk
        if c + 1 < nchunk:
            glu_chunk(c + 1)
        for sft in range(1, sub):
            sh_ref[sft - 1, 0:span, :] = ext_ref[lo + sft:lo + sft + span, :]
        rg = min(chunk, 4 * sub)
        for r0 in range(0, chunk, rg):
            acc = None
            for t in range(CF_KERNEL):
                sft = (first + t) % sub
                base = first + t - sft + r0
                if sft == 0:
                    src = ext_ref[lo + base:lo + base + rg, :]
                else:
                    src = sh_ref[sft - 1, base:base + rg, :]
                term = src.reshape(rg // sub, sub, src.shape[1]) * cw_ref[t]
                acc = term if acc is None else acc + term
            c_ref[lo + r0:lo + r0 + rg, :] = acc.reshape(rg, acc.shape[2]) + cb_ref[...]
    tail = ext_ref[tm:tm + CF_HALO, :]
    carry_ref[j] = tail
    tail_ref[...] = tail


def norm_matmul(x, g, w, b, tm, tn, out_dtype=F32):
    m, k = x.shape
    n = w.shape[1]
    tm = _row_tile(m, tm)
    return pl.pallas_call(
        _norm_mm_kernel,
        grid=(m // tm, n // tn),
        in_specs=[pl.BlockSpec((tm, k), lambda i, j: (i, 0)),
                  pl.BlockSpec((1, k), lambda i, j: (0, 0)),
                  pl.BlockSpec((k, tn), lambda i, j: (0, j)),
                  pl.BlockSpec((1, tn), lambda i, j: (0, j))],
        out_specs=pl.BlockSpec((tm, tn), lambda i, j: (i, j)),
        out_shape=jax.ShapeDtypeStruct((m, n), out_dtype),
        scratch_shapes=[pltpu.VMEM((tm, k), BF16)],
        compiler_params=_cparams(("parallel", "arbitrary")),
        name="norm_matmul",
    )(x, g, w, b)


def glu_conv(x, g, w, b, hist, cw, cb, nseq, seq_len, tm, tn, chunk):
    m, k = x.shape
    n = w.shape[1] // 2
    tm = _row_tile(seq_len, tm)
    chunk = min(chunk, tm)
    assert tm % chunk == 0 and chunk % 8 == 0
    tiles_per_seq = seq_len // tm
    nb = n // tn
    nhs = hist.shape[0]
    col_major = tiles_per_seq == 1 and tm < 256
    if col_major:
        grid = (nb, m // tm)
        spec = lambda shape, f: pl.BlockSpec(shape, lambda j, i: f(i, j))
    else:
        grid = (m // tm, nb)
        spec = lambda shape, f: pl.BlockSpec(shape, f)
    hidx = ((lambda i, j: (i // tiles_per_seq, 0, j)) if nhs > 1 else (lambda i, j: (0, 0, j)))
    c, tails = pl.pallas_call(
        functools.partial(_glu_conv_kernel, tm=tm, chunk=chunk, tiles_per_seq=tiles_per_seq,
                          col_major=col_major),
        grid=grid,
        in_specs=[spec((tm, k), lambda i, j: (i, 0)),
                  spec((1, k), lambda i, j: (0, 0)),
                  spec((k, tn), lambda i, j: (0, j)),
                  spec((k, tn), lambda i, j: (0, j + nb)),
                  spec((1, tn), lambda i, j: (0, j)),
                  spec((1, tn), lambda i, j: (0, j + nb)),
                  spec((None, CF_HALO, tn), hidx),
                  spec((cw.shape[0], 8, tn), lambda i, j: (0, 0, j)),
                  spec((1, tn), lambda i, j: (0, j))],
        out_specs=[spec((tm, tn), lambda i, j: (i, j)),
                   spec((None, CF_HALO, tn), lambda i, j: (i, 0, j))],
        out_shape=[jax.ShapeDtypeStruct((m, n), F32),
                   jax.ShapeDtypeStruct((m // tm, CF_HALO, n), F32)],
        scratch_shapes=[pltpu.VMEM((tm, k), BF16),
                        pltpu.VMEM((tm + CF_HALO, tn), F32),
                        pltpu.VMEM((7, chunk + CF_HALO - 8, tn), F32),
                        pltpu.VMEM((nb, CF_HALO, tn), F32)],
        compiler_params=_cparams(("arbitrary", "arbitrary")),
        name="glu_conv",
    )(x, g, w, w, b, b, hist, cw, cb)
    return c, tails.reshape(nseq, tiles_per_seq, CF_HALO, n)[:, -1]


def _mm_resnorm_kernel(*refs, n_in):
    xs = refs[:n_in]
    ws = refs[n_in:2 * n_in]
    b_ref, g_ref, res_ref, o_ref = refs[2 * n_in:]
    y = b_ref[...]
    for x_ref, w_ref in zip(xs, ws):
        y = y + _dot(x_ref[...], w_ref[...])
    o_ref[...] = res_ref[...] + _rms(y, g_ref[...])


def _ln_mm_resnorm_kernel(c_ref, lg_ref, lb_ref, w_ref, b_ref, g_ref, res_ref, o_ref, *, tm, chunk):
    for c in range(tm // chunk):
        rows = slice(c * chunk, (c + 1) * chunk)
        x = c_ref[rows, :]
        xc = x - jnp.mean(x, axis=-1, keepdims=True)
        y = xc * lax.rsqrt(jnp.mean(xc * xc, axis=-1, keepdims=True) + EPS) * lg_ref[...] + lb_ref[...]
        a = (y * _sigmoid(y)).astype(BF16)
        z = _dot(a, w_ref[...]) + b_ref[...]
        o_ref[rows, :] = res_ref[rows, :] + _rms(z, g_ref[...])


def ln_matmul_resnorm(c, lg, lb, w, b, g, res, tm, chunk):
    m, n = res.shape
    k = c.shape[1]
    tm = _row_tile(m, tm)
    chunk = min(chunk, tm)
    assert tm % chunk == 0
    vec = lambda width: pl.BlockSpec((1, width), lambda i: (0, 0))
    return pl.pallas_call(
        functools.partial(_ln_mm_resnorm_kernel, tm=tm, chunk=chunk),
        grid=(m // tm,),
        in_specs=[pl.BlockSpec((tm, k), lambda i: (i, 0)), vec(k), vec(k),
                  pl.BlockSpec(w.shape, lambda i: (0, 0)), vec(n), vec(n),
                  pl.BlockSpec((tm, n), lambda i: (i, 0))],
        out_specs=pl.BlockSpec((tm, n), lambda i: (i, 0)),
        out_shape=jax.ShapeDtypeStruct((m, n), F32),
        compiler_params=_cparams(("parallel",)),
        name="ln_matmul_resnorm",
    )(c, lg, lb, w, b, g, res)


def matmul_resnorm(xs, ws, b, g, res, tm):
    m, n = res.shape
    tm = _row_tile(m, tm)
    n_in = len(xs)
    in_specs = ([pl.BlockSpec((tm, x.shape[1]), lambda i: (i, 0)) for x in xs]
                + [pl.BlockSpec(w.shape, lambda i: (0, 0)) for w in ws]
                + [pl.BlockSpec((1, n), lambda i: (0, 0)),
                   pl.BlockSpec((1, n), lambda i: (0, 0)),
                   pl.BlockSpec((tm, n), lambda i: (i, 0))])
    return pl.pallas_call(
        functools.partial(_mm_resnorm_kernel, n_in=n_in),
        grid=(m // tm,),
        in_specs=in_specs,
        out_specs=pl.BlockSpec((tm, n), lambda i: (i, 0)),
        out_shape=jax.ShapeDtypeStruct((m, n), F32),
        compiler_params=_cparams(("parallel",)),
        name="matmul_resnorm",
    )(*xs, *ws, b, g, res)


def _ffn_kernel(h_ref, g2_ref, wg_ref, wu_ref, wd_ref, g3_ref, o_ref, xn_ref, acc_ref):
    j = pl.program_id(1)

    @pl.when(j == 0)
    def _():
        xn_ref[...] = _rms(h_ref[...], g2_ref[...]).astype(BF16)
        acc_ref[...] = jnp.zeros_like(acc_ref)

    xn = xn_ref[...]
    gate = _dot(xn, wg_ref[...])
    up = _dot(xn, wu_ref[...])
    a = (gate * _sigmoid(gate) * up).astype(BF16)
    acc_ref[...] += _dot(a, wd_ref[...])

    @pl.when(j == pl.num_programs(1) - 1)
    def _():
        o_ref[...] = h_ref[...] + _rms(acc_ref[...], g3_ref[...])


def ffn(h, g2, wg, wu, wd, g3, layer, tm, tf):
    m, d = h.shape
    f = wg.shape[2]
    tm = _row_tile(m, tm)
    return pl.pallas_call(
        _ffn_kernel,
        grid=(m // tm, f // tf),
        in_specs=[pl.BlockSpec((tm, d), lambda i, j: (i, 0)),
                  pl.BlockSpec((1, d), lambda i, j: (0, 0)),
                  pl.BlockSpec((None, d, tf), lambda i, j: (layer, 0, j)),
                  pl.BlockSpec((None, d, tf), lambda i, j: (layer, 0, j)),
                  pl.BlockSpec((None, tf, d), lambda i, j: (layer, j, 0)),
                  pl.BlockSpec((1, d), lambda i, j: (0, 0))],
        out_specs=pl.BlockSpec((tm, d), lambda i, j: (i, 0)),
        out_shape=jax.ShapeDtypeStruct((m, d), F32),
        scratch_shapes=[pltpu.VMEM((tm, d), BF16), pltpu.VMEM((tm, d), F32)],
        compiler_params=_cparams(("parallel", "arbitrary")),
        name="ffn",
    )(h, g2, wg, wu, wd, g3)


def _mla_prep_kernel(qd_ref, kvd_ref, sm_ref, gq_ref, gkv_ref, wq_ref, wk_ref, wv_ref, cq_ref, sq_ref, ck_ref,
                     qn_ref, qr_ref, ckv_ref, kn_ref, v_ref, kro_ref, krp_ref, *, v_transposed):
    hw = MLA_HEADS * QK_NOPE
    scale = (QK_NOPE + QK_ROPE) ** -0.5
    cq = _rms(qd_ref[...], gq_ref[...]).astype(BF16)
    q = _dot(cq, wq_ref[...])
    qn_ref[...] = (q[:, :hw] * scale).astype(BF16)
    cos8 = jnp.tile(cq_ref[...], (1, MLA_HEADS))
    sin8 = jnp.tile(sq_ref[...], (1, MLA_HEADS))
    qr_ref[...] = ((q[:, hw:2 * hw] * cos8 + q[:, 2 * hw:] * sin8) * scale).astype(BF16)
    ckv = _rms(kvd_ref[...], gkv_ref[...])
    ckv_ref[...] = ckv
    ckv_b = ckv.astype(BF16)
    kn_ref[...] = _dot(ckv_b, wk_ref[...]).astype(BF16)
    if v_transposed:
        vt = _dot_nt(wv_ref[...], ckv_b).astype(BF16)
        for t in range(v_ref.shape[0]):
            v_ref[t] = vt[:, t * ATT_TILE:(t + 1) * ATT_TILE]
    else:
        v_ref[...] = _dot(ckv_b, wv_ref[...]).astype(BF16)
    y = sm_ref[:, :LANE] * ck_ref[...]
    kro = y + pltpu.roll(y, QK_ROPE, 1)
    kro_ref[...] = kro[:, :QK_ROPE]
    lane = lax.broadcasted_iota(jnp.int32, kro.shape, 1)
    krp_ref[...] = jnp.where(lane < QK_ROPE, kro, 0.0).astype(BF16)


def mla_prep(proj, gq, gkv, wq, wk, wv, cq_tab, sq_tab, ck_tab, tm, v_transposed):
    m = proj.shape[0]
    tm = _row_tile(min(m, cq_tab.shape[0]), tm)
    nt = cq_tab.shape[0] // tm
    hw = MLA_HEADS * QK_NOPE
    tab = lambda: pl.BlockSpec((tm, LANE), lambda i: (i % nt, 0))
    full = lambda a: pl.BlockSpec(a.shape, lambda i: (0, 0))
    if v_transposed:
        assert tm % ATT_TILE == 0
        v_spec = pl.BlockSpec((tm // ATT_TILE, hw, ATT_TILE), lambda i: (i, 0, 0))
        v_shape = jax.ShapeDtypeStruct((m // ATT_TILE, hw, ATT_TILE), BF16)
    else:
        v_spec = pl.BlockSpec((tm, hw), lambda i: (i, 0))
        v_shape = jax.ShapeDtypeStruct((m, hw), BF16)
    return pl.pallas_call(
        functools.partial(_mla_prep_kernel, v_transposed=v_transposed),
        grid=(m // tm,),
        in_specs=[pl.BlockSpec((tm, Q_LORA), lambda i: (i, COL_QD // Q_LORA)),
                  pl.BlockSpec((tm, KV_LORA), lambda i: (i, COL_KVD // KV_LORA)),
                  pl.BlockSpec((tm, SMALL_W), lambda i: (i, COL_SMALL // SMALL_W)),
                  full(gq), full(gkv), full(wq), full(wk), full(wv), tab(), tab(), tab()],
        out_specs=[pl.BlockSpec((tm, hw), lambda i: (i, 0)),
                   pl.BlockSpec((tm, hw), lambda i: (i, 0)),
                   pl.BlockSpec((tm, KV_LORA), lambda i: (i, 0)),
                   pl.BlockSpec((tm, hw), lambda i: (i, 0)),
                   v_spec,
                   pl.BlockSpec((tm, QK_ROPE), lambda i: (i, 0)),
                   pl.BlockSpec((tm, LANE), lambda i: (i, 0))],
        out_shape=[jax.ShapeDtypeStruct((m, hw), BF16),
                   jax.ShapeDtypeStruct((m, hw), BF16),
                   jax.ShapeDtypeStruct((m, KV_LORA), F32),
                   jax.ShapeDtypeStruct((m, hw), BF16),
                   v_shape,
                   jax.ShapeDtypeStruct((m, QK_ROPE), F32),
                   jax.ShapeDtypeStruct((m, LANE), BF16)],
        compiler_params=_cparams(("parallel",)),
        name="mla_prep",
    )(proj, proj, proj, gq, gkv, wq, wk, wv, cq_tab, sq_tab, ck_tab)


def _kv_up_kernel(ckv_ref, wkv_ref, kn_ref, v_ref):
    hw = MLA_HEADS * QK_NOPE
    kv = _dot(ckv_ref[...].astype(BF16), wkv_ref[...])
    kn_ref[...] = kv[:, :hw].astype(BF16)
    v_ref[...] = kv[:, hw:].astype(BF16)


def kv_up(ckv, wkv, tm):
    m = ckv.shape[0]
    tm = _row_tile(m, tm)
    hw = MLA_HEADS * QK_NOPE
    return pl.pallas_call(
        _kv_up_kernel,
        grid=(m // tm,),
        in_specs=[pl.BlockSpec((tm, KV_LORA), lambda i: (i, 0)),
                  pl.BlockSpec(wkv.shape, lambda i: (0, 0))],
        out_specs=[pl.BlockSpec((tm, hw), lambda i: (i, 0)),
                   pl.BlockSpec((tm, hw), lambda i: (i, 0))],
        out_shape=[jax.ShapeDtypeStruct((m, hw), BF16), jax.ShapeDtypeStruct((m, hw), BF16)],
        compiler_params=_cparams(("parallel",)),
        name="kv_up",
    )(ckv, wkv)


def _attn_kernel(hlen_ref, qn_ref, qr_ref, kn_ref, kr_ref, v_ref, hkn_ref, hkr_ref, hv_ref, o_ref,
                 m_ref, l_ref, acc_ref, a_ref, qc_ref, s_ref, p_ref, hs_ref, hp_ref, *, tile, lh):
    s_idx = pl.program_id(0)
    qt = pl.program_id(1)
    hlen = hlen_ref[s_idx]
    nh = MLA_HEADS
    heads = [slice(h * LANE, (h + 1) * LANE) for h in range(nh)]

    for h, hs in enumerate(heads):
        qc_ref[h] = jnp.concatenate([qn_ref[:, hs], qr_ref[:, hs]], axis=1)

    def process(get_k, get_v, s_scr, p_scr, valid, first):
        for h in range(nh):
            s_scr[h] = _dot_nt(qc_ref[h], get_k(h))
        for h, hs in enumerate(heads):
            s = s_scr[h]
            if valid is not None:
                s = jnp.where(valid, s, NEG_INF)
            m_cur = jnp.max(s, axis=1, keepdims=True)
            if first:
                m_new = jnp.broadcast_to(m_cur, (tile, LANE))
            else:
                m_prev = m_ref[:, hs]
                m_new = jnp.maximum(m_prev, m_cur)
            p = jnp.exp(s - m_new[:, :1])
            if valid is not None:
                p = jnp.where(valid, p, 0.0)
            p_scr[h] = p.astype(BF16)
            l_cur = jnp.sum(p, axis=1, keepdims=True)
            if first:
                l_ref[:, hs] = jnp.broadcast_to(l_cur, (tile, LANE))
            else:
                alpha = jnp.exp(m_prev - m_new)
                a_ref[:, hs] = alpha
                l_ref[:, hs] = alpha * l_ref[:, hs] + l_cur
            m_ref[:, hs] = m_new
        for h, hs in enumerate(heads):
            pv = _dot(p_scr[h], get_v(h))
            if first:
                acc_ref[:, hs] = pv
            else:
                acc_ref[:, hs] = a_ref[:, hs] * acc_ref[:, hs] + pv

    hvalid = lax.broadcasted_iota(jnp.int32, (tile, lh), 1) < hlen
    process(lambda h: jnp.concatenate([hkn_ref[:, heads[h]], hkr_ref[...]], axis=1),
            lambda h: hv_ref[:, heads[h]], hs_ref, hp_ref, hvalid, True)

    def own_tile(kt, valid):
        rows = pl.ds(pl.multiple_of(kt * tile, tile), tile)
        process(lambda h: jnp.concatenate([kn_ref[rows, heads[h]], kr_ref[rows, :]], axis=1),
                lambda h: v_ref[rows, heads[h]], s_ref, p_ref, valid, False)

    def body(kt, carry):
        own_tile(kt, None)
        return carry

    lax.fori_loop(0, qt, body, 0)
    if tile > CHUNK:
        row = lax.broadcasted_iota(jnp.int32, (tile, tile), 0) // CHUNK
        col = lax.broadcasted_iota(jnp.int32, (tile, tile), 1) // CHUNK
        dvalid = col <= row
    else:
        dvalid = None
    own_tile(qt, dvalid)

    for hs in heads:
        o_ref[:, hs] = (acc_ref[:, hs] / l_ref[:, hs]).astype(o_ref.dtype)


def attention(qn, qr, kn, krp, v, hkn, hkrp, hv, hlen, nseq, seq_len, tile):
    hw = MLA_HEADS * LANE
    tile = min(tile, seq_len)
    assert seq_len % tile == 0 and (tile % CHUNK == 0 or seq_len == tile <= CHUNK)
    nqt = seq_len // tile
    nhs, lh = hkn.shape[0], hkn.shape[1]
    assert nhs in (1, nseq)
    hidx = (lambda s, q, hl: (s, 0, 0)) if nhs > 1 else (lambda s, q, hl: (0, 0, 0))
    grid_spec = pltpu.PrefetchScalarGridSpec(
        num_scalar_prefetch=1,
        grid=(nseq, nqt),
        in_specs=[pl.BlockSpec((tile, hw), lambda s, q, hl: (s * nqt + q, 0)),
                  pl.BlockSpec((tile, hw), lambda s, q, hl: (s * nqt + q, 0)),
                  pl.BlockSpec((seq_len, hw), lambda s, q, hl: (s, 0)),
                  pl.BlockSpec((seq_len, LANE), lambda s, q, hl: (s, 0)),
                  pl.BlockSpec((seq_len, hw), lambda s, q, hl: (s, 0)),
                  pl.BlockSpec((None, lh, hw), hidx),
                  pl.BlockSpec((None, lh, LANE), hidx),
                  pl.BlockSpec((None, lh, hw), hidx)],
        out_specs=pl.BlockSpec((tile, hw), lambda s, q, hl: (s * nqt + q, 0)),
        scratch_shapes=[pltpu.VMEM((tile, hw), F32), pltpu.VMEM((tile, hw), F32),
                        pltpu.VMEM((tile, hw), F32), pltpu.VMEM((tile, hw), F32),
                        pltpu.VMEM((MLA_HEADS, tile, 2 * LANE), BF16),
                        pltpu.VMEM((MLA_HEADS, tile, tile), F32),
                        pltpu.VMEM((MLA_HEADS, tile, tile), BF16),
                        pltpu.VMEM((MLA_HEADS, tile, lh), F32),
                        pltpu.VMEM((MLA_HEADS, tile, lh), BF16)],
    )
    return pl.pallas_call(
        functools.partial(_attn_kernel, tile=tile, lh=lh),
        grid_spec=grid_spec,
        out_shape=jax.ShapeDtypeStruct((nseq * seq_len, hw), BF16),
        compiler_params=_cparams(("parallel", "arbitrary")),
        name="attention",
    )(hlen, qn, qr, kn, krp, v, hkn, hkrp, hv)


def _attn_t_kernel(hlen_ref, qn_ref, qr_ref, kn_ref, kr_ref, vt_ref, hkn_ref, hkr_ref, hvt_ref, o_ref,
                   m_ref, l_ref, a_ref, acc_ref, qc_ref, s_ref, p_ref, s2_ref, p2_ref, hs_ref, hp_ref,
                   *, tile, lh):
    s_idx = pl.program_id(0)
    qt = pl.program_id(1)
    hlen = hlen_ref[s_idx]
    nh = MLA_HEADS
    heads = [slice(h * LANE, (h + 1) * LANE) for h in range(nh)]

    for h, hs in enumerate(heads):
        qc_ref[h] = jnp.concatenate([qn_ref[:, hs], qr_ref[:, hs]], axis=1)

    def process(get_k, get_vt, s_scr, p_scr, valid, first):
        for h in range(nh):
            s_scr[h] = _dot_nt(get_k(h), qc_ref[h])
        for h in range(nh):
            for q0 in range(0, tile, LANE):
                qs = slice(q0, q0 + LANE)
                s = s_scr[h, :, qs]
                ok = None if valid is None else valid(q0)
                if ok is not None:
                    s = jnp.where(ok, s, NEG_INF)
                m_cur = jnp.max(s, axis=0, keepdims=True)
                if first:
                    m_new = m_cur
                else:
                    m_prev = m_ref[h, :, qs]
                    m_new = jnp.maximum(m_prev, m_cur)
                p = jnp.exp(s - m_new)
                if ok is not None:
                    p = jnp.where(ok, p, 0.0)
                p_scr[h, :, qs] = p.astype(BF16)
                l_cur = jnp.sum(p, axis=0, keepdims=True)
                if first:
                    l_ref[h, :, qs] = l_cur
                else:
                    alpha = jnp.exp(m_prev - m_new)
                    a_ref[h, :, qs] = alpha
                    l_ref[h, :, qs] = alpha * l_ref[h, :, qs] + l_cur
                m_ref[h, :, qs] = m_new
        for h in range(nh):
            pv = _dot(get_vt(h), p_scr[h])
            if first:
                acc_ref[h] = pv
            else:
                acc_ref[h] = a_ref[h] * acc_ref[h] + pv

    hvalid = lax.broadcasted_iota(jnp.int32, (lh, LANE), 0) < hlen
    process(lambda h: jnp.concatenate([hkn_ref[:, heads[h]], hkr_ref[...]], axis=1),
            lambda h: hvt_ref[heads[h], :], hs_ref, hp_ref, lambda q0: hvalid, True)

    def own_tile(kt, valid):
        rows = pl.ds(pl.multiple_of(kt * tile, tile), tile)
        process(lambda h: jnp.concatenate([kn_ref[rows, heads[h]], kr_ref[rows, :]], axis=1),
                lambda h: vt_ref[kt, heads[h], :], s_ref, p_ref, valid, False)

    def own_pair(kp):
        rows = pl.ds(pl.multiple_of(kp * (2 * tile), 2 * tile), 2 * tile)
        process(lambda h: jnp.concatenate([kn_ref[rows, heads[h]], kr_ref[rows, :]], axis=1),
                lambda h: jnp.concatenate([vt_ref[2 * kp, heads[h], :], vt_ref[2 * kp + 1, heads[h], :]],
                                          axis=1),
                s2_ref, p2_ref, None, False)

    def body(kp, carry):
        own_pair(kp)
        return carry

    lax.fori_loop(0, qt // 2, body, 0)

    @pl.when(qt % 2 == 1)
    def _():
        own_tile(qt - 1, None)

    key_chunk = lax.broadcasted_iota(jnp.int32, (tile, LANE), 0) // CHUNK
    qry_lane = lax.broadcasted_iota(jnp.int32, (tile, LANE), 1)
    own_tile(qt, lambda q0: key_chunk <= (qry_lane + q0) // CHUNK)

    for h, hs in enumerate(heads):
        o_ref[:, hs] = (acc_ref[h] / l_ref[h]).T.astype(o_ref.dtype)


def attention_t(qn, qr, kn, krp, vt, hkn, hkrp, hvt, hlen, nseq, seq_len):
    hw = MLA_HEADS * LANE
    tile = ATT_TILE
    assert seq_len % tile == 0 and tile % CHUNK == 0
    nqt = seq_len // tile
    lh = hkn.shape[0]
    nh = MLA_HEADS
    grid_spec = pltpu.PrefetchScalarGridSpec(
        num_scalar_prefetch=1,
        grid=(nseq, nqt),
        in_specs=[pl.BlockSpec((tile, hw), lambda s, q, hl: (s * nqt + q, 0)),
                  pl.BlockSpec((tile, hw), lambda s, q, hl: (s * nqt + q, 0)),
                  pl.BlockSpec((seq_len, hw), lambda s, q, hl: (s, 0)),
                  pl.BlockSpec((seq_len, LANE), lambda s, q, hl: (s, 0)),
                  pl.BlockSpec((nqt, hw, tile), lambda s, q, hl: (s, 0, 0)),
                  pl.BlockSpec((lh, hw), lambda s, q, hl: (0, 0)),
                  pl.BlockSpec((lh, LANE), lambda s, q, hl: (0, 0)),
                  pl.BlockSpec((hw, lh), lambda s, q, hl: (0, 0))],
        out_specs=pl.BlockSpec((tile, hw), lambda s, q, hl: (s * nqt + q, 0)),
        scratch_shapes=[pltpu.VMEM((nh, 1, tile), F32), pltpu.VMEM((nh, 1, tile), F32),
                        pltpu.VMEM((nh, 1, tile), F32),
                        pltpu.VMEM((nh, V_HEAD, tile), F32),
                        pltpu.VMEM((nh, tile, 2 * LANE), BF16),
                        pltpu.VMEM((nh, tile, tile), F32),
                        pltpu.VMEM((nh, tile, tile), BF16),
                        pltpu.VMEM((nh, 2 * tile, tile), F32),
                        pltpu.VMEM((nh, 2 * tile, tile), BF16),
                        pltpu.VMEM((nh, lh, tile), F32),
                        pltpu.VMEM((nh, lh, tile), BF16)],
    )
    return pl.pallas_call(
        functools.partial(_attn_t_kernel, tile=tile, lh=lh),
        grid_spec=grid_spec,
        out_shape=jax.ShapeDtypeStruct((nseq * seq_len, hw), BF16),
        compiler_params=_cparams(("parallel", "arbitrary")),
        name="attention_t",
    )(hlen, qn, qr, kn, krp, vt, hkn, hkrp, hvt)


def _dn_prep_kernel(x_ref, sm_ref, cw_ref, hist_ref, ab_ref,
                    u_ref, w_ref, qd_ref, kd_ref, attn_ref, gl_ref,
                    ext_ref, act_ref, lhs_ref, kb_ref, dec_ref, rhs_ref, mm_ref, t_ref, xs_ref, *, c, g):
    j = pl.program_id(1)
    nh, dk, dv = DN_HEADS, DN_DK, DN_DV
    kw = nh * dk
    rows_all = c * g
    inst = [(b, h) for b in range(g) for h in range(nh)]

    @pl.when(j == 0)
    def _():
        ext_ref[0:8, :] = hist_ref[...]

    ext_ref[8:8 + rows_all, :] = x_ref[...]
    conv = cw_ref[0:1, :] * ext_ref[5:5 + rows_all, :]
    for t in range(1, DN_CONV):
        conv = conv + cw_ref[t:t + 1, :] * ext_ref[5 + t:5 + t + rows_all, :]
    ext_ref[0:8, :] = ext_ref[rows_all:rows_all + 8, :]
    act_ref[...] = conv * _sigmoid(conv)

    gates = sm_ref[:, LANE:2 * LANE]
    xa = gates + ab_ref[1:2, :]
    softplus = jnp.maximum(xa, 0.0) + jnp.log(1.0 + jnp.exp(-jnp.abs(xa)))
    g_all = -jnp.exp(ab_ref[0:1, :]) * softplus
    beta_all = _sigmoid(gates)

    r = lax.broadcasted_iota(jnp.int32, (c, c), 0)
    q = lax.broadcasted_iota(jnp.int32, (c, c), 1)
    incl = r >= q
    strict = r > q
    eye = jnp.where(r == q, 1.0, 0.0)
    tri = jnp.where(incl, 1.0, 0.0)

    for b in range(g):
        rows = slice(b * c, (b + 1) * c)
        gc = jnp.dot(tri, g_all[rows], preferred_element_type=F32, precision=lax.Precision.HIGHEST)
        if c < LANE:
            gc_sq = jnp.concatenate([gc, jnp.zeros((LANE - c, LANE), F32)], axis=0)
        else:
            gc_sq = gc
        gc_t = gc_sq.T
        egc = jnp.exp(gc)
        glast = gc[c - 1:c, :]
        edl = jnp.exp(glast - gc)
        gl_ref[rows, :] = jnp.broadcast_to(glast, (c, LANE))
        beta_b = beta_all[rows]
        for h in range(nh):
            i = b * nh + h
            hs = slice(h * dk, (h + 1) * dk)
            qh = act_ref[rows, hs]
            kh = act_ref[rows, kw + h * dk:kw + (h + 1) * dk]
            vh = act_ref[rows, 2 * kw + h * dv:2 * kw + (h + 1) * dv]
            qh = qh * (lax.rsqrt(jnp.sum(qh * qh, axis=1, keepdims=True) + EPS) * dk ** -0.5)
            kh = kh * lax.rsqrt(jnp.sum(kh * kh, axis=1, keepdims=True) + EPS)
            bcol = beta_b[:, 8 + h:9 + h]
            gcol = gc[:, h:h + 1]
            grow = gc_t[h:h + 1, :c]
            dec_ref[i] = jnp.where(incl, jnp.exp(jnp.where(incl, gcol - grow, 0.0)), 0.0)
            kb = kh * bcol
            ecol = egc[:, h:h + 1]
            lhs_ref[i] = jnp.concatenate([kb, qh], axis=0).astype(BF16)
            kb_ref[i] = kh.astype(BF16)
            rhs_ref[i] = jnp.concatenate([vh * bcol, kb * ecol], axis=1).astype(BF16)
            qd_ref[rows, hs] = (qh * ecol).astype(BF16)
            kd_ref[rows, hs] = (kh * edl[:, h:h + 1]).astype(BF16)

    for i, (b, h) in enumerate(inst):
        kq = _dot_nt(lhs_ref[i], kb_ref[i])
        dec = dec_ref[i]
        mm = jnp.where(strict, kq[:c] * dec, 0.0)
        mm_ref[i] = mm
        t_ref[i] = eye - jnp.where((r ^ q) == 1, mm, 0.0)
        attn_ref[b, h] = (kq[c:] * dec).astype(BF16)

    s = 2
    while s < c:
        sh = s.bit_length() - 1
        sel = ((r >> sh) ^ (q >> sh)) == 1
        for i in range(len(inst)):
            e = jnp.where(sel, mm_ref[i], 0.0).astype(BF16)
            xs_ref[i] = _dot(e, t_ref[i].astype(BF16)).astype(BF16)
        for i in range(len(inst)):
            t = t_ref[i]
            t_ref[i] = t - _dot(t.astype(BF16), xs_ref[i])
        s *= 2

    for i, (b, h) in enumerate(inst):
        rows = slice(b * c, (b + 1) * c)
        hs = slice(h * dk, (h + 1) * dk)
        uw = _dot(t_ref[i].astype(BF16), rhs_ref[i])
        u_ref[rows, hs] = uw[:, :dv]
        w_ref[rows, hs] = uw[:, dv:].astype(BF16)


def _dn_scan_kernel(u_ref, w_ref, qd_ref, kd_ref, attn_ref, gl_ref, z_ref, s0_ref, nw_ref,
                    y_ref, sfin_ref, s_ref, rr_ref, vn_ref, *, c, nseq, shared_s0):
    j = pl.program_id(0)
    nh, dk = DN_HEADS, DN_DK
    inst = [(s, h) for s in range(nseq) for h in range(nh)]
    heads = [slice(h * dk, (h + 1) * dk) for h in range(nh)]

    @pl.when(j == 0)
    def _():
        for s in range(nseq):
            s_ref[s] = s0_ref[0 if shared_s0 else s]

    for i, (s, h) in enumerate(inst):
        lhs = jnp.concatenate([w_ref[s, :, heads[h]], qd_ref[s, :, heads[h]]], axis=0)
        rr_ref[i] = _dot(lhs, s_ref[s, h].astype(BF16))
    for i, (s, h) in enumerate(inst):
        vn_ref[i] = (u_ref[s, :, heads[h]] - rr_ref[i, 0:c, :]).astype(BF16)
    for i, (s, h) in enumerate(inst):
        o = rr_ref[i, c:2 * c, :] + _dot(attn_ref[s, h], vn_ref[i])
        zh = z_ref[s, :, heads[h]]
        y_ref[s, :, heads[h]] = (_rms(o, nw_ref[...]) * (zh * _sigmoid(zh))).astype(y_ref.dtype)
    for i, (s, h) in enumerate(inst):
        ebd = jnp.exp(gl_ref[s, 0:1, h:h + 1])
        s_ref[s, h] = s_ref[s, h] * ebd + _dot_tn(kd_ref[s, :, heads[h]], vn_ref[i])

    @pl.when(j == pl.num_programs(0) - 1)
    def _():
        sfin_ref[...] = s_ref[...]


def deltanet(proj, cw, hist, s0, ab, nw, nseq, seq_len, c, g):
    c = min(c, seq_len)
    nblk = seq_len // c
    g = min(g, nblk)
    assert seq_len % c == 0 and c % 8 == 0 and nblk % g == 0
    ntile = nblk // g
    rows_t = c * g
    nhs = hist.shape[0]
    assert nhs in (1, nseq) and s0.shape[0] == nhs
    hidx3 = (lambda s, j: (s, 0, 0)) if nhs > 1 else (lambda s, j: (0, 0, 0))
    nh, dk, dv = DN_HEADS, DN_DK, DN_DV
    zw = nh * dv
    rows = nseq * seq_len
    ni = g * nh
    row_blk = lambda w: pl.BlockSpec((rows_t, w), lambda s, j: (s * ntile + j, 0))
    u, w, qd, kd, attn, gl = pl.pallas_call(
        functools.partial(_dn_prep_kernel, c=c, g=g),
        grid=(nseq, ntile),
        in_specs=[pl.BlockSpec((rows_t, DN_QKV), lambda s, j: (s * ntile + j, COL_QKV // DN_QKV)),
                  pl.BlockSpec((rows_t, SMALL_W), lambda s, j: (s * ntile + j, COL_SMALL // SMALL_W)),
                  pl.BlockSpec(cw.shape, lambda s, j: (0, 0)),
                  pl.BlockSpec((None, 8, DN_QKV), hidx3),
                  pl.BlockSpec(ab.shape, lambda s, j: (0, 0))],
        out_specs=[row_blk(zw), row_blk(zw), row_blk(zw), row_blk(zw),
                   pl.BlockSpec((g, nh, c, c), lambda s, j: (s * ntile + j, 0, 0, 0)),
                   row_blk(LANE)],
        out_shape=[jax.ShapeDtypeStruct((rows, zw), F32),
                   jax.ShapeDtypeStruct((rows, zw), BF16),
                   jax.ShapeDtypeStruct((rows, zw), BF16),
                   jax.ShapeDtypeStruct((rows, zw), BF16),
                   jax.ShapeDtypeStruct((nseq * nblk, nh, c, c), BF16),
                   jax.ShapeDtypeStruct((rows, LANE), F32)],
        scratch_shapes=[pltpu.VMEM((rows_t + 8, DN_QKV), F32),
                        pltpu.VMEM((rows_t, DN_QKV), F32),
                        pltpu.VMEM((ni, 2 * c, dk), BF16),
                        pltpu.VMEM((ni, c, dk), BF16),
                        pltpu.VMEM((ni, c, c), F32),
                        pltpu.VMEM((ni, c, dk + dv), BF16),
                        pltpu.VMEM((ni, c, c), F32),
                        pltpu.VMEM((ni, c, c), F32),
                        pltpu.VMEM((ni, c, c), BF16)],
        compiler_params=_cparams(("parallel", "arbitrary")),
        name="dn_prep",
    )(proj, proj, cw, hist, ab)

    seq3 = lambda a: a.reshape(nseq, seq_len, a.shape[1])
    blk3 = lambda wdt, col=0: pl.BlockSpec((nseq, c, wdt), lambda j: (0, j, col))
    y, sfin = pl.pallas_call(
        functools.partial(_dn_scan_kernel, c=c, nseq=nseq, shared_s0=nhs == 1),
        grid=(nblk,),
        in_specs=[blk3(zw), blk3(zw), blk3(zw), blk3(zw),
                  pl.BlockSpec((nseq, None, nh, c, c), lambda j: (0, j, 0, 0, 0)),
                  blk3(LANE), blk3(zw, COL_Z // zw),
                  pl.BlockSpec(s0.shape, lambda j: (0, 0, 0, 0)),
                  pl.BlockSpec(nw.shape, lambda j: (0, 0))],
        out_specs=[blk3(zw), pl.BlockSpec((nseq, nh, dk, dv), lambda j: (0, 0, 0, 0))],
        out_shape=[jax.ShapeDtypeStruct((nseq, seq_len, zw), BF16),
                   jax.ShapeDtypeStruct((nseq, nh, dk, dv), F32)],
        scratch_shapes=[pltpu.VMEM((nseq, nh, dk, dv), F32),
                        pltpu.VMEM((nseq * nh, 2 * c, dv), F32),
                        pltpu.VMEM((nseq * nh, c, dv), BF16)],
        compiler_params=_cparams(("arbitrary",)),
        name="dn_scan",
    )(seq3(u), seq3(w), seq3(qd), seq3(kd), attn.reshape(nseq, nblk, nh, c, c), seq3(gl), seq3(proj),
      s0, nw)
    return y.reshape(rows, zw), sfin


def _rope_tables(pos):
    half = QK_ROPE // 2
    inv = ROPE_THETA ** (-jnp.arange(half, dtype=F32) / half)
    ang = pos.astype(F32)[:, None] * inv[None, :]
    cos, sin = jnp.cos(ang), jnp.sin(ang)
    zeros = jnp.zeros((pos.shape[0], LANE - QK_ROPE), F32)
    cq = jnp.concatenate([cos, cos, zeros], axis=1)
    sq = jnp.concatenate([sin, sin, zeros], axis=1)
    ck = jnp.concatenate([cos, cos, sin, sin], axis=1)
    return cq, sq, ck


def _rot_cols(w):
    half = w.shape[-1] // 2
    return jnp.concatenate([-w[..., half:], w[..., :half]], axis=-1)


def _row(v, width=None):
    v = v.astype(F32).reshape(1, -1)
    if width is not None and v.shape[1] < width:
        v = jnp.pad(v, ((0, 0), (0, width - v.shape[1])))
    return v


def kernel(x_prompt, x_sample, cache_mla_ckv, cache_mla_krope, state_dn_s, state_dn_conv, state_cf_conv, meta_tokens, norm_gains, w_in, mla_gq, mla_gkv, w_uq, w_uk, w_uv, dn_conv_w, dn_a_log, dn_dt_bias, dn_norm_w, w_out, cf_w_pw1, cf_b_pw1, cf_w_dw, cf_b_dw, cf_ln_g, cf_ln_b, cf_w_pw2, cf_b_pw2, w_gate, w_up, w_down):
    bp, lp, d = x_prompt.shape
    bs, ls, _ = x_sample.shape
    n_meta = meta_tokens.shape[0]
    past = cache_mla_ckv.shape[2] - n_meta
    depth = norm_gains.shape[0]
    assert n_meta == N_META and ls == n_meta and n_meta <= CHUNK
    assert past % CHUNK == 0 and ls <= CHUNK and lp % CHUNK == 0
    ns = bs + 1
    hw = MLA_HEADS * QK_NOPE

    hp = x_prompt.reshape(bp * lp, d)
    hs = jnp.concatenate([x_sample.reshape(bs * ls, d), meta_tokens.astype(F32)], axis=0)
    meta_rows = slice(bs * ls, bs * ls + n_meta)

    pos_p = n_meta + jnp.arange(lp)
    pos_s = jnp.concatenate([jnp.tile(n_meta + past + jnp.arange(ls), bs), jnp.arange(n_meta)])
    tab_p = _rope_tables(pos_p)
    tab_s = _rope_tables(pos_s)
    zero_d = jnp.zeros((1, d), F32)
    wg_all, wu_all, wd_all = w_gate.astype(BF16), w_up.astype(BF16), w_down.astype(BF16)

    outs = {k: [] for k in ("p_ckv", "p_kr", "p_s", "p_conv", "p_cf", "s_ckv", "s_kr", "s_s", "s_conv", "s_cf")}
    for layer in range(depth):
        ng = norm_gains[layer].astype(F32)
        g0, g1, g2, g3 = (ng[i:i + 1] for i in range(4))
        if layer % 2 == 0:
            e = layer // 2
            offs = np.cumsum((Q_LORA, KV_LORA, QK_ROPE, DN_QKV, DN_HEADS * DN_DV, DN_HEADS, DN_HEADS))
            wi = w_in[e]
            w_qd, w_kvd, w_kr = wi[:, :offs[0]], wi[:, offs[0]:offs[1]], wi[:, offs[1]:offs[2]]
            w_qkv, w_z = wi[:, offs[2]:offs[3]], wi[:, offs[3]:offs[4]]
            w_a, w_b = wi[:, offs[4]:offs[5]], wi[:, offs[5]:offs[6]]
            w_proj = jnp.concatenate(
                [w_qkv, w_z, w_qd, w_kvd, w_kr, _rot_cols(w_kr), w_a, w_b,
                 jnp.zeros((d, SMALL_W - 2 * QK_ROPE - 2 * DN_HEADS), F32)], axis=1).astype(BF16)
            zero_proj = jnp.zeros((1, PROJ_W), F32)
            wq3 = w_uq[e]
            wq_n = wq3[:, :, :QK_NOPE].reshape(Q_LORA, hw)
            wq_r = wq3[:, :, QK_NOPE:]
            pad_r = lambda w: jnp.pad(w, ((0, 0), (0, 0), (0, LANE - QK_ROPE))).reshape(Q_LORA, hw)
            wq = jnp.concatenate([wq_n, pad_r(wq_r), pad_r(_rot_cols(wq_r))], axis=1).astype(BF16)
            wk = w_uk[e].reshape(KV_LORA, hw).astype(BF16)
            wv = w_uv[e].reshape(KV_LORA, hw).astype(BF16)
            wkv = jnp.concatenate([wk, wv], axis=1)
            gq, gkv = _row(mla_gq[e]), _row(mla_gkv[e])
            cw = jnp.pad(dn_conv_w[e].astype(F32), ((0, 8 - DN_CONV), (0, 0)))
            ab = jnp.concatenate([_row(dn_a_log[e], LANE), _row(dn_dt_bias[e], LANE),
                                  jnp.zeros((6, LANE), F32)], axis=0)
            nw = _row(dn_norm_w[e])
            wo = w_out[e].astype(BF16)
            wo_mla, wo_dn = wo[:hw], wo[hw:]

            proj_s = norm_matmul(hs, g0, w_proj, zero_proj, PROJ_TM, PROJ_TN)
            qn_s, qr_s, ckv_s, kn_s, v_s, kro_s, krp_s = mla_prep(proj_s, gq, gkv, wq, wk, wv, *tab_s,
                                                                  MLA_TM, False)
            lh = n_meta + past
            lh_pad = -(-lh // LANE) * LANE
            hist_ckv = jnp.pad(cache_mla_ckv[e].astype(F32), ((0, 1), (0, lh_pad - lh), (0, 0)))
            hkn, hv = kv_up(hist_ckv.reshape(ns * lh_pad, KV_LORA), wkv, lh_pad)
            hkrp = jnp.pad(cache_mla_krope[e].astype(BF16),
                           ((0, 1), (0, lh_pad - lh), (0, LANE - QK_ROPE)))
            hlen_s = jnp.concatenate([jnp.full((bs,), lh, jnp.int32), jnp.zeros((1,), jnp.int32)])
            ymla_s = attention(qn_s, qr_s, kn_s, krp_s, v_s, hkn.reshape(ns, lh_pad, hw), hkrp,
                               hv.reshape(ns, lh_pad, hw), hlen_s, ns, ls, ls)
            conv_hist_s = jnp.pad(state_dn_conv[e].astype(F32), ((0, 1), (8 - (DN_CONV - 1), 0), (0, 0)))
            s0_s = jnp.pad(state_dn_s[e].astype(F32), ((0, 1), (0, 0), (0, 0), (0, 0)))
            ydn_s, sfin_s = deltanet(proj_s, cw, conv_hist_s, s0_s, ab, nw, ns, ls, CHUNK, 1)
            hs = matmul_resnorm([ymla_s, ydn_s], [wo_mla, wo_dn], zero_d, g1, hs, RES_TM)

            proj_p = norm_matmul(hp, g0, w_proj, zero_proj, PROJ_TM, PROJ_TN)
            qn_p, qr_p, ckv_p, kn_p, vt_p, kro_p, krp_p = mla_prep(proj_p, gq, gkv, wq, wk, wv.T, *tab_p,
                                                                   MLA_TM, True)
            hlen_p = jnp.full((bp,), n_meta, jnp.int32)
            ymla_p = attention_t(qn_p, qr_p, kn_p, krp_p, vt_p, kn_s[meta_rows], krp_s[meta_rows],
                                 v_s[meta_rows].T, hlen_p, bp, lp)
            conv_hist_p = jnp.pad(proj_s[meta_rows, COL_QKV:COL_QKV + DN_QKV][-(DN_CONV - 1):],
                                  ((8 - (DN_CONV - 1), 0), (0, 0)))[None]
            ydn_p, sfin_p = deltanet(proj_p, cw, conv_hist_p, sfin_s[bs:], ab, nw, bp, lp, CHUNK,
                                     DN_BLOCKS_PER_STEP)
            hp = matmul_resnorm([ymla_p, ydn_p], [wo_mla, wo_dn], zero_d, g1, hp, RES_TM)

            bc = lambda a: jnp.broadcast_to(a[None], (bp,) + a.shape)
            outs["p_ckv"].append(jnp.concatenate([bc(ckv_s[meta_rows]), ckv_p.reshape(bp, lp, KV_LORA)], axis=1))
            outs["p_kr"].append(jnp.concatenate([bc(kro_s[meta_rows]), kro_p.reshape(bp, lp, QK_ROPE)], axis=1))
            outs["p_s"].append(sfin_p)
            outs["p_conv"].append(proj_p.reshape(bp, lp, PROJ_W)[:, lp - (DN_CONV - 1):, COL_QKV:COL_QKV + DN_QKV])
            outs["s_ckv"].append(ckv_s[:bs * ls].reshape(bs, ls, KV_LORA))
            outs["s_kr"].append(kro_s[:bs * ls].reshape(bs, ls, QK_ROPE))
            outs["s_s"].append(sfin_s[:bs])
            xqkv_s = proj_s[:bs * ls, COL_QKV:COL_QKV + DN_QKV].reshape(bs, ls, DN_QKV)
            outs["s_conv"].append(jnp.concatenate([state_dn_conv[e].astype(F32), xqkv_s], axis=1)[:, -(DN_CONV - 1):])
        else:
            o = layer // 2
            w1 = cf_w_pw1[o].astype(BF16)
            b1 = _row(cf_b_pw1[o])
            wdw = jnp.broadcast_to(cf_w_dw[o].astype(F32)[:, None, :], (CF_KERNEL, 8, d))
            bdw, lg, lb = _row(cf_b_dw[o]), _row(cf_ln_g[o]), _row(cf_ln_b[o])
            w2 = cf_w_pw2[o].astype(BF16)
            b2 = _row(cf_b_pw2[o])
            keep = CF_KERNEL - 1

            hist_s = jnp.pad(state_cf_conv[o].astype(F32), ((0, 1), (CF_HALO - keep, 0), (0, 0)))
            c_s, tail_s = glu_conv(hs, g0, w1, b1, hist_s, wdw, bdw, ns, ls, GLU_TM, GLU_TN, GLU_CHUNK)
            hs = ln_matmul_resnorm(c_s, lg, lb, w2, b2, g1, hs, RES_TM, LN_CHUNK)

            c_p, tail_p = glu_conv(hp, g0, w1, b1, tail_s[bs:], wdw, bdw, bp, lp, GLU_TM, GLU_TN, GLU_CHUNK)
            hp = ln_matmul_resnorm(c_p, lg, lb, w2, b2, g1, hp, RES_TM, LN_CHUNK)

            outs["p_cf"].append(tail_p[:, CF_HALO - keep:])
            outs["s_cf"].append(tail_s[:bs, CF_HALO - keep:])
        hs = ffn(hs, g2, wg_all, wu_all, wd_all, g3, layer, FFN_TM, FFN_TF)
        hp = ffn(hp, g2, wg_all, wu_all, wd_all, g3, layer, FFN_TM, FFN_TF)

    y_prompt = hp.reshape(bp, lp, d)
    y_sample = hs[:bs * ls].reshape(bs, ls, d)
    st = lambda k: jnp.stack(outs[k])
    return (y_prompt, y_sample, st("p_ckv"), st("p_kr"), st("p_s"), st("p_conv"), st("p_cf"),
            st("s_ckv"), st("s_kr"), st("s_s"), st("s_conv"), st("s_cf"))
```
